```python
import jax, jax.numpy as jnp
from jax import lax
import numpy as np

D_MODEL = 1024
BATCH = 16
SEQ = 4096
DEPTH = 4

CHUNK = 64
BRANCH_W = 512
N_BRANCH = 3
EPS = 1e-6
SG_BLOCK = 128
SG_GROUPS = 4
SG_GROUP_W = BRANCH_W // SG_GROUPS
MLA_HEADS = 8
MLA_NOPE = 64
MLA_ROPE = 32
MLA_QK = MLA_NOPE + MLA_ROPE
MLA_V = 64
MLA_Q_RANK = 256
MLA_KV_RANK = 128
ROPE_THETA = 10000.0
Q_BLOCK = 128
GLA_HEADS = 4
GLA_DK = 64
GLA_DV = 128
GLA_GATE_RANK = 16
GLA_TAU = 16.0

SPLITS = (BRANCH_W, BRANCH_W, BRANCH_W,
          MLA_Q_RANK, MLA_KV_RANK, MLA_ROPE, MLA_HEADS * MLA_V,
          GLA_HEADS * GLA_DK, GLA_HEADS * GLA_DK, GLA_HEADS * GLA_DV,
          GLA_GATE_RANK, GLA_HEADS * GLA_DV,
          N_BRANCH * D_MODEL)
IN_COLS = sum(SPLITS)

kernel_name = 'hybrid_gmlp_mla_gla_gated_merge'


def _rmsnorm(x, g):
    x32 = x.astype(jnp.float32)
    y = x32 * lax.rsqrt(jnp.mean(x32 * x32, axis=-1, keepdims=True) + EPS)
    return (y * g.astype(jnp.float32)).astype(x.dtype)


def _layernorm(x, g, b):
    x32 = x.astype(jnp.float32)
    mu = jnp.mean(x32, axis=-1, keepdims=True)
    xc = x32 - mu
    y = xc * lax.rsqrt(jnp.mean(xc * xc, axis=-1, keepdims=True) + EPS)
    return (y * g.astype(jnp.float32) + b.astype(jnp.float32)).astype(x.dtype)


def _split_cols(proj):
    idx = np.cumsum(np.array(SPLITS))[:-1].tolist()
    return jnp.split(proj, idx, axis=-1)


def _rope_tables(positions):
    half = MLA_ROPE // 2
    inv = 1.0 / (ROPE_THETA ** (jnp.arange(half, dtype=jnp.float32) * 2.0 / MLA_ROPE))
    ang = positions.astype(jnp.float32)[..., None] * inv
    return jnp.cos(ang)[:, :, None, :], jnp.sin(ang)[:, :, None, :]


def _apply_rope(x, cos, sin):
    half = MLA_ROPE // 2
    x32 = x.astype(jnp.float32)
    x1, x2 = x32[..., :half], x32[..., half:]
    return jnp.concatenate([x1 * cos - x2 * sin, x1 * sin + x2 * cos], axis=-1).astype(x.dtype)


def _spatial_gating(u, v, z, ln_g, ln_b, w_s, b_s):
    bn, s, _ = u.shape
    u = jax.nn.gelu(u)
    v = _layernorm(jax.nn.gelu(v), ln_g, ln_b)
    nb = s // SG_BLOCK
    vb = v.reshape(bn, nb, SG_BLOCK, SG_GROUPS, SG_GROUP_W)
    cid = jnp.arange(SG_BLOCK) // CHUNK
    mask = cid[:, None] >= cid[None, :]
    w = jnp.where(mask[None], w_s, jnp.zeros_like(w_s))
    sv = jnp.einsum('gij,bnjgc->bnigc', w, vb) + b_s.T[None, None, :, :, None]
    return u * sv.reshape(bn, s, BRANCH_W) * jax.nn.silu(z)


def _chunk_causal_attention(q, k, v):
    bn, s, h, dq = q.shape
    dv = v.shape[-1]
    nb = s // Q_BLOCK
    scale = dq ** -0.5
    qb = q.reshape(bn, nb, Q_BLOCK, h, dq).transpose(1, 0, 3, 2, 4)
    kh = k.transpose(0, 2, 1, 3)
    vh = v.transpose(0, 2, 1, 3)
    k_chunk = jnp.arange(s) // CHUNK

    def one_block(args):
        qblk, bi = args
        sc = jnp.einsum('bhqd,bhkd->bhqk', qblk, kh).astype(jnp.float32) * scale
        q_chunk = (bi * Q_BLOCK + jnp.arange(Q_BLOCK)) // CHUNK
        mask = k_chunk[None, :] <= q_chunk[:, None]
        sc = jnp.where(mask, sc, -jnp.inf)
        p = jax.nn.softmax(sc, axis=-1)
        return jnp.einsum('bhqk,bhkd->bhqd', p.astype(vh.dtype), vh)

    out = lax.map(one_block, (qb, jnp.arange(nb)))
    return out.transpose(1, 0, 3, 2, 4).reshape(bn, s, h, dv)


def _mla(c_q, c_kv, k_r, z, cq_g, ckv_g, w_uq, w_ukv, q_g, k_g, cos, sin):
    bn, s, _ = c_q.shape
    q = (_rmsnorm(c_q, cq_g) @ w_uq).reshape(bn, s, MLA_HEADS, MLA_QK)
    kv = (_rmsnorm(c_kv, ckv_g) @ w_ukv).reshape(bn, s, MLA_HEADS, MLA_NOPE + MLA_V)
    k_nope, v = kv[..., :MLA_NOPE], kv[..., MLA_NOPE:]
    k_rope = jnp.broadcast_to(k_r[:, :, None, :], (bn, s, MLA_HEADS, MLA_ROPE))
    k = jnp.concatenate([k_nope, k_rope], axis=-1)
    q = _rmsnorm(q, q_g)
    k = _rmsnorm(k, k_g)
    q = jnp.concatenate([q[..., :MLA_NOPE], _apply_rope(q[..., MLA_NOPE:], cos, sin)], axis=-1)
    k = jnp.concatenate([k[..., :MLA_NOPE], _apply_rope(k[..., MLA_NOPE:], cos, sin)], axis=-1)
    o = _chunk_causal_attention(q, k, v)
    return o.reshape(bn, s, MLA_HEADS * MLA_V) * jax.nn.silu(z)


def _gla(q, k, v, g_lr, z, w_gu, b_gu, o_g):
    bn, s, _ = q.shape
    n = s // CHUNK
    f32 = jnp.float32
    qc = q.astype(f32).reshape(bn, n, CHUNK, GLA_HEADS, GLA_DK) * (GLA_DK ** -0.5)
    kc = k.astype(f32).reshape(bn, n, CHUNK, GLA_HEADS, GLA_DK)
    vc = v.astype(f32).reshape(bn, n, CHUNK, GLA_HEADS, GLA_DV)
    log_a = jax.nn.log_sigmoid((g_lr @ w_gu + b_gu).astype(f32)) / GLA_TAU
    log_a = log_a.reshape(bn, n, CHUNK, GLA_HEADS, GLA_DK)
    b = jnp.cumsum(log_a, axis=2)
    b_last = b[:, :, -1]
    q_t = qc * jnp.exp(b)
    k_t = kc * jnp.exp(-b)
    k_s = kc * jnp.exp(b_last[:, :, None] - b)
    causal = jnp.tril(jnp.ones((CHUNK, CHUNK), dtype=bool))
    att = jnp.einsum('bnihd,bnjhd->bnhij', q_t, k_t)
    att = jnp.where(causal, att, 0.0)
    o_intra = jnp.einsum('bnhij,bnjhe->bnihe', att, vc)

    def step(state, xs):
        q_i, k_i, v_i, dec = xs
        o_i = jnp.einsum('bihd,bhde->bihe', q_i, state)
        new_state = dec[..., None] * state + jnp.einsum('bjhd,bjhe->bhde', k_i, v_i)
        return new_state, o_i

    s0 = jnp.zeros((bn, GLA_HEADS, GLA_DK, GLA_DV), f32)
    xs = (jnp.moveaxis(q_t, 1, 0), jnp.moveaxis(k_s, 1, 0), jnp.moveaxis(vc, 1, 0), jnp.moveaxis(jnp.exp(b_last), 1, 0))
    _, o_inter = lax.scan(step, s0, xs)
    o = (o_intra + jnp.moveaxis(o_inter, 0, 1)).reshape(bn, s, GLA_HEADS, GLA_DV)
    o = _rmsnorm(o, o_g).reshape(bn, s, GLA_HEADS * GLA_DV).astype(z.dtype)
    return o * jax.nn.silu(z)


def setup_inputs(seed: int = 0) -> dict:
    key = jax.random.key(seed)
    ks = jax.random.split(key, 24)
    f32 = jnp.float32
    nrm = lambda k, shp, sc: jax.random.normal(k, shp, f32) * sc
    gain = lambda k, shp: 1.0 + 0.02 * jax.random.normal(k, shp, f32)
    x = jax.random.normal(ks[0], (BATCH, SEQ, D_MODEL), f32)
    offset = jax.random.randint(ks[1], (BATCH, 1), 0, 4096, dtype=jnp.int32)
    positions = offset + jnp.arange(SEQ, dtype=jnp.int32)[None, :]
    return {
        'x': x,
        'positions': positions,
        'norm_g': gain(ks[2], (DEPTH, D_MODEL)),
        'w_in': nrm(ks[3], (DEPTH, D_MODEL, IN_COLS), D_MODEL ** -0.5),
        'b_gate': nrm(ks[4], (DEPTH, N_BRANCH * D_MODEL), 0.1),
        'sg_ln_g': gain(ks[5], (DEPTH, BRANCH_W)),
        'sg_ln_b': nrm(ks[6], (DEPTH, BRANCH_W), 0.02),
        'sg_w': nrm(ks[7], (DEPTH, SG_GROUPS, SG_BLOCK, SG_BLOCK), SG_BLOCK ** -0.5),
        'sg_b': gain(ks[8], (DEPTH, SG_GROUPS, SG_BLOCK)),
        'mla_cq_g': gain(ks[9], (DEPTH, MLA_Q_RANK)),
        'mla_ckv_g': gain(ks[10], (DEPTH, MLA_KV_RANK)),
        'mla_w_uq': nrm(ks[11], (DEPTH, MLA_Q_RANK, MLA_HEADS * MLA_QK), MLA_Q_RANK ** -0.5),
        'mla_w_ukv': nrm(ks[12], (DEPTH, MLA_KV_RANK, MLA_HEADS * (MLA_NOPE + MLA_V)), MLA_KV_RANK ** -0.5),
        'mla_q_g': gain(ks[13], (DEPTH, MLA_QK)),
        'mla_k_g': gain(ks[14], (DEPTH, MLA_QK)),
        'gla_w_gate': nrm(ks[15], (DEPTH, GLA_GATE_RANK, GLA_HEADS * GLA_DK), GLA_GATE_RANK ** -0.5),
        'gla_b_gate': nrm(ks[16], (DEPTH, GLA_HEADS * GLA_DK), 0.1),
        'gla_o_g': gain(ks[17], (DEPTH, GLA_DV)),
        'w_branch': nrm(ks[18], (DEPTH, N_BRANCH, BRANCH_W, D_MODEL), BRANCH_W ** -0.5),
        'w_out': nrm(ks[19], (DEPTH, D_MODEL, D_MODEL), D_MODEL ** -0.5),
    }


def reference(x, positions, norm_g, w_in, b_gate, sg_ln_g, sg_ln_b, sg_w, sg_b, mla_cq_g, mla_ckv_g,
              mla_w_uq, mla_w_ukv, mla_q_g, mla_k_g, gla_w_gate, gla_b_gate, gla_o_g, w_branch, w_out):
    bn, s, d = x.shape
    cos, sin = _rope_tables(positions)
    for l in range(DEPTH):
        h = _rmsnorm(x, norm_g[l])
        (u_a, v_a, z_a, c_q, c_kv, k_r, z_b, q_c, k_c, v_c, g_c, z_c, gate_logits) = _split_cols(h @ w_in[l])
        y_a = _spatial_gating(u_a, v_a, z_a, sg_ln_g[l], sg_ln_b[l], sg_w[l], sg_b[l])
        y_b = _mla(c_q, c_kv, k_r, z_b, mla_cq_g[l], mla_ckv_g[l], mla_w_uq[l], mla_w_ukv[l],
                   mla_q_g[l], mla_k_g[l], cos, sin)
        y_c = _gla(q_c, k_c, v_c, g_c, z_c, gla_w_gate[l], gla_b_gate[l], gla_o_g[l])
        gates = jax.nn.sigmoid(gate_logits + b_gate[l]).reshape(bn, s, N_BRANCH, d)
        merged = (gates[:, :, 0] * (y_a @ w_branch[l, 0])
                  + gates[:, :, 1] * (y_b @ w_branch[l, 1])
                  + gates[:, :, 2] * (y_c @ w_branch[l, 2]))
        x = x + merged @ w_out[l]
    return x
```

```python
import functools
import math

import jax
import jax.numpy as jnp
from jax import lax
from jax.experimental import pallas as pl
from jax.experimental.pallas import tpu as pltpu

F32 = jnp.float32
BF16 = jnp.bfloat16

D_MODEL = 1024
CHUNK = 64
BRANCH_W = 512
N_BRANCH = 3
EPS = 1e-6
SG_BLOCK = 128
SG_GROUPS = 4
MLA_HEADS = 8
MLA_NOPE = 64
MLA_ROPE = 32
MLA_QK = MLA_NOPE + MLA_ROPE
MLA_V = 64
MLA_Q_RANK = 256
MLA_KV_RANK = 128
ROPE_THETA = 10000.0
GLA_HEADS = 4
GLA_DK = 64
GLA_DV = 128
GLA_GATE_RANK = 16
GLA_TAU = 16.0

LANES = 128
HEAD_PAD = LANES
ROPE_HALF = MLA_ROPE // 2
X1_LO = MLA_NOPE
X2_LO = MLA_NOPE + ROPE_HALF

A_U, A_V, A_Z = 0, 512, 1024
B_CQ, B_CKV, B_KR, B_Z = 1536, 1792, 1920, 2048
C_Q, C_K, C_V, C_G, C_Z = 2560, 2816, 3072, 3584, 3712
GATE = 4224
IN_COLS_PAD = GATE + N_BRANCH * D_MODEL

TOKEN_TILE = 256
ATTN_TILE = 256
GLA_TILE = 256
VMEM_LIMIT = 56 * 1024 * 1024


def _const_spec(shape):
    zeros = (0,) * len(shape)
    return pl.BlockSpec(shape, lambda *_: zeros, pipeline_mode=pl.Buffered(1))


def _sigmoid(x):
    return 1.0 / (1.0 + jnp.exp(-x))


def _silu(x):
    return x * _sigmoid(x)


def _gelu_tanh(x):
    c = math.sqrt(2.0 / math.pi)
    return 0.5 * x * (1.0 + jnp.tanh(c * (x + 0.044715 * (x * x * x))))


def _log_sigmoid(x):
    return jnp.minimum(x, 0.0) - jnp.log(1.0 + jnp.exp(-jnp.abs(x)))


def _rms(x, width):
    return lax.rsqrt(jnp.sum(x * x, axis=-1, keepdims=True) * (1.0 / width) + EPS)


def _rope_body(pos_ref, cos_ref, sin_ref):
    lane = lax.broadcasted_iota(jnp.int32, (1, LANES), 1)
    in_x1 = (lane >= X1_LO) & (lane < X2_LO)
    in_x2 = (lane >= X2_LO) & (lane < MLA_QK)
    fidx = jnp.where(in_x1, lane - X1_LO, lane - X2_LO).astype(F32)
    inv = 1.0 / jnp.exp(fidx * (2.0 / MLA_ROPE) * math.log(ROPE_THETA))
    ang = pos_ref[...].astype(F32) * inv
    cos = jnp.cos(ang)
    sin = jnp.sin(ang)
    rope = in_x1 | in_x2
    cos_ref[...] = jnp.where(rope, cos, jnp.where(lane < MLA_NOPE, 1.0, 0.0))
    sin_ref[...] = jnp.where(in_x1, -sin, jnp.where(in_x2, sin, 0.0))


def _rope_tables(positions):
    t = positions.size
    tm = min(t, 1024)
    return pl.pallas_call(
        _rope_body,
        grid=(t // tm,),
        in_specs=[pl.BlockSpec((tm, 1), lambda i: (i, 0))],
        out_specs=[pl.BlockSpec((tm, LANES), lambda i: (i, 0))] * 2,
        out_shape=[jax.ShapeDtypeStruct((t, LANES), F32)] * 2,
        name="rope_tables",
    )(positions.reshape(t, 1))


def _in_proj_body(x_ref, cos_ref, sin_ref, ng_ref, win_ref, lng_ref, lnb_ref, sgw_ref, sgbt_ref,
                  cqg_ref, ckvg_ref, wuq_ref, wuk_ref, wuv_ref, qg_ref, kg_ref, wgu_ref, bgu_ref,
                  bgate_ref,
                  ya_ref, qb_ref, kb_ref, vb_ref, zb_ref, qc_ref, kc_ref, vc_ref, la_ref, zc_ref,
                  gate_ref):
    tm = x_ref.shape[0]
    x = x_ref[...]
    h = (x * _rms(x, D_MODEL) * ng_ref[...]).astype(BF16)

    def proj(lo, width):
        return jnp.dot(h, win_ref[:, lo:lo + width], preferred_element_type=F32)

    u = _gelu_tanh(proj(A_U, BRANCH_W))
    v = _gelu_tanh(proj(A_V, BRANCH_W))
    mu = jnp.mean(v, axis=-1, keepdims=True)
    vc = v - mu
    vn = vc * lax.rsqrt(jnp.mean(vc * vc, axis=-1, keepdims=True) + EPS)
    vn = (vn * lng_ref[...] + lnb_ref[...]).astype(BF16)
    uz = u * _silu(proj(A_Z, BRANCH_W))
    ri = lax.broadcasted_iota(jnp.int32, (SG_BLOCK, SG_BLOCK), 0) // CHUNK
    ci = lax.broadcasted_iota(jnp.int32, (SG_BLOCK, SG_BLOCK), 1) // CHUNK
    chunk_causal = ri >= ci
    for g in range(SG_GROUPS):
        wg = jnp.where(chunk_causal, sgw_ref[g], 0.0).astype(BF16)
        bias = sgbt_ref[:, g:g + 1]
        cols = slice(g * LANES, (g + 1) * LANES)
        for n in range(tm // SG_BLOCK):
            rows = slice(n * SG_BLOCK, (n + 1) * SG_BLOCK)
            sv = jnp.dot(wg, vn[rows, cols], preferred_element_type=F32) + bias
            ya_ref[rows, cols] = (uz[rows, cols] * sv).astype(BF16)

    cq = proj(B_CQ, MLA_Q_RANK)
    cqn = (cq * _rms(cq, MLA_Q_RANK) * cqg_ref[...]).astype(BF16)
    q = jnp.dot(cqn, wuq_ref[...], preferred_element_type=F32)
    ckv = proj(B_CKV, MLA_KV_RANK)
    ckvn = (ckv * _rms(ckv, MLA_KV_RANK) * ckvg_ref[...]).astype(BF16)
    kn = jnp.dot(ckvn, wuk_ref[...], preferred_element_type=F32)
    vb_ref[...] = jnp.dot(ckvn, wuv_ref[...], preferred_element_type=F32).astype(BF16)
    kr = proj(B_KR, LANES)
    cos = cos_ref[...]
    sin = sin_ref[...]
    lane = lax.broadcasted_iota(jnp.int32, (1, LANES), 1)
    in_x1 = (lane >= X1_LO) & (lane < X2_LO)

    def norm_rope(xh, gain):
        xn = xh * _rms(xh, MLA_QK) * gain
        partner = jnp.where(in_x1, pltpu.roll(xn, LANES - ROPE_HALF, 1), pltpu.roll(xn, ROPE_HALF, 1))
        return (xn * cos + partner * sin).astype(BF16)

    for hd in range(MLA_HEADS):
        cols = slice(hd * HEAD_PAD, (hd + 1) * HEAD_PAD)
        qb_ref[:, cols] = norm_rope(q[:, cols], qg_ref[...])
        kb_ref[:, cols] = norm_rope(kn[:, cols] + kr, kg_ref[...])
    zb_ref[...] = _silu(proj(B_Z, BRANCH_W)).astype(BF16)

    qc_ref[...] = (proj(C_Q, GLA_HEADS * GLA_DK) * (GLA_DK ** -0.5)).astype(BF16)
    kc_ref[...] = proj(C_K, GLA_HEADS * GLA_DK).astype(BF16)
    vc_ref[...] = proj(C_V, GLA_HEADS * GLA_DV).astype(BF16)
    glr = proj(C_G, LANES).astype(BF16)
    gl = jnp.dot(glr, wgu_ref[...], preferred_element_type=F32) + bgu_ref[...]
    la_ref[...] = _log_sigmoid(gl) * (1.0 / GLA_TAU)
    zc_ref[...] = _silu(proj(C_Z, BRANCH_W)).astype(BF16)

    for n in range(N_BRANCH):
        cols = slice(n * D_MODEL, (n + 1) * D_MODEL)
        logits = proj(GATE + n * D_MODEL, D_MODEL) + bgate_ref[:, cols]
        gate_ref[:, cols] = _sigmoid(logits).astype(BF16)


def _in_proj(x2, cos_t, sin_t, p):
    t = x2.shape[0]
    tm = TOKEN_TILE
    row = lambda w: pl.BlockSpec((tm, w), lambda i: (i, 0))
    consts = [p["norm_g"], p["w_in"], p["sg_ln_g"], p["sg_ln_b"], p["sg_w"], p["sg_bt"],
              p["cq_g"], p["ckv_g"], p["w_uq"], p["w_uk"], p["w_uv"], p["q_g"], p["k_g"],
              p["w_gu"], p["b_gu"], p["b_gate"]]
    out_widths = [(BRANCH_W, BF16), (MLA_HEADS * HEAD_PAD, BF16), (MLA_HEADS * HEAD_PAD, BF16),
                  (MLA_HEADS * HEAD_PAD, BF16), (BRANCH_W, BF16),
                  (GLA_HEADS * GLA_DK, BF16), (GLA_HEADS * GLA_DK, BF16), (GLA_HEADS * GLA_DV, BF16),
                  (GLA_HEADS * GLA_DK, F32), (BRANCH_W, BF16), (N_BRANCH * D_MODEL, BF16)]
    return pl.pallas_call(
        _in_proj_body,
        grid=(t // tm,),
        in_specs=[row(D_MODEL), row(LANES), row(LANES)] + [_const_spec(c.shape) for c in consts],
        out_specs=[row(w) for w, _ in out_widths],
        out_shape=[jax.ShapeDtypeStruct((t, w), dt) for w, dt in out_widths],
        compiler_params=pltpu.CompilerParams(dimension_semantics=("parallel",),
                                             vmem_limit_bytes=VMEM_LIMIT),
        name="in_proj",
    )(x2, cos_t, sin_t, *consts)


def _gla_body(q_ref, k_ref, v_ref, la_ref, z_ref, og_ref, y_ref, state_ref):
    @pl.when(pl.program_id(1) == 0)
    def _():
        state_ref[...] = jnp.zeros_like(state_ref)

    tc = q_ref.shape[0]
    kw = GLA_HEADS * GLA_DK
    ri = lax.broadcasted_iota(jnp.int32, (CHUNK, CHUNK), 0)
    ci = lax.broadcasted_iota(jnp.int32, (CHUNK, CHUNK), 1)
    causal = ri >= ci
    tri = jnp.where(causal, 1.0, 0.0).astype(BF16)
    lane = lax.broadcasted_iota(jnp.int32, (1, kw), 1)
    nt = (((1,), (1,)), ((), ()))
    tn = (((0,), (0,)), ((), ()))
    for c in range(tc // CHUNK):
        rows = slice(c * CHUNK, (c + 1) * CHUNK)
        la = la_ref[rows, :]
        la_hi = la.astype(BF16)
        la_lo = (la - la_hi.astype(F32)).astype(BF16)
        b = (jnp.dot(tri, la_hi, preferred_element_type=F32)
             + jnp.dot(tri, la_lo, preferred_element_type=F32))
        b_last = b[CHUNK - 1:CHUNK, :]
        qt = q_ref[rows, :].astype(F32) * jnp.exp(b)
        kf = k_ref[rows, :].astype(F32)
        kt = (kf * jnp.exp(-b)).astype(BF16)
        ks = (kf * jnp.exp(b_last - b)).astype(BF16)
        dec = jnp.exp(b_last)
        for hd in range(GLA_HEADS):
            own = (lane >= hd * GLA_DK) & (lane < (hd + 1) * GLA_DK)
            qh = jnp.where(own, qt, 0.0).astype(BF16)
            vh = v_ref[rows, hd * GLA_DV:(hd + 1) * GLA_DV]
            att = lax.dot_general(qh, kt, nt, preferred_element_type=F32)
            att = jnp.where(causal, att, 0.0).astype(BF16)
            st = state_ref[hd]
            o = (jnp.dot(att, vh, preferred_element_type=F32)
                 + lax.dot_general(qh, st.astype(BF16), nt, preferred_element_type=F32))
            state_ref[hd] = st * dec + lax.dot_general(vh, ks, tn, preferred_element_type=F32)
            cols = slice(hd * GLA_DV, (hd + 1) * GLA_DV)
            on = o * _rms(o, GLA_DV) * og_ref[...]
            y_ref[rows, cols] = (on * z_ref[rows, cols].astype(F32)).astype(BF16)


def _gla(qc, kc, vc, la, zc, o_g, batch, seq):
    tc = GLA_TILE
    ns = seq // tc
    row = lambda w: pl.BlockSpec((tc, w), lambda b, s: (b * ns + s, 0))
    kw = GLA_HEADS * GLA_DK
    vw = GLA_HEADS * GLA_DV
    return pl.pallas_call(
        _gla_body,
        grid=(batch, ns),
        in_specs=[row(kw), row(kw), row(vw), row(kw), row(vw), _const_spec(o_g.shape)],
        out_specs=row(vw),
        out_shape=jax.ShapeDtypeStruct((batch * seq, vw), BF16),
        scratch_shapes=[pltpu.VMEM((GLA_HEADS, GLA_DV, kw), F32)],
        compiler_params=pltpu.CompilerParams(dimension_semantics=("parallel", "arbitrary"),
                                             vmem_limit_bytes=VMEM_LIMIT),
        name="gla",
    )(qc, kc, vc, la, zc, o_g)


def _attn_body(q_ref, k_ref, v_ref, z_ref, y_ref, m_ref, l_ref, acc_ref):
    tq = q_ref.shape[0]
    tk = tq
    i = pl.program_id(1)
    nt = (((1,), (1,)), ((), ()))
    ri = lax.broadcasted_iota(jnp.int32, (tq, tk), 0) // CHUNK
    ci = lax.broadcasted_iota(jnp.int32, (tq, tk), 1) // CHUNK
    visible = ci <= ri
    lane = lax.broadcasted_iota(jnp.int32, (1, LANES), 1)

    def head_out(hd):
        cols = slice(hd * HEAD_PAD, (hd + 1) * HEAD_PAD)
        qh = q_ref[:, cols]
        m_ref[...] = jnp.full(m_ref.shape, -jnp.inf, F32)
        l_ref[...] = jnp.zeros(l_ref.shape, F32)
        acc_ref[...] = jnp.zeros(acc_ref.shape, F32)

        def step(j, masked):
            start = pl.multiple_of(j * tk, tk)
            kblk = k_ref[pl.ds(start, tk), cols]
            vblk = v_ref[pl.ds(start, tk), cols]
            s = lax.dot_general(qh, kblk, nt, preferred_element_type=F32)
            if masked:
                s = jnp.where(visible, s, -jnp.inf)
            m_prev = m_ref[...]
            m_new = jnp.maximum(m_prev, jnp.max(s, axis=-1, keepdims=True))
            alpha = jnp.exp(m_prev - m_new)
            p = jnp.exp(s - m_new)
            l_ref[...] = alpha * l_ref[...] + jnp.sum(p, axis=-1, keepdims=True)
            acc_ref[...] = alpha * acc_ref[...] + jnp.dot(p.astype(BF16), vblk,
                                                         preferred_element_type=F32)
            m_ref[...] = m_new

        def body(j, carry):
            step(j, False)
            return carry

        lax.fori_loop(0, i, body, 0)
        step(i, True)
        return acc_ref[...] / l_ref[...]

    for pair in range(MLA_HEADS // 2):
        o0 = head_out(2 * pair)
        o1 = head_out(2 * pair + 1)
        o = jnp.where(lane < MLA_V, o0, pltpu.roll(o1, MLA_V, 1))
        cols = slice(pair * LANES, (pair + 1) * LANES)
        y_ref[:, cols] = (o * z_ref[:, cols].astype(F32)).astype(BF16)


def _attn(qb, kb, vb, zb, batch, seq):
    tq = ATTN_TILE
    nq = seq // tq
    hw = MLA_HEADS * HEAD_PAD
    seq_spec = pl.BlockSpec((seq, hw), lambda b, i: (b, 0), pipeline_mode=pl.Buffered(1))
    return pl.pallas_call(
        _attn_body,
        grid=(batch, nq),
        in_specs=[pl.BlockSpec((tq, hw), lambda b, i: (b * nq + i, 0)), seq_spec, seq_spec,
                  pl.BlockSpec((tq, BRANCH_W), lambda b, i: (b * nq + i, 0))],
        out_specs=pl.BlockSpec((tq, BRANCH_W), lambda b, i: (b * nq + i, 0)),
        out_shape=jax.ShapeDtypeStruct((batch * seq, BRANCH_W), BF16),
        scratch_shapes=[pltpu.VMEM((tq, 1), F32), pltpu.VMEM((tq, 1), F32),
                        pltpu.VMEM((tq, HEAD_PAD), F32)],
        compiler_params=pltpu.CompilerParams(dimension_semantics=("parallel", "arbitrary"),
                                             vmem_limit_bytes=VMEM_LIMIT),
        name="mla_attn",
    )(qb, kb, vb, zb)


def _merge_body(x_ref, ya_ref, yb_ref, yc_ref, gate_ref, wbr_ref, wout_ref, o_ref):
    merged = None
    for n, y_ref in enumerate((ya_ref, yb_ref, yc_ref)):
        br = jnp.dot(y_ref[...], wbr_ref[n], preferred_element_type=F32)
        term = gate_ref[:, n * D_MODEL:(n + 1) * D_MODEL].astype(F32) * br
        merged = term if merged is None else merged + term
    o_ref[...] = x_ref[...] + jnp.dot(merged.astype(BF16), wout_ref[...], preferred_element_type=F32)


def _merge(x2, ya, yb, yc, gates, w_branch, w_out):
    t = x2.shape[0]
    tm = TOKEN_TILE
    row = lambda w: pl.BlockSpec((tm, w), lambda i: (i, 0))
    return pl.pallas_call(
        _merge_body,
        grid=(t // tm,),
        in_specs=[row(D_MODEL), row(BRANCH_W), row(BRANCH_W), row(BRANCH_W), row(N_BRANCH * D_MODEL),
                  _const_spec(w_branch.shape), _const_spec(w_out.shape)],
        out_specs=row(D_MODEL),
        out_shape=jax.ShapeDtypeStruct((t, D_MODEL), F32),
        compiler_params=pltpu.CompilerParams(dimension_semantics=("parallel",),
                                             vmem_limit_bytes=VMEM_LIMIT),
        name="merge",
    )(x2, ya, yb, yc, gates, w_branch, w_out)


def _pad_heads(w, heads, width):
    lead = w.shape[:-1]
    w = w.reshape(lead + (heads, width))
    w = jnp.pad(w, [(0, 0)] * len(lead) + [(0, 0), (0, HEAD_PAD - width)])
    return w.reshape(lead + (heads * HEAD_PAD,))


def _pack_layer(l, norm_g, w_in, b_gate, sg_ln_g, sg_ln_b, sg_w, sg_b, mla_cq_g, mla_ckv_g, mla_w_uq,
                mla_w_ukv, mla_q_g, mla_k_g, gla_w_gate, gla_b_gate, gla_o_g, w_branch, w_out):
    w = w_in[l]
    d = w.shape[0]
    zeros = lambda n: jnp.zeros((d, n), w.dtype)
    w_packed = jnp.concatenate([
        w[:, :1920], zeros(MLA_NOPE), w[:, 1920:1952], zeros(LANES - MLA_QK),
        w[:, 1952:3504], zeros(LANES - GLA_GATE_RANK), w[:, 3504:]], axis=1)
    ukv = mla_w_ukv[l].reshape(MLA_KV_RANK, MLA_HEADS, MLA_NOPE + MLA_V)
    w_uk = _pad_heads(ukv[:, :, :MLA_NOPE].reshape(MLA_KV_RANK, -1), MLA_HEADS, MLA_NOPE)
    w_uv = _pad_heads(ukv[:, :, MLA_NOPE:].reshape(MLA_KV_RANK, -1), MLA_HEADS, MLA_V)
    pad_gain = lambda g: jnp.pad(g, (0, HEAD_PAD - MLA_QK)).reshape(1, HEAD_PAD)
    return {
        "norm_g": norm_g[l].reshape(1, -1),
        "w_in": w_packed.astype(BF16),
        "sg_ln_g": sg_ln_g[l].reshape(1, -1),
        "sg_ln_b": sg_ln_b[l].reshape(1, -1),
        "sg_w": sg_w[l],
        "sg_bt": sg_b[l].T,
        "cq_g": mla_cq_g[l].reshape(1, -1),
        "ckv_g": mla_ckv_g[l].reshape(1, -1),
        "w_uq": _pad_heads(mla_w_uq[l], MLA_HEADS, MLA_QK).astype(BF16),
        "w_uk": w_uk.astype(BF16),
        "w_uv": w_uv.astype(BF16),
        "q_g": pad_gain(mla_q_g[l]) * (MLA_QK ** -0.5),
        "k_g": pad_gain(mla_k_g[l]),
        "w_gu": jnp.pad(gla_w_gate[l], ((0, LANES - GLA_GATE_RANK), (0, 0))).astype(BF16),
        "b_gu": gla_b_gate[l].reshape(1, -1),
        "b_gate": b_gate[l].reshape(1, -1),
        "o_g": gla_o_g[l].reshape(1, -1),
        "w_branch": w_branch[l].astype(BF16),
        "w_out": w_out[l].astype(BF16),
    }


def kernel(x, positions, norm_g, w_in, b_gate, sg_ln_g, sg_ln_b, sg_w, sg_b, mla_cq_g, mla_ckv_g, mla_w_uq, mla_w_ukv, mla_q_g, mla_k_g, gla_w_gate, gla_b_gate, gla_o_g, w_branch, w_out):
    batch, seq, d = x.shape
    assert d == D_MODEL and seq % max(ATTN_TILE, GLA_TILE, TOKEN_TILE) == 0
    depth = w_in.shape[0]
    cos_t, sin_t = _rope_tables(positions)
    x2 = x.reshape(batch * seq, d)
    for l in range(depth):
        p = _pack_layer(l, norm_g, w_in, b_gate, sg_ln_g, sg_ln_b, sg_w, sg_b, mla_cq_g, mla_ckv_g,
                        mla_w_uq, mla_w_ukv, mla_q_g, mla_k_g, gla_w_gate, gla_b_gate, gla_o_g,
                        w_branch, w_out)
        ya, qb, kb, vb, zb, qc, kc, vc, la, zc, gates = _in_proj(x2, cos_t, sin_t, p)
        yc = _gla(qc, kc, vc, la, zc, p["o_g"], batch, seq)
        yb = _attn(qb, kb, vb, zb, batch, seq)
        x2 = _merge(x2, ya, yb, yc, gates, p["w_branch"], p["w_out"])
    return x2.reshape(batch, seq, d)
```

```python
import functools
import math

import jax
import jax.numpy as jnp
from jax import lax
from jax.experimental import pallas as pl
from jax.experimental.pallas import tpu as pltpu

F32 = jnp.float32
BF16 = jnp.bfloat16

D_MODEL = 1024
CHUNK = 64
BRANCH_W = 512
N_BRANCH = 3
EPS = 1e-6
SG_BLOCK = 128
SG_GROUPS = 4
MLA_HEADS = 8
MLA_NOPE = 64
MLA_ROPE = 32
MLA_QK = MLA_NOPE + MLA_ROPE
MLA_V = 64
MLA_Q_RANK = 256
MLA_KV_RANK = 128
ROPE_THETA = 10000.0
GLA_HEADS = 4
GLA_DK = 64
GLA_DV = 128
GLA_GATE_RANK = 16
GLA_TAU = 16.0

LANES = 128
SUBLANES = 8
HEAD_PAD = LANES
ROPE_HALF = MLA_ROPE // 2
X1_LO = MLA_NOPE
X2_LO = MLA_NOPE + ROPE_HALF

A_U, A_V, A_Z = 0, 512, 1024
B_CQ, B_CKV, B_KR, B_Z = 1536, 1792, 1920, 2048
C_Q, C_K, C_V, C_G, C_Z = 2560, 2816, 3072, 3584, 3712
GATE = 4224
IN_COLS_PAD = GATE + N_BRANCH * D_MODEL

TOKEN_TILE = 256
ATTN_TILE = 256
GLA_TILE = 256
VMEM_LIMIT = 56 * 1024 * 1024


def _const_spec(shape):
    zeros = (0,) * len(shape)
    return pl.BlockSpec(shape, lambda *_: zeros, pipeline_mode=pl.Buffered(1))


def _sigmoid(x):
    return 1.0 / (1.0 + jnp.exp(-x))


def _silu(x):
    return x * _sigmoid(x)


def _gelu_tanh(x):
    c = math.sqrt(2.0 / math.pi)
    return 0.5 * x * (1.0 + jnp.tanh(c * (x + 0.044715 * (x * x * x))))


def _log_sigmoid(x):
    return jnp.minimum(x, 0.0) - jnp.log(1.0 + jnp.exp(-jnp.abs(x)))


def _rms(x, width):
    return lax.rsqrt(jnp.sum(x * x, axis=-1, keepdims=True) * (1.0 / width) + EPS)


def _rope_body(pos_ref, cos_ref, sin_ref):
    lane = lax.broadcasted_iota(jnp.int32, (1, LANES), 1)
    in_x1 = (lane >= X1_LO) & (lane < X2_LO)
    in_x2 = (lane >= X2_LO) & (lane < MLA_QK)
    fidx = jnp.where(in_x1, lane - X1_LO, lane - X2_LO).astype(F32)
    inv = 1.0 / jnp.exp(fidx * (2.0 / MLA_ROPE) * math.log(ROPE_THETA))
    ang = pos_ref[...].astype(F32) * inv
    cos = jnp.cos(ang)
    sin = jnp.sin(ang)
    rope = in_x1 | in_x2
    cos_ref[...] = jnp.where(rope, cos, jnp.where(lane < MLA_NOPE, 1.0, 0.0))
    sin_ref[...] = jnp.where(in_x1, -sin, jnp.where(in_x2, sin, 0.0))


def _rope_tables(positions):
    t = positions.size
    tm = 1024 if t % 1024 == 0 else TOKEN_TILE
    return pl.pallas_call(
        _rope_body,
        grid=(t // tm,),
        in_specs=[pl.BlockSpec((tm, 1), lambda i: (i, 0))],
        out_specs=[pl.BlockSpec((tm, LANES), lambda i: (i, 0))] * 2,
        out_shape=[jax.ShapeDtypeStruct((t, LANES), F32)] * 2,
        name="rope_tables",
    )(positions.reshape(t, 1))


def _in_proj_body(x_ref, cos_ref, sin_ref, ng_ref, win_ref, lng_ref, lnb_ref, sgw_ref, sgbt_ref,
                  cqg_ref, ckvg_ref, wuq_ref, wuk_ref, wuvt_ref, qg_ref, kg_ref, wgu_ref, bgu_ref,
                  bgate_ref,
                  ya_ref, qb_ref, kb_ref, vbt_ref, zb_ref, qc_ref, kc_ref, vc_ref, la_ref, zc_ref,
                  gate_ref):
    tm = x_ref.shape[0]
    x = x_ref[...]
    h = (x * _rms(x, D_MODEL) * ng_ref[...]).astype(BF16)

    def proj(lo, width):
        return jnp.dot(h, win_ref[:, lo:lo + width], preferred_element_type=F32)

    u = _gelu_tanh(proj(A_U, BRANCH_W))
    v = _gelu_tanh(proj(A_V, BRANCH_W))
    mu = jnp.mean(v, axis=-1, keepdims=True)
    vc = v - mu
    vn = vc * lax.rsqrt(jnp.mean(vc * vc, axis=-1, keepdims=True) + EPS)
    vn = (vn * lng_ref[...] + lnb_ref[...]).astype(BF16)
    uz = u * _silu(proj(A_Z, BRANCH_W))
    ri = lax.broadcasted_iota(jnp.int32, (SG_BLOCK, SG_BLOCK), 0) // CHUNK
    ci = lax.broadcasted_iota(jnp.int32, (SG_BLOCK, SG_BLOCK), 1) // CHUNK
    chunk_causal = ri >= ci
    for g in range(SG_GROUPS):
        wg = jnp.where(chunk_causal, sgw_ref[g], 0.0).astype(BF16)
        bias = sgbt_ref[:, g:g + 1]
        cols = slice(g * LANES, (g + 1) * LANES)
        for n in range(tm // SG_BLOCK):
            rows = slice(n * SG_BLOCK, (n + 1) * SG_BLOCK)
            sv = jnp.dot(wg, vn[rows, cols], preferred_element_type=F32) + bias
            ya_ref[rows, cols] = (uz[rows, cols] * sv).astype(BF16)

    cq = proj(B_CQ, MLA_Q_RANK)
    cqn = (cq * _rms(cq, MLA_Q_RANK) * cqg_ref[...]).astype(BF16)
    q = jnp.dot(cqn, wuq_ref[...], preferred_element_type=F32)
    ckv = proj(B_CKV, MLA_KV_RANK)
    ckvn = (ckv * _rms(ckv, MLA_KV_RANK) * ckvg_ref[...]).astype(BF16)
    kn = jnp.dot(ckvn, wuk_ref[...], preferred_element_type=F32)
    vbt_ref[0] = lax.dot_general(wuvt_ref[...], ckvn, (((1,), (1,)), ((), ())),
                                 preferred_element_type=F32).astype(BF16)
    kr = proj(B_KR, LANES)
    cos = cos_ref[...]
    sin = sin_ref[...]
    lane = lax.broadcasted_iota(jnp.int32, (1, LANES), 1)
    in_x1 = (lane >= X1_LO) & (lane < X2_LO)

    def norm_rope(xh, gain):
        xn = xh * _rms(xh, MLA_QK) * gain
        partner = jnp.where(in_x1, pltpu.roll(xn, LANES - ROPE_HALF, 1), pltpu.roll(xn, ROPE_HALF, 1))
        return (xn * cos + partner * sin).astype(BF16)

    for hd in range(MLA_HEADS):
        cols = slice(hd * HEAD_PAD, (hd + 1) * HEAD_PAD)
        qb_ref[:, cols] = norm_rope(q[:, cols], qg_ref[...])
        kb_ref[:, cols] = norm_rope(kn[:, cols] + kr, kg_ref[...])
    zb_ref[...] = _silu(proj(B_Z, BRANCH_W)).astype(BF16)

    qc_ref[...] = (proj(C_Q, GLA_HEADS * GLA_DK) * (GLA_DK ** -0.5)).astype(BF16)
    kc_ref[...] = proj(C_K, GLA_HEADS * GLA_DK).astype(BF16)
    vc_ref[...] = proj(C_V, GLA_HEADS * GLA_DV).astype(BF16)
    glr = proj(C_G, LANES).astype(BF16)
    gl = jnp.dot(glr, wgu_ref[...], preferred_element_type=F32) + bgu_ref[...]
    la_ref[...] = _log_sigmoid(gl) * (1.0 / GLA_TAU)
    zc_ref[...] = _silu(proj(C_Z, BRANCH_W)).astype(BF16)

    for n in range(N_BRANCH):
        cols = slice(n * D_MODEL, (n + 1) * D_MODEL)
        logits = proj(GATE + n * D_MODEL, D_MODEL) + bgate_ref[:, cols]
        gate_ref[:, cols] = _sigmoid(logits).astype(BF16)


def _in_proj(x2, cos_t, sin_t, p):
    t = x2.shape[0]
    tm = TOKEN_TILE
    row = lambda w: pl.BlockSpec((tm, w), lambda i: (i, 0))
    consts = [p["norm_g"], p["w_in"], p["sg_ln_g"], p["sg_ln_b"], p["sg_w"], p["sg_bt"],
              p["cq_g"], p["ckv_g"], p["w_uq"], p["w_uk"], p["w_uvt"], p["q_g"], p["k_g"],
              p["w_gu"], p["b_gu"], p["b_gate"]]
    out_widths = [(BRANCH_W, BF16), (MLA_HEADS * HEAD_PAD, BF16), (MLA_HEADS * HEAD_PAD, BF16),
                  None, (BRANCH_W, BF16),
                  (GLA_HEADS * GLA_DK, BF16), (GLA_HEADS * GLA_DK, BF16), (GLA_HEADS * GLA_DV, BF16),
                  (GLA_HEADS * GLA_DK, F32), (BRANCH_W, BF16), (N_BRANCH * D_MODEL, BF16)]
    vt_spec = pl.BlockSpec((1, MLA_HEADS * MLA_V, tm), lambda i: (i, 0, 0))
    vt_shape = jax.ShapeDtypeStruct((t // tm, MLA_HEADS * MLA_V, tm), BF16)
    return pl.pallas_call(
        _in_proj_body,
        grid=(t // tm,),
        in_specs=[row(D_MODEL), row(LANES), row(LANES)] + [_const_spec(c.shape) for c in consts],
        out_specs=[vt_spec if o is None else row(o[0]) for o in out_widths],
        out_shape=[vt_shape if o is None else jax.ShapeDtypeStruct((t, o[0]), o[1]) for o in out_widths],
        compiler_params=pltpu.CompilerParams(dimension_semantics=("parallel",),
                                             vmem_limit_bytes=VMEM_LIMIT),
        name="in_proj",
    )(x2, cos_t, sin_t, *consts)


def _gla_body(q_ref, k_ref, v_ref, la_ref, z_ref, og_ref, y_ref, state_ref):
    @pl.when(pl.program_id(1) == 0)
    def _():
        state_ref[...] = jnp.zeros_like(state_ref)

    tc = q_ref.shape[0]
    kw = GLA_HEADS * GLA_DK
    ri = lax.broadcasted_iota(jnp.int32, (CHUNK, CHUNK), 0)
    ci = lax.broadcasted_iota(jnp.int32, (CHUNK, CHUNK), 1)
    causal = ri >= ci
    tri = jnp.where(causal, 1.0, 0.0).astype(BF16)
    lane = lax.broadcasted_iota(jnp.int32, (1, kw), 1)
    nt = (((1,), (1,)), ((), ()))
    tn = (((0,), (0,)), ((), ()))
    for c in range(tc // CHUNK):
        rows = slice(c * CHUNK, (c + 1) * CHUNK)
        la = la_ref[rows, :]
        la_hi = la.astype(BF16)
        la_lo = (la - la_hi.astype(F32)).astype(BF16)
        b = (jnp.dot(tri, la_hi, preferred_element_type=F32)
             + jnp.dot(tri, la_lo, preferred_element_type=F32))
        b_last = b[CHUNK - 1:CHUNK, :]
        qt = q_ref[rows, :].astype(F32) * jnp.exp(b)
        kf = k_ref[rows, :].astype(F32)
        kt = (kf * jnp.exp(-b)).astype(BF16)
        ks = (kf * jnp.exp(b_last - b)).astype(BF16)
        dec = jnp.exp(b_last)
        for hd in range(GLA_HEADS):
            own = (lane >= hd * GLA_DK) & (lane < (hd + 1) * GLA_DK)
            qh = jnp.where(own, qt, 0.0).astype(BF16)
            vh = v_ref[rows, hd * GLA_DV:(hd + 1) * GLA_DV]
            att = lax.dot_general(qh, kt, nt, preferred_element_type=F32)
            att = jnp.where(causal, att, 0.0).astype(BF16)
            st = state_ref[hd]
            o = (jnp.dot(att, vh, preferred_element_type=F32)
                 + lax.dot_general(qh, st.astype(BF16), nt, preferred_element_type=F32))
            state_ref[hd] = st * dec + lax.dot_general(vh, ks, tn, preferred_element_type=F32)
            cols = slice(hd * GLA_DV, (hd + 1) * GLA_DV)
            on = o * _rms(o, GLA_DV) * og_ref[...]
            y_ref[rows, cols] = (on * z_ref[rows, cols].astype(F32)).astype(BF16)


def _gla(qc, kc, vc, la, zc, o_g, batch, seq):
    tc = GLA_TILE
    ns = seq // tc
    row = lambda w: pl.BlockSpec((tc, w), lambda b, s: (b * ns + s, 0))
    kw = GLA_HEADS * GLA_DK
    vw = GLA_HEADS * GLA_DV
    return pl.pallas_call(
        _gla_body,
        grid=(batch, ns),
        in_specs=[row(kw), row(kw), row(vw), row(kw), row(vw), _const_spec(o_g.shape)],
        out_specs=row(vw),
        out_shape=jax.ShapeDtypeStruct((batch * seq, vw), BF16),
        scratch_shapes=[pltpu.VMEM((GLA_HEADS, GLA_DV, kw), F32)],
        compiler_params=pltpu.CompilerParams(dimension_semantics=("parallel", "arbitrary"),
                                             vmem_limit_bytes=VMEM_LIMIT),
        name="gla",
    )(qc, kc, vc, la, zc, o_g)


def _attn_body(q_ref, k_ref, vt_ref, z_ref, y_ref, m_ref, l_ref, alpha_ref, acc_ref, s_ref, p_ref):
    tq = q_ref.shape[0]
    tk = tq
    i = pl.program_id(1)
    nt = (((1,), (1,)), ((), ()))
    ki = lax.broadcasted_iota(jnp.int32, (tk, tq), 0) // CHUNK
    qi = lax.broadcasted_iota(jnp.int32, (tk, tq), 1) // CHUNK
    visible = ki <= qi
    m_ref[...] = jnp.full(m_ref.shape, -jnp.inf, F32)
    l_ref[...] = jnp.zeros(l_ref.shape, F32)
    acc_ref[...] = jnp.zeros(acc_ref.shape, F32)

    def all_sublanes(x, op):
        for shift in (4, 2, 1):
            x = op(x, pltpu.roll(x, shift, 0))
        return x

    def scores(j, masked):
        start = pl.multiple_of(j * tk, tk)
        for hd in range(MLA_HEADS):
            cols = slice(hd * HEAD_PAD, (hd + 1) * HEAD_PAD)
            st = lax.dot_general(k_ref[pl.ds(start, tk), cols], q_ref[:, cols], nt,
                                 preferred_element_type=F32)
            if masked:
                st = jnp.where(visible, st, -jnp.inf)
            s_ref[hd] = st

    def weighted_values(j):
        for hd in range(MLA_HEADS):
            rows = slice(hd * MLA_V, (hd + 1) * MLA_V)
            pv = jnp.dot(vt_ref[j, rows, :], p_ref[hd], preferred_element_type=F32)
            acc = acc_ref[rows, :].reshape(MLA_V // SUBLANES, SUBLANES, tq) * alpha_ref[hd][None]
            acc_ref[rows, :] = acc.reshape(MLA_V, tq) + pv

    def reduce_rows(st, op):
        parts = op(st.reshape(4, tk // SUBLANES // 4, SUBLANES, tq), axis=1)
        return op(parts, axis=0)

    def softmax():
        for hd in range(MLA_HEADS):
            st = s_ref[hd].reshape(tk // SUBLANES, SUBLANES, tq)
            m_prev = m_ref[hd]
            m_new = jnp.maximum(m_prev, all_sublanes(reduce_rows(st, jnp.max), jnp.maximum))
            alpha = jnp.exp2(m_prev - m_new)
            p = jnp.exp2(st - m_new[None])
            l_ref[hd] = alpha * l_ref[hd] + all_sublanes(reduce_rows(p, jnp.sum), jnp.add)
            p_ref[hd] = p.reshape(tk, tq).astype(BF16)
            alpha_ref[hd] = alpha
            m_ref[hd] = m_new

    scores(i, True)

    @pl.when(i == 0)
    def _():
        softmax()

    @pl.when(i > 0)
    def _():
        softmax()
        scores(0, False)

        def body(t, carry):
            weighted_values(jnp.where(t == 1, i, t - 2))
            softmax()
            scores(t, False)
            return carry

        lax.fori_loop(1, i, body, 0)
        weighted_values(jnp.where(i == 1, i, i - 2))
        softmax()

    weighted_values(jnp.where(i == 0, 0, i - 1))
    for hd in range(MLA_HEADS):
        rows = slice(hd * MLA_V, (hd + 1) * MLA_V)
        acc = acc_ref[rows, :].reshape(MLA_V // SUBLANES, SUBLANES, tq) / l_ref[hd][None]
        acc_ref[rows, :] = acc.reshape(MLA_V, tq)
    y_ref[...] = (acc_ref[...].T * z_ref[...].astype(F32)).astype(BF16)


def _attn(qb, kb, vbt, zb, batch, seq):
    tq = ATTN_TILE
    nq = seq // tq
    hw = MLA_HEADS * HEAD_PAD
    return pl.pallas_call(
        _attn_body,
        grid=(batch, nq),
        in_specs=[pl.BlockSpec((tq, hw), lambda b, i: (b * nq + i, 0)),
                  pl.BlockSpec((seq, hw), lambda b, i: (b, 0), pipeline_mode=pl.Buffered(1)),
                  pl.BlockSpec((nq, BRANCH_W, tq), lambda b, i: (b, 0, 0), pipeline_mode=pl.Buffered(1)),
                  pl.BlockSpec((tq, BRANCH_W), lambda b, i: (b * nq + i, 0))],
        out_specs=pl.BlockSpec((tq, BRANCH_W), lambda b, i: (b * nq + i, 0)),
        out_shape=jax.ShapeDtypeStruct((batch * seq, BRANCH_W), BF16),
        scratch_shapes=[pltpu.VMEM((MLA_HEADS, SUBLANES, tq), F32),
                        pltpu.VMEM((MLA_HEADS, SUBLANES, tq), F32),
                        pltpu.VMEM((MLA_HEADS, SUBLANES, tq), F32),
                        pltpu.VMEM((MLA_HEADS * MLA_V, tq), F32),
                        pltpu.VMEM((MLA_HEADS, tq, tq), F32),
                        pltpu.VMEM((MLA_HEADS, tq, tq), BF16)],
        compiler_params=pltpu.CompilerParams(dimension_semantics=("parallel", "arbitrary"),
                                             vmem_limit_bytes=VMEM_LIMIT),
        name="mla_attn",
    )(qb, kb, vbt, zb)


def _merge_body(x_ref, ya_ref, yb_ref, yc_ref, gate_ref, wbr_ref, wout_ref, o_ref):
    merged = None
    for n, y_ref in enumerate((ya_ref, yb_ref, yc_ref)):
        br = jnp.dot(y_ref[...], wbr_ref[n], preferred_element_type=F32)
        term = gate_ref[:, n * D_MODEL:(n + 1) * D_MODEL].astype(F32) * br
        merged = term if merged is None else merged + term
    o_ref[...] = x_ref[...] + jnp.dot(merged.astype(BF16), wout_ref[...], preferred_element_type=F32)


def _merge(x2, ya, yb, yc, gates, w_branch, w_out):
    t = x2.shape[0]
    tm = TOKEN_TILE
    row = lambda w: pl.BlockSpec((tm, w), lambda i: (i, 0))
    return pl.pallas_call(
        _merge_body,
        grid=(t // tm,),
        in_specs=[row(D_MODEL), row(BRANCH_W), row(BRANCH_W), row(BRANCH_W), row(N_BRANCH * D_MODEL),
                  _const_spec(w_branch.shape), _const_spec(w_out.shape)],
        out_specs=row(D_MODEL),
        out_shape=jax.ShapeDtypeStruct((t, D_MODEL), F32),
        compiler_params=pltpu.CompilerParams(dimension_semantics=("parallel",),
                                             vmem_limit_bytes=VMEM_LIMIT),
        name="merge",
    )(x2, ya, yb, yc, gates, w_branch, w_out)


def _pad_heads(w, heads, width):
    lead = w.shape[:-1]
    w = w.reshape(lead + (heads, width))
    w = jnp.pad(w, [(0, 0)] * len(lead) + [(0, 0), (0, HEAD_PAD - width)])
    return w.reshape(lead + (heads * HEAD_PAD,))


def _pack_layer(l, norm_g, w_in, b_gate, sg_ln_g, sg_ln_b, sg_w, sg_b, mla_cq_g, mla_ckv_g, mla_w_uq,
                mla_w_ukv, mla_q_g, mla_k_g, gla_w_gate, gla_b_gate, gla_o_g, w_branch, w_out):
    w = w_in[l]
    d = w.shape[0]
    zeros = lambda n: jnp.zeros((d, n), w.dtype)
    w_packed = jnp.concatenate([
        w[:, :1920], zeros(MLA_NOPE), w[:, 1920:1952], zeros(LANES - MLA_QK),
        w[:, 1952:3504], zeros(LANES - GLA_GATE_RANK), w[:, 3504:]], axis=1)
    ukv = mla_w_ukv[l].reshape(MLA_KV_RANK, MLA_HEADS, MLA_NOPE + MLA_V)
    w_uk = _pad_heads(ukv[:, :, :MLA_NOPE].reshape(MLA_KV_RANK, -1), MLA_HEADS, MLA_NOPE)
    w_uvt = ukv[:, :, MLA_NOPE:].reshape(MLA_KV_RANK, -1).T
    pad_gain = lambda g: jnp.pad(g, (0, HEAD_PAD - MLA_QK)).reshape(1, HEAD_PAD)
    return {
        "norm_g": norm_g[l].reshape(1, -1),
        "w_in": w_packed.astype(BF16),
        "sg_ln_g": sg_ln_g[l].reshape(1, -1),
        "sg_ln_b": sg_ln_b[l].reshape(1, -1),
        "sg_w": sg_w[l],
        "sg_bt": sg_b[l].T,
        "cq_g": mla_cq_g[l].reshape(1, -1),
        "ckv_g": mla_ckv_g[l].reshape(1, -1),
        "w_uq": _pad_heads(mla_w_uq[l], MLA_HEADS, MLA_QK).astype(BF16),
        "w_uk": w_uk.astype(BF16),
        "w_uvt": w_uvt.astype(BF16),
        "q_g": pad_gain(mla_q_g[l]) * (MLA_QK ** -0.5 * math.log2(math.e)),
        "k_g": pad_gain(mla_k_g[l]),
        "w_gu": jnp.pad(gla_w_gate[l], ((0, LANES - GLA_GATE_RANK), (0, 0))).astype(BF16),
        "b_gu": gla_b_gate[l].reshape(1, -1),
        "b_gate": b_gate[l].reshape(1, -1),
        "o_g": gla_o_g[l].reshape(1, -1),
        "w_branch": w_branch[l].astype(BF16),
        "w_out": w_out[l].astype(BF16),
    }


def kernel(x, positions, norm_g, w_in, b_gate, sg_ln_g, sg_ln_b, sg_w, sg_b, mla_cq_g, mla_ckv_g, mla_w_uq, mla_w_ukv, mla_q_g, mla_k_g, gla_w_gate, gla_b_gate, gla_o_g, w_branch, w_out):
    batch, seq, d = x.shape
    assert d == D_MODEL and seq % max(ATTN_TILE, GLA_TILE, TOKEN_TILE) == 0
    assert TOKEN_TILE == ATTN_TILE
    depth = w_in.shape[0]
    cos_t, sin_t = _rope_tables(positions)
    x2 = x.reshape(batch * seq, d)
    for l in range(depth):
        p = _pack_layer(l, norm_g, w_in, b_gate, sg_ln_g, sg_ln_b, sg_w, sg_b, mla_cq_g, mla_ckv_g,
                        mla_w_uq, mla_w_ukv, mla_q_g, mla_k_g, gla_w_gate, gla_b_gate, gla_o_g,
                        w_branch, w_out)
        ya, qb, kb, vbt, zb, qc, kc, vc, la, zc, gates = _in_proj(x2, cos_t, sin_t, p)
        yc = _gla(qc, kc, vc, la, zc, p["o_g"], batch, seq)
        yb = _attn(qb, kb, vbt, zb, batch, seq)
        x2 = _merge(x2, ya, yb, yc, gates, p["w_branch"], p["w_out"])
    return x2.reshape(batch, seq, d)
```

```python
import math

import jax
import jax.numpy as jnp
from jax import lax
from jax.experimental import pallas as pl
from jax.experimental.pallas import tpu as pltpu

F32 = jnp.float32
BF16 = jnp.bfloat16

D_MODEL = 1024
CHUNK = 64
BRANCH_W = 512
N_BRANCH = 3
EPS = 1e-6
SG_BLOCK = 128
SG_GROUPS = 4
MLA_HEADS = 8
MLA_NOPE = 64
MLA_ROPE = 32
MLA_QK = MLA_NOPE + MLA_ROPE
MLA_V = 64
MLA_Q_RANK = 256
MLA_KV_RANK = 128
ROPE_THETA = 10000.0
GLA_HEADS = 4
GLA_DK = 64
GLA_DV = 128
GLA_GATE_RANK = 16
GLA_TAU = 16.0

LANES = 128
SUBLANES = 8
HEAD_PAD = LANES
ROPE_HALF = MLA_ROPE // 2
X1_LO = MLA_NOPE
X2_LO = MLA_NOPE + ROPE_HALF
BIAS_LANE = MLA_QK

A_U, A_V, A_Z = 0, 512, 1024
B_CQ, B_CKV, B_KR, B_Z = 1536, 1792, 1920, 2048
C_Q, C_K, C_V, C_G, C_Z = 2560, 2816, 3072, 3584, 3712
MIX_COLS_PAD = 4224
GATE_SRC = 4016

TOKEN_TILE = 512
ATTN_TILE = 256
GLA_TILE = 256
VMEM_LIMIT = 56 * 1024 * 1024

SCORE_BOUND_MARGIN = 1.02
FIXED_REFERENCE_MAX_BOUND = 40.0

NT_DIMS = (((1,), (1,)), ((), ()))
TN_DIMS = (((0,), (0,)), ((), ()))


def _const_spec(shape):
    zeros = (0,) * len(shape)
    return pl.BlockSpec(shape, lambda *_: zeros, pipeline_mode=pl.Buffered(1))


def _sigmoid(x):
    return 0.5 * (jnp.tanh(0.5 * x) + 1.0)


def _silu(x):
    return x * _sigmoid(x)


def _gelu_tanh(x):
    c = math.sqrt(2.0 / math.pi)
    return 0.5 * x * (1.0 + jnp.tanh(c * (x + 0.044715 * (x * x * x))))


def _log_sigmoid(x):
    return jnp.minimum(x, 0.0) - jnp.log(1.0 + jnp.exp(-jnp.abs(x)))


def _rms(x, width):
    return lax.rsqrt(jnp.sum(x * x, axis=-1, keepdims=True) * (1.0 / width) + EPS)


def _rope_body(pos_ref, cos_ref, sin_ref):
    lane = lax.broadcasted_iota(jnp.int32, (1, LANES), 1)
    in_x1 = (lane >= X1_LO) & (lane < X2_LO)
    in_x2 = (lane >= X2_LO) & (lane < MLA_QK)
    fidx = jnp.where(in_x1, lane - X1_LO, lane - X2_LO).astype(F32)
    inv = 1.0 / jnp.exp(fidx * (2.0 / MLA_ROPE) * math.log(ROPE_THETA))
    ang = pos_ref[...].astype(F32) * inv
    cos = jnp.cos(ang)
    sin = jnp.sin(ang)
    rope = in_x1 | in_x2
    cos_ref[...] = jnp.where(rope, cos, jnp.where(lane < MLA_NOPE, 1.0, 0.0))
    sin_ref[...] = jnp.where(in_x1, -sin, jnp.where(in_x2, sin, 0.0))


def _rope_tables(positions):
    t = positions.size
    tm = 1024 if t % 1024 == 0 else ATTN_TILE
    return pl.pallas_call(
        _rope_body,
        grid=(t // tm,),
        in_specs=[pl.BlockSpec((tm, 1), lambda i: (i, 0))],
        out_specs=[pl.BlockSpec((tm, LANES), lambda i: (i, 0))] * 2,
        out_shape=[jax.ShapeDtypeStruct((t, LANES), F32)] * 2,
        name="rope_tables",
    )(positions.reshape(t, 1))


def _in_proj_body(x_ref, cos_ref, sin_ref, ng_ref, win_ref, lng_ref, lnb_ref, sgw_ref, sgbt_ref,
                  cqg_ref, ckvg_ref, wuq_ref, wuk_ref, wuvt_ref, qg_ref, kg_ref, qpad_ref, kpad_ref,
                  wgu_ref, bgu_ref,
                  ya_ref, qb_ref, kb_ref, vbt_ref, zb_ref, qc_ref, kc_ref, vc_ref, la_ref, zc_ref):
    tm = x_ref.shape[0]
    x = x_ref[...]
    h = (x * _rms(x, D_MODEL) * ng_ref[...]).astype(BF16)

    def proj(lo, width):
        return jnp.dot(h, win_ref[:, lo:lo + width], preferred_element_type=F32)

    u = _gelu_tanh(proj(A_U, BRANCH_W))
    v = _gelu_tanh(proj(A_V, BRANCH_W))
    mu = jnp.mean(v, axis=-1, keepdims=True)
    vc = v - mu
    vn = vc * lax.rsqrt(jnp.mean(vc * vc, axis=-1, keepdims=True) + EPS)
    vn = (vn * lng_ref[...] + lnb_ref[...]).astype(BF16)
    uz = u * _silu(proj(A_Z, BRANCH_W))
    ri = lax.broadcasted_iota(jnp.int32, (SG_BLOCK, SG_BLOCK), 0) // CHUNK
    ci = lax.broadcasted_iota(jnp.int32, (SG_BLOCK, SG_BLOCK), 1) // CHUNK
    chunk_causal = ri >= ci
    for g in range(SG_GROUPS):
        wg = jnp.where(chunk_causal, sgw_ref[g], 0.0).astype(BF16)
        bias = sgbt_ref[:, g:g + 1]
        cols = slice(g * LANES, (g + 1) * LANES)
        for n in range(tm // SG_BLOCK):
            rows = slice(n * SG_BLOCK, (n + 1) * SG_BLOCK)
            sv = jnp.dot(wg, vn[rows, cols], preferred_element_type=F32) + bias
            ya_ref[rows, cols] = (uz[rows, cols] * sv).astype(BF16)

    cq = proj(B_CQ, MLA_Q_RANK)
    cqn = (cq * _rms(cq, MLA_Q_RANK) * cqg_ref[...]).astype(BF16)
    q = jnp.dot(cqn, wuq_ref[...], preferred_element_type=F32)
    ckv = proj(B_CKV, MLA_KV_RANK)
    ckvn = (ckv * _rms(ckv, MLA_KV_RANK) * ckvg_ref[...]).astype(BF16)
    kn = jnp.dot(ckvn, wuk_ref[...], preferred_element_type=F32)
    for s in range(tm // ATTN_TILE):
        vbt_ref[s] = lax.dot_general(wuvt_ref[...], ckvn[s * ATTN_TILE:(s + 1) * ATTN_TILE], NT_DIMS,
                                     preferred_element_type=F32).astype(BF16)
    kr = proj(B_KR, LANES)
    cos = cos_ref[...]
    sin = sin_ref[...]
    lane = lax.broadcasted_iota(jnp.int32, (1, LANES), 1)
    in_x1 = (lane >= X1_LO) & (lane < X2_LO)

    def norm_rope(xh, gain, pad_row):
        xn = xh * _rms(xh, MLA_QK) * gain
        partner = jnp.where(in_x1, pltpu.roll(xn, LANES - ROPE_HALF, 1), pltpu.roll(xn, ROPE_HALF, 1))
        return (xn * cos + partner * sin + pad_row).astype(BF16)

    for hd in range(MLA_HEADS):
        cols = slice(hd * HEAD_PAD, (hd + 1) * HEAD_PAD)
        qb_ref[:, cols] = norm_rope(q[:, cols], qg_ref[...], qpad_ref[...])
        kb_ref[:, cols] = norm_rope(kn[:, cols] + kr, kg_ref[...], kpad_ref[...])
    zb_ref[...] = _silu(proj(B_Z, BRANCH_W)).astype(BF16)

    qc_ref[...] = (proj(C_Q, GLA_HEADS * GLA_DK) * (GLA_DK ** -0.5)).astype(BF16)
    kc_ref[...] = proj(C_K, GLA_HEADS * GLA_DK).astype(BF16)
    vc_ref[...] = proj(C_V, GLA_HEADS * GLA_DV).astype(BF16)
    glr = proj(C_G, LANES).astype(BF16)
    gl = jnp.dot(glr, wgu_ref[...], preferred_element_type=F32) + bgu_ref[...]
    la_ref[...] = _log_sigmoid(gl) * (1.0 / GLA_TAU)
    zc_ref[...] = _silu(proj(C_Z, BRANCH_W)).astype(BF16)


def _in_proj(x2, cos_t, sin_t, p):
    t = x2.shape[0]
    tm = TOKEN_TILE
    row = lambda w: pl.BlockSpec((tm, w), lambda i: (i, 0))
    consts = [p["norm_g"], p["w_mix"], p["sg_ln_g"], p["sg_ln_b"], p["sg_w"], p["sg_bt"],
              p["cq_g"], p["ckv_g"], p["w_uq"], p["w_uk"], p["w_uvt"], p["q_g"], p["k_g"],
              p["q_pad"], p["k_pad"], p["w_gu"], p["b_gu"]]
    out_widths = [(BRANCH_W, BF16), (MLA_HEADS * HEAD_PAD, BF16), (MLA_HEADS * HEAD_PAD, BF16),
                  None, (BRANCH_W, BF16),
                  (GLA_HEADS * GLA_DK, BF16), (GLA_HEADS * GLA_DK, BF16), (GLA_HEADS * GLA_DV, BF16),
                  (GLA_HEADS * GLA_DK, F32), (BRANCH_W, BF16)]
    per_step = tm // ATTN_TILE
    vt_spec = pl.BlockSpec((per_step, MLA_HEADS * MLA_V, ATTN_TILE), lambda i: (i, 0, 0))
    vt_shape = jax.ShapeDtypeStruct((t // ATTN_TILE, MLA_HEADS * MLA_V, ATTN_TILE), BF16)
    return pl.pallas_call(
        _in_proj_body,
        grid=(t // tm,),
        in_specs=[row(D_MODEL), row(LANES), row(LANES)] + [_const_spec(c.shape) for c in consts],
        out_specs=[vt_spec if o is None else row(o[0]) for o in out_widths],
        out_shape=[vt_shape if o is None else jax.ShapeDtypeStruct((t, o[0]), o[1]) for o in out_widths],
        compiler_params=pltpu.CompilerParams(dimension_semantics=("parallel",),
                                             vmem_limit_bytes=VMEM_LIMIT),
        name="in_proj",
    )(x2, cos_t, sin_t, *consts)


def _gla_body(q_ref, k_ref, v_ref, la_ref, z_ref, og_ref, y_ref, state_ref):
    @pl.when(pl.program_id(1) == 0)
    def _():
        state_ref[...] = jnp.zeros_like(state_ref)

    tc = q_ref.shape[0]
    kw = GLA_HEADS * GLA_DK
    ri = lax.broadcasted_iota(jnp.int32, (CHUNK, CHUNK), 0)
    ci = lax.broadcasted_iota(jnp.int32, (CHUNK, CHUNK), 1)
    causal = ri >= ci
    tri = jnp.where(causal, 1.0, 0.0).astype(BF16)
    lane = lax.broadcasted_iota(jnp.int32, (1, kw), 1)
    for c in range(tc // CHUNK):
        rows = slice(c * CHUNK, (c + 1) * CHUNK)
        la = la_ref[rows, :]
        la_hi = la.astype(BF16)
        la_lo = (la - la_hi.astype(F32)).astype(BF16)
        b = (jnp.dot(tri, la_hi, preferred_element_type=F32)
             + jnp.dot(tri, la_lo, preferred_element_type=F32))
        b_last = b[CHUNK - 1:CHUNK, :]
        qt = q_ref[rows, :].astype(F32) * jnp.exp(b)
        kf = k_ref[rows, :].astype(F32)
        kt = (kf * jnp.exp(-b)).astype(BF16)
        ks = (kf * jnp.exp(b_last - b)).astype(BF16)
        dec = jnp.exp(b_last)
        for hd in range(GLA_HEADS):
            own = (lane >= hd * GLA_DK) & (lane < (hd + 1) * GLA_DK)
            qh = jnp.where(own, qt, 0.0).astype(BF16)
            vh = v_ref[rows, hd * GLA_DV:(hd + 1) * GLA_DV]
            att = lax.dot_general(qh, kt, NT_DIMS, preferred_element_type=F32)
            att = jnp.where(causal, att, 0.0).astype(BF16)
            st = state_ref[hd]
            o = (jnp.dot(att, vh, preferred_element_type=F32)
                 + lax.dot_general(qh, st.astype(BF16), NT_DIMS, preferred_element_type=F32))
            state_ref[hd] = st * dec + lax.dot_general(vh, ks, TN_DIMS, preferred_element_type=F32)
            cols = slice(hd * GLA_DV, (hd + 1) * GLA_DV)
            on = o * _rms(o, GLA_DV) * og_ref[...]
            y_ref[rows, cols] = (on * z_ref[rows, cols].astype(F32)).astype(BF16)


def _gla(qc, kc, vc, la, zc, o_g, batch, seq):
    tc = GLA_TILE
    ns = seq // tc
    row = lambda w: pl.BlockSpec((tc, w), lambda b, s: (b * ns + s, 0))
    kw = GLA_HEADS * GLA_DK
    vw = GLA_HEADS * GLA_DV
    return pl.pallas_call(
        _gla_body,
        grid=(batch, ns),
        in_specs=[row(kw), row(kw), row(vw), row(kw), row(vw), _const_spec(o_g.shape)],
        out_specs=row(vw),
        out_shape=jax.ShapeDtypeStruct((batch * seq, vw), BF16),
        scratch_shapes=[pltpu.VMEM((GLA_HEADS, GLA_DV, kw), F32)],
        compiler_params=pltpu.CompilerParams(dimension_semantics=("parallel", "arbitrary"),
                                             vmem_limit_bytes=VMEM_LIMIT),
        name="gla",
    )(qc, kc, vc, la, zc, o_g)


def _attn_body(fixed_ref, q_ref, k_ref, vt_ref, z_ref, y_ref, m_ref, l_ref, alpha_ref, acc_ref, s_ref,
               p_ref):
    tq = q_ref.shape[0]
    tk = tq
    i = pl.program_id(1)
    ki = lax.broadcasted_iota(jnp.int32, (tk, tq), 0) // CHUNK
    qi = lax.broadcasted_iota(jnp.int32, (tk, tq), 1) // CHUNK
    visible = ki <= qi
    slabs = MLA_V // SUBLANES

    def all_sublanes(x, op):
        for shift in (4, 2, 1):
            x = op(x, pltpu.roll(x, shift, 0))
        return x

    def reduce_rows(st, op):
        parts = op(st.reshape(4, tk // SUBLANES // 4, SUBLANES, tq), axis=1)
        return op(parts, axis=0)

    def scores(j, masked):
        start = pl.multiple_of(j * tk, tk)
        for hd in range(MLA_HEADS):
            cols = slice(hd * HEAD_PAD, (hd + 1) * HEAD_PAD)
            st = lax.dot_general(k_ref[pl.ds(start, tk), cols], q_ref[:, cols], NT_DIMS,
                                 preferred_element_type=F32)
            if masked:
                st = jnp.where(visible, st, -jnp.inf)
            s_ref[hd] = st

    def softmax_running_max():
        for hd in range(MLA_HEADS):
            st = s_ref[hd].reshape(tk // SUBLANES, SUBLANES, tq)
            m_prev = m_ref[hd]
            m_new = jnp.maximum(m_prev, all_sublanes(reduce_rows(st, jnp.max), jnp.maximum))
            alpha = jnp.exp2(m_prev - m_new)
            p = jnp.exp2(st - m_new[None])
            l_ref[hd] = alpha * l_ref[hd] + all_sublanes(reduce_rows(p, jnp.sum), jnp.add)
            p_ref[hd] = p.reshape(tk, tq).astype(BF16)
            alpha_ref[hd] = alpha
            m_ref[hd] = m_new

    def values_rescaled(j):
        for hd in range(MLA_HEADS):
            rows = slice(hd * MLA_V, (hd + 1) * MLA_V)
            pv = jnp.dot(vt_ref[j, rows, :], p_ref[hd], preferred_element_type=F32)
            acc = acc_ref[rows, :].reshape(slabs, SUBLANES, tq) * alpha_ref[hd][None]
            acc_ref[rows, :] = acc.reshape(MLA_V, tq) + pv

    def softmax_fixed_reference():
        for hd in range(MLA_HEADS):
            p = jnp.exp2(s_ref[hd].reshape(tk // SUBLANES, SUBLANES, tq))
            l_ref[hd] = l_ref[hd] + all_sublanes(reduce_rows(p, jnp.sum), jnp.add)
            p_ref[hd] = p.reshape(tk, tq).astype(BF16)

    def values_plain(j):
        for hd in range(MLA_HEADS):
            rows = slice(hd * MLA_V, (hd + 1) * MLA_V)
            acc_ref[rows, :] = acc_ref[rows, :] + jnp.dot(vt_ref[j, rows, :], p_ref[hd],
                                                          preferred_element_type=F32)

    def sweep(softmax, values):
        l_ref[...] = jnp.zeros(l_ref.shape, F32)
        acc_ref[...] = jnp.zeros(acc_ref.shape, F32)
        scores(i, True)

        @pl.when(i == 0)
        def _():
            softmax()

        @pl.when(i > 0)
        def _():
            softmax()
            scores(0, False)

            def body(t, carry):
                values(jnp.where(t == 1, i, t - 2))
                softmax()
                scores(t, False)
                return carry

            lax.fori_loop(1, i, body, 0)
            values(jnp.where(i == 1, i, i - 2))
            softmax()

        values(jnp.where(i == 0, 0, i - 1))

    @pl.when(fixed_ref[0] != 0)
    def _():
        sweep(softmax_fixed_reference, values_plain)

    @pl.when(fixed_ref[0] == 0)
    def _():
        m_ref[...] = jnp.full(m_ref.shape, -jnp.inf, F32)
        sweep(softmax_running_max, values_rescaled)

    for hd in range(MLA_HEADS):
        rows = slice(hd * MLA_V, (hd + 1) * MLA_V)
        acc = acc_ref[rows, :].reshape(slabs, SUBLANES, tq) / l_ref[hd][None]
        acc_ref[rows, :] = acc.reshape(MLA_V, tq)
    y_ref[...] = (acc_ref[...].T * z_ref[...].astype(F32)).astype(BF16)


def _attn(fixed, qb, kb, vbt, zb, batch, seq):
    tq = ATTN_TILE
    nq = seq // tq
    hw = MLA_HEADS * HEAD_PAD
    stat = pltpu.VMEM((MLA_HEADS, SUBLANES, tq), F32)
    return pl.pallas_call(
        _attn_body,
        grid=(batch, nq),
        in_specs=[pl.BlockSpec(memory_space=pltpu.SMEM),
                  pl.BlockSpec((tq, hw), lambda b, i: (b * nq + i, 0)),
                  pl.BlockSpec((seq, hw), lambda b, i: (b, 0), pipeline_mode=pl.Buffered(1)),
                  pl.BlockSpec((nq, BRANCH_W, tq), lambda b, i: (b, 0, 0), pipeline_mode=pl.Buffered(1)),
                  pl.BlockSpec((tq, BRANCH_W), lambda b, i: (b * nq + i, 0))],
        out_specs=pl.BlockSpec((tq, BRANCH_W), lambda b, i: (b * nq + i, 0)),
        out_shape=jax.ShapeDtypeStruct((batch * seq, BRANCH_W), BF16),
        scratch_shapes=[stat, stat, stat,
                        pltpu.VMEM((MLA_HEADS * MLA_V, tq), F32),
                        pltpu.VMEM((MLA_HEADS, tq, tq), F32),
                        pltpu.VMEM((MLA_HEADS, tq, tq), BF16)],
        compiler_params=pltpu.CompilerParams(dimension_semantics=("parallel", "arbitrary"),
                                             vmem_limit_bytes=VMEM_LIMIT),
        name="mla_attn",
    )(fixed, qb, kb, vbt, zb)


def _merge_body(x_ref, ya_ref, yb_ref, yc_ref, ng_ref, wgate_ref, bgate_ref, wbr_ref, wout_ref, o_ref):
    x = x_ref[...]
    h = (x * _rms(x, D_MODEL) * ng_ref[...]).astype(BF16)
    merged = None
    for n, y_ref in enumerate((ya_ref, yb_ref, yc_ref)):
        cols = slice(n * D_MODEL, (n + 1) * D_MODEL)
        logits = jnp.dot(h, wgate_ref[:, cols], preferred_element_type=F32) + bgate_ref[:, cols]
        term = _sigmoid(logits) * jnp.dot(y_ref[...], wbr_ref[n], preferred_element_type=F32)
        merged = term if merged is None else merged + term
    o_ref[...] = x + jnp.dot(merged.astype(BF16), wout_ref[...], preferred_element_type=F32)


def _merge(x2, ya, yb, yc, p):
    t = x2.shape[0]
    tm = TOKEN_TILE
    row = lambda w: pl.BlockSpec((tm, w), lambda i: (i, 0))
    consts = [p["norm_g"], p["w_gate"], p["b_gate"], p["w_branch"], p["w_out"]]
    return pl.pallas_call(
        _merge_body,
        grid=(t // tm,),
        in_specs=[row(D_MODEL), row(BRANCH_W), row(BRANCH_W), row(BRANCH_W)]
                 + [_const_spec(c.shape) for c in consts],
        out_specs=row(D_MODEL),
        out_shape=jax.ShapeDtypeStruct((t, D_MODEL), F32),
        compiler_params=pltpu.CompilerParams(dimension_semantics=("parallel",),
                                             vmem_limit_bytes=VMEM_LIMIT),
        name="merge",
    )(x2, ya, yb, yc, *consts)


def _pad_heads(w, heads, width):
    lead = w.shape[:-1]
    w = w.reshape(lead + (heads, width))
    w = jnp.pad(w, [(0, 0)] * len(lead) + [(0, 0), (0, HEAD_PAD - width)])
    return w.reshape(lead + (heads * HEAD_PAD,))


def _pack_layer(l, norm_g, w_in, b_gate, sg_ln_g, sg_ln_b, sg_w, sg_b, mla_cq_g, mla_ckv_g, mla_w_uq,
                mla_w_ukv, mla_q_g, mla_k_g, gla_w_gate, gla_b_gate, gla_o_g, w_branch, w_out):
    w = w_in[l]
    d = w.shape[0]
    zeros = lambda n: jnp.zeros((d, n), w.dtype)
    w_mix = jnp.concatenate([
        w[:, :1920], zeros(MLA_NOPE), w[:, 1920:1952], zeros(LANES - MLA_QK),
        w[:, 1952:3504], zeros(LANES - GLA_GATE_RANK), w[:, 3504:GATE_SRC]], axis=1)
    ukv = mla_w_ukv[l].reshape(MLA_KV_RANK, MLA_HEADS, MLA_NOPE + MLA_V)
    w_uk = _pad_heads(ukv[:, :, :MLA_NOPE].reshape(MLA_KV_RANK, -1), MLA_HEADS, MLA_NOPE)
    w_uvt = ukv[:, :, MLA_NOPE:].reshape(MLA_KV_RANK, -1).T
    pad_gain = lambda g: jnp.pad(g, (0, HEAD_PAD - MLA_QK)).reshape(1, HEAD_PAD)
    q_g = pad_gain(mla_q_g[l]) * (MLA_QK ** -0.5 * math.log2(math.e))
    k_g = pad_gain(mla_k_g[l])
    bound = MLA_QK * jnp.max(jnp.abs(q_g)) * jnp.max(jnp.abs(k_g)) * SCORE_BOUND_MARGIN
    fixed = bound <= FIXED_REFERENCE_MAX_BOUND
    bias_lane = (jnp.arange(HEAD_PAD) == BIAS_LANE).reshape(1, HEAD_PAD)
    return {
        "norm_g": norm_g[l].reshape(1, -1),
        "w_mix": w_mix.astype(BF16),
        "w_gate": w[:, GATE_SRC:].astype(BF16),
        "sg_ln_g": sg_ln_g[l].reshape(1, -1),
        "sg_ln_b": sg_ln_b[l].reshape(1, -1),
        "sg_w": sg_w[l],
        "sg_bt": sg_b[l].T,
        "cq_g": mla_cq_g[l].reshape(1, -1),
        "ckv_g": mla_ckv_g[l].reshape(1, -1),
        "w_uq": _pad_heads(mla_w_uq[l], MLA_HEADS, MLA_QK).astype(BF16),
        "w_uk": w_uk.astype(BF16),
        "w_uvt": w_uvt.astype(BF16),
        "q_g": q_g,
        "k_g": k_g,
        "q_pad": jnp.where(bias_lane & fixed, -bound, 0.0).astype(F32),
        "k_pad": jnp.where(bias_lane, 1.0, 0.0).astype(F32),
        "fixed": fixed.astype(jnp.int32).reshape(1),
        "w_gu": jnp.pad(gla_w_gate[l], ((0, LANES - GLA_GATE_RANK), (0, 0))).astype(BF16),
        "b_gu": gla_b_gate[l].reshape(1, -1),
        "b_gate": b_gate[l].reshape(1, -1),
        "o_g": gla_o_g[l].reshape(1, -1),
        "w_branch": w_branch[l].astype(BF16),
        "w_out": w_out[l].astype(BF16),
    }


def kernel(x, positions, norm_g, w_in, b_gate, sg_ln_g, sg_ln_b, sg_w, sg_b, mla_cq_g, mla_ckv_g, mla_w_uq, mla_w_ukv, mla_q_g, mla_k_g, gla_w_gate, gla_b_gate, gla_o_g, w_branch, w_out):
    batch, seq, d = x.shape
    assert d == D_MODEL and seq % max(ATTN_TILE, GLA_TILE) == 0 and (batch * seq) % TOKEN_TILE == 0
    assert TOKEN_TILE % ATTN_TILE == 0 and TOKEN_TILE % SG_BLOCK == 0
    depth = w_in.shape[0]
    cos_t, sin_t = _rope_tables(positions)
    x2 = x.reshape(batch * seq, d)
    for l in range(depth):
        p = _pack_layer(l, norm_g, w_in, b_gate, sg_ln_g, sg_ln_b, sg_w, sg_b, mla_cq_g, mla_ckv_g,
                        mla_w_uq, mla_w_ukv, mla_q_g, mla_k_g, gla_w_gate, gla_b_gate, gla_o_g,
                        w_branch, w_out)
        ya, qb, kb, vbt, zb, qc, kc, vc, la, zc = _in_proj(x2, cos_t, sin_t, p)
        yc = _gla(qc, kc, vc, la, zc, p["o_g"], batch, seq)
        yb = _attn(p["fixed"], qb, kb, vbt, zb, batch, seq)
        x2 = _merge(x2, ya, yb, yc, p)
    return x2.reshape(batch, seq, d)
```

```python
import math

import jax
import jax.numpy as jnp
from jax import lax
from jax.experimental import pallas as pl
from jax.experimental.pallas import tpu as pltpu

F32 = jnp.float32
BF16 = jnp.bfloat16

D_MODEL = 1024
CHUNK = 64
BRANCH_W = 512
N_BRANCH = 3
EPS = 1e-6
SG_BLOCK = 128
SG_GROUPS = 4
MLA_HEADS = 8
MLA_NOPE = 64
MLA_ROPE = 32
MLA_QK = MLA_NOPE + MLA_ROPE
MLA_V = 64
MLA_Q_RANK = 256
MLA_KV_RANK = 128
ROPE_THETA = 10000.0
GLA_HEADS = 4
GLA_DK = 64
GLA_DV = 128
GLA_GATE_RANK = 16
GLA_TAU = 16.0

LANES = 128
SUBLANES = 8
HEAD_PAD = LANES
ROPE_HALF = MLA_ROPE // 2
X1_LO = MLA_NOPE
X2_LO = MLA_NOPE + ROPE_HALF
BIAS_LANE = MLA_QK

A_U, A_V, A_Z = 0, 512, 1024
B_CQ, B_CKV, B_KR, B_Z = 1536, 1792, 1920, 2048
C_Q, C_K, C_V, C_G, C_Z = 2560, 2816, 3072, 3584, 3712
MIX_COLS_PAD = 4224
GATE_SRC = 4016

TOKEN_TILE = 512
ATTN_TILE = 256
GLA_TILE = 256
ROWS_PER_STEP = 2
VMEM_LIMIT = 56 * 1024 * 1024

SCORE_BOUND_MARGIN = 1.02
FIXED_REFERENCE_MAX_BOUND = 40.0

NT_DIMS = (((1,), (1,)), ((), ()))
TN_DIMS = (((0,), (0,)), ((), ()))


def _const_spec(shape):
    zeros = (0,) * len(shape)
    return pl.BlockSpec(shape, lambda *_: zeros, pipeline_mode=pl.Buffered(1))


def _sigmoid(x):
    return 0.5 * (jnp.tanh(0.5 * x) + 1.0)


def _silu(x):
    return x * _sigmoid(x)


def _gelu_tanh(x):
    c = math.sqrt(2.0 / math.pi)
    return 0.5 * x * (1.0 + jnp.tanh(c * (x + 0.044715 * (x * x * x))))


def _log_sigmoid(x):
    return jnp.minimum(x, 0.0) - jnp.log(1.0 + jnp.exp(-jnp.abs(x)))


def _rms(x, width):
    return lax.rsqrt(jnp.sum(x * x, axis=-1, keepdims=True) * (1.0 / width) + EPS)


def _rope_body(pos_ref, cos_ref, sin_ref):
    lane = lax.broadcasted_iota(jnp.int32, (1, LANES), 1)
    in_x1 = (lane >= X1_LO) & (lane < X2_LO)
    in_x2 = (lane >= X2_LO) & (lane < MLA_QK)
    fidx = jnp.where(in_x1, lane - X1_LO, lane - X2_LO).astype(F32)
    inv = 1.0 / jnp.exp(fidx * (2.0 / MLA_ROPE) * math.log(ROPE_THETA))
    ang = pos_ref[...].astype(F32) * inv
    cos = jnp.cos(ang)
    sin = jnp.sin(ang)
    rope = in_x1 | in_x2
    cos_ref[...] = jnp.where(rope, cos, jnp.where(lane < MLA_NOPE, 1.0, 0.0))
    sin_ref[...] = jnp.where(in_x1, -sin, jnp.where(in_x2, sin, 0.0))


def _rope_tables(positions):
    t = positions.size
    tm = 1024 if t % 1024 == 0 else ATTN_TILE
    return pl.pallas_call(
        _rope_body,
        grid=(t // tm,),
        in_specs=[pl.BlockSpec((tm, 1), lambda i: (i, 0))],
        out_specs=[pl.BlockSpec((tm, LANES), lambda i: (i, 0))] * 2,
        out_shape=[jax.ShapeDtypeStruct((t, LANES), F32)] * 2,
        name="rope_tables",
    )(positions.reshape(t, 1))


def _in_proj_body(x_ref, cos_ref, sin_ref, ng_ref, win_ref, lng_ref, lnb_ref, sgw_ref, sgbt_ref,
                  cqg_ref, ckvg_ref, wuq_ref, wuk_ref, wuvt_ref, qg_ref, kg_ref, qpad_ref, kpad_ref,
                  wgu_ref, bgu_ref,
                  ya_ref, qb_ref, kb_ref, vbt_ref, zb_ref, qc_ref, kc_ref, vc_ref, la_ref, zc_ref):
    tm = x_ref.shape[0]
    x = x_ref[...]
    h = (x * _rms(x, D_MODEL) * ng_ref[...]).astype(BF16)

    def proj(lo, width):
        return jnp.dot(h, win_ref[:, lo:lo + width], preferred_element_type=F32)

    u = _gelu_tanh(proj(A_U, BRANCH_W))
    v = _gelu_tanh(proj(A_V, BRANCH_W))
    mu = jnp.mean(v, axis=-1, keepdims=True)
    vc = v - mu
    vn = vc * lax.rsqrt(jnp.mean(vc * vc, axis=-1, keepdims=True) + EPS)
    vn = (vn * lng_ref[...] + lnb_ref[...]).astype(BF16)
    uz = u * _silu(proj(A_Z, BRANCH_W))
    ri = lax.broadcasted_iota(jnp.int32, (SG_BLOCK, SG_BLOCK), 0) // CHUNK
    ci = lax.broadcasted_iota(jnp.int32, (SG_BLOCK, SG_BLOCK), 1) // CHUNK
    chunk_causal = ri >= ci
    for g in range(SG_GROUPS):
        wg = jnp.where(chunk_causal, sgw_ref[g], 0.0).astype(BF16)
        bias = sgbt_ref[:, g:g + 1]
        cols = slice(g * LANES, (g + 1) * LANES)
        for n in range(tm // SG_BLOCK):
            rows = slice(n * SG_BLOCK, (n + 1) * SG_BLOCK)
            sv = jnp.dot(wg, vn[rows, cols], preferred_element_type=F32) + bias
            ya_ref[rows, cols] = (uz[rows, cols] * sv).astype(BF16)

    cq = proj(B_CQ, MLA_Q_RANK)
    cqn = (cq * _rms(cq, MLA_Q_RANK) * cqg_ref[...]).astype(BF16)
    q = jnp.dot(cqn, wuq_ref[...], preferred_element_type=F32)
    ckv = proj(B_CKV, MLA_KV_RANK)
    ckvn = (ckv * _rms(ckv, MLA_KV_RANK) * ckvg_ref[...]).astype(BF16)
    kn = jnp.dot(ckvn, wuk_ref[...], preferred_element_type=F32)
    for s in range(tm // ATTN_TILE):
        vbt_ref[s] = lax.dot_general(wuvt_ref[...], ckvn[s * ATTN_TILE:(s + 1) * ATTN_TILE], NT_DIMS,
                                     preferred_element_type=F32).astype(BF16)
    kr = proj(B_KR, LANES)
    cos = cos_ref[...]
    sin = sin_ref[...]
    lane = lax.broadcasted_iota(jnp.int32, (1, LANES), 1)
    in_x1 = (lane >= X1_LO) & (lane < X2_LO)

    def norm_rope(xh, gain, pad_row):
        xn = xh * _rms(xh, MLA_QK) * gain
        partner = jnp.where(in_x1, pltpu.roll(xn, LANES - ROPE_HALF, 1), pltpu.roll(xn, ROPE_HALF, 1))
        return (xn * cos + partner * sin + pad_row).astype(BF16)

    half = BRANCH_W // 2

    def gated(ref, src, part):
        def run():
            ref[:, part * half:(part + 1) * half] = _silu(proj(src + part * half, half)).astype(BF16)
        return run

    def plain(ref, src, part, scale=None):
        def run():
            r = proj(src + part * half, half)
            ref[:, part * half:(part + 1) * half] = (r if scale is None else r * scale).astype(BF16)
        return run

    def decay():
        glr = proj(C_G, LANES).astype(BF16)
        gl = jnp.dot(glr, wgu_ref[...], preferred_element_type=F32) + bgu_ref[...]
        la_ref[...] = _log_sigmoid(gl) * (1.0 / GLA_TAU)

    pieces = [gated(zb_ref, B_Z, 0), gated(zb_ref, B_Z, 1), plain(qc_ref, C_Q, 0, GLA_DK ** -0.5),
              plain(kc_ref, C_K, 0), plain(vc_ref, C_V, 0), plain(vc_ref, C_V, 1), decay,
              gated(zc_ref, C_Z, 0), gated(zc_ref, C_Z, 1)]

    for hd in range(MLA_HEADS):
        cols = slice(hd * HEAD_PAD, (hd + 1) * HEAD_PAD)
        pieces[hd]()
        qb_ref[:, cols] = norm_rope(q[:, cols], qg_ref[...], qpad_ref[...])
        kb_ref[:, cols] = norm_rope(kn[:, cols] + kr, kg_ref[...], kpad_ref[...])
    for piece in pieces[MLA_HEADS:]:
        piece()


def _in_proj(x2, cos_t, sin_t, p):
    t = x2.shape[0]
    tm = TOKEN_TILE
    row = lambda w: pl.BlockSpec((tm, w), lambda i: (i, 0))
    consts = [p["norm_g"], p["w_mix"], p["sg_ln_g"], p["sg_ln_b"], p["sg_w"], p["sg_bt"],
              p["cq_g"], p["ckv_g"], p["w_uq"], p["w_uk"], p["w_uvt"], p["q_g"], p["k_g"],
              p["q_pad"], p["k_pad"], p["w_gu"], p["b_gu"]]
    out_widths = [(BRANCH_W, BF16), (MLA_HEADS * HEAD_PAD, BF16), (MLA_HEADS * HEAD_PAD, BF16),
                  None, (BRANCH_W, BF16),
                  (GLA_HEADS * GLA_DK, BF16), (GLA_HEADS * GLA_DK, BF16), (GLA_HEADS * GLA_DV, BF16),
                  (GLA_HEADS * GLA_DK, F32), (BRANCH_W, BF16)]
    per_step = tm // ATTN_TILE
    vt_spec = pl.BlockSpec((per_step, MLA_HEADS * MLA_V, ATTN_TILE), lambda i: (i, 0, 0))
    vt_shape = jax.ShapeDtypeStruct((t // ATTN_TILE, MLA_HEADS * MLA_V, ATTN_TILE), BF16)
    return pl.pallas_call(
        _in_proj_body,
        grid=(t // tm,),
        in_specs=[row(D_MODEL), row(LANES), row(LANES)] + [_const_spec(c.shape) for c in consts],
        out_specs=[vt_spec if o is None else row(o[0]) for o in out_widths],
        out_shape=[vt_shape if o is None else jax.ShapeDtypeStruct((t, o[0]), o[1]) for o in out_widths],
        compiler_params=pltpu.CompilerParams(dimension_semantics=("parallel",),
                                             vmem_limit_bytes=VMEM_LIMIT),
        name="in_proj",
    )(x2, cos_t, sin_t, *consts)


def _gla_body(q_ref, k_ref, v_ref, la_ref, z_ref, og_ref, y_ref, state_ref):
    @pl.when(pl.program_id(1) == 0)
    def _():
        state_ref[...] = jnp.zeros_like(state_ref)

    n_rows, tc = q_ref.shape[0], q_ref.shape[1]
    kw = GLA_HEADS * GLA_DK
    stack = GLA_HEADS * CHUNK
    ri = lax.broadcasted_iota(jnp.int32, (tc, tc), 0)
    ci = lax.broadcasted_iota(jnp.int32, (tc, tc), 1)
    tri = jnp.where((ri // CHUNK == ci // CHUNK) & (ri >= ci), 1.0, 0.0).astype(BF16)
    si = lax.broadcasted_iota(jnp.int32, (stack, CHUNK), 0) % CHUNK
    sj = lax.broadcasted_iota(jnp.int32, (stack, CHUNK), 1)
    causal = si >= sj
    lane = lax.broadcasted_iota(jnp.int32, (1, kw), 1)
    cums = []
    for r in range(n_rows):
        la = la_ref[r]
        la_hi = la.astype(BF16)
        la_lo = (la - la_hi.astype(F32)).astype(BF16)
        cums.append(jnp.dot(tri, la_hi, preferred_element_type=F32)
                    + jnp.dot(tri, la_lo, preferred_element_type=F32))
    for c in range(tc // CHUNK):
        rows = slice(c * CHUNK, (c + 1) * CHUNK)
        for r in range(n_rows):
            b = cums[r][rows]
            b_last = b[CHUNK - 1:CHUNK, :]
            qt = q_ref[r, rows, :].astype(F32) * jnp.exp(b)
            kf = k_ref[r, rows, :].astype(F32)
            kt = (kf * jnp.exp(-b)).astype(BF16)
            ks = (kf * jnp.exp(b_last - b)).astype(BF16)
            dec = jnp.exp(b_last)
            q_stack = jnp.concatenate(
                [jnp.where((lane >= hd * GLA_DK) & (lane < (hd + 1) * GLA_DK), qt, 0.0).astype(BF16)
                 for hd in range(GLA_HEADS)], axis=0)
            v = v_ref[r, rows, :]
            att = lax.dot_general(q_stack, kt, NT_DIMS, preferred_element_type=F32)
            att = jnp.where(causal, att, 0.0).astype(BF16)
            st = state_ref[r]
            o_intra = jnp.dot(att, v, preferred_element_type=F32)
            o_inter = lax.dot_general(q_stack, st.astype(BF16), NT_DIMS, preferred_element_type=F32)
            state_ref[r] = st * dec + lax.dot_general(v, ks, TN_DIMS, preferred_element_type=F32)
            for hd in range(GLA_HEADS):
                srows = slice(hd * CHUNK, (hd + 1) * CHUNK)
                cols = slice(hd * GLA_DV, (hd + 1) * GLA_DV)
                o = o_intra[srows, cols] + o_inter[srows, cols]
                on = o * _rms(o, GLA_DV) * og_ref[...]
                y_ref[r, rows, cols] = (on * z_ref[r, rows, cols].astype(F32)).astype(BF16)


def _gla(qc, kc, vc, la, zc, o_g, batch, seq):
    tc = GLA_TILE
    rps = ROWS_PER_STEP
    kw = GLA_HEADS * GLA_DK
    vw = GLA_HEADS * GLA_DV
    row = lambda w: pl.BlockSpec((rps, tc, w), lambda b, s: (b, s, 0))
    rows3 = lambda a: a.reshape(batch, seq, a.shape[-1])
    return pl.pallas_call(
        _gla_body,
        grid=(batch // rps, seq // tc),
        in_specs=[row(kw), row(kw), row(vw), row(kw), row(vw), _const_spec(o_g.shape)],
        out_specs=row(vw),
        out_shape=jax.ShapeDtypeStruct((batch, seq, vw), BF16),
        scratch_shapes=[pltpu.VMEM((rps, vw, kw), F32)],
        compiler_params=pltpu.CompilerParams(dimension_semantics=("parallel", "arbitrary"),
                                             vmem_limit_bytes=VMEM_LIMIT),
        name="gla",
    )(rows3(qc), rows3(kc), rows3(vc), rows3(la), rows3(zc), o_g).reshape(batch * seq, vw)


def _attn_body(fixed_ref, q_ref, k_ref, vt_ref, z_ref, y_ref, m_ref, l_ref, alpha_ref, acc_ref, s_ref,
               p_ref):
    tq = q_ref.shape[0]
    tk = tq
    i = pl.program_id(1)
    ki = lax.broadcasted_iota(jnp.int32, (tk, tq), 0) // CHUNK
    qi = lax.broadcasted_iota(jnp.int32, (tk, tq), 1) // CHUNK
    visible = ki <= qi
    slabs = MLA_V // SUBLANES

    def all_sublanes(x, op):
        for shift in (4, 2, 1):
            x = op(x, pltpu.roll(x, shift, 0))
        return x

    def reduce_rows(st, op):
        parts = op(st.reshape(4, tk // SUBLANES // 4, SUBLANES, tq), axis=1)
        return op(parts, axis=0)

    def scores(hd, j, masked):
        start = pl.multiple_of(j * tk, tk)
        cols = slice(hd * HEAD_PAD, (hd + 1) * HEAD_PAD)
        st = lax.dot_general(k_ref[pl.ds(start, tk), cols], q_ref[:, cols], NT_DIMS,
                             preferred_element_type=F32)
        if masked:
            st = jnp.where(visible, st, -jnp.inf)
        s_ref[hd] = st

    def softmax_running_max(hd):
        st = s_ref[hd].reshape(tk // SUBLANES, SUBLANES, tq)
        m_prev = m_ref[hd]
        m_new = jnp.maximum(m_prev, all_sublanes(reduce_rows(st, jnp.max), jnp.maximum))
        alpha = jnp.exp2(m_prev - m_new)
        p = jnp.exp2(st - m_new[None])
        l_ref[hd] = alpha * l_ref[hd] + all_sublanes(reduce_rows(p, jnp.sum), jnp.add)
        p_ref[hd] = p.reshape(tk, tq).astype(BF16)
        alpha_ref[hd] = alpha
        m_ref[hd] = m_new

    def values_rescaled(hd, j):
        rows = slice(hd * MLA_V, (hd + 1) * MLA_V)
        pv = jnp.dot(vt_ref[j, rows, :], p_ref[hd], preferred_element_type=F32)
        acc = acc_ref[rows, :].reshape(slabs, SUBLANES, tq) * alpha_ref[hd][None]
        acc_ref[rows, :] = acc.reshape(MLA_V, tq) + pv

    def softmax_fixed_reference(hd):
        p = jnp.exp2(s_ref[hd].reshape(tk // SUBLANES, SUBLANES, tq))
        l_ref[hd] = l_ref[hd] + all_sublanes(reduce_rows(p, jnp.sum), jnp.add)
        p_ref[hd] = p.reshape(tk, tq).astype(BF16)

    def values_plain(hd, j):
        rows = slice(hd * MLA_V, (hd + 1) * MLA_V)
        acc_ref[rows, :] = acc_ref[rows, :] + jnp.dot(vt_ref[j, rows, :], p_ref[hd],
                                                      preferred_element_type=F32)

    def sweep(softmax, values):
        def round_(prev_block, next_block):
            for hd in range(MLA_HEADS):
                if prev_block is not None:
                    values(hd, prev_block)
                softmax(hd)
                if next_block is not None:
                    scores(hd, next_block, False)

        l_ref[...] = jnp.zeros(l_ref.shape, F32)
        acc_ref[...] = jnp.zeros(acc_ref.shape, F32)
        for hd in range(MLA_HEADS):
            scores(hd, i, True)

        @pl.when(i == 0)
        def _():
            round_(None, None)

        @pl.when(i > 0)
        def _():
            round_(None, 0)

            def body(t, carry):
                round_(jnp.where(t == 1, i, t - 2), t)
                return carry

            lax.fori_loop(1, i, body, 0)
            round_(jnp.where(i == 1, i, i - 2), None)

        for hd in range(MLA_HEADS):
            values(hd, jnp.where(i == 0, 0, i - 1))

    @pl.when(fixed_ref[0] != 0)
    def _():
        sweep(softmax_fixed_reference, values_plain)

    @pl.when(fixed_ref[0] == 0)
    def _():
        m_ref[...] = jnp.full(m_ref.shape, -jnp.inf, F32)
        sweep(softmax_running_max, values_rescaled)

    for hd in range(MLA_HEADS):
        rows = slice(hd * MLA_V, (hd + 1) * MLA_V)
        acc = acc_ref[rows, :].reshape(slabs, SUBLANES, tq) / l_ref[hd][None]
        acc_ref[rows, :] = acc.reshape(MLA_V, tq)
    y_ref[...] = (acc_ref[...].T * z_ref[...].astype(F32)).astype(BF16)


def _attn(fixed, qb, kb, vbt, zb, batch, seq):
    tq = ATTN_TILE
    nq = seq // tq
    hw = MLA_HEADS * HEAD_PAD
    stat = pltpu.VMEM((MLA_HEADS, SUBLANES, tq), F32)
    return pl.pallas_call(
        _attn_body,
        grid=(batch, nq),
        in_specs=[pl.BlockSpec(memory_space=pltpu.SMEM),
                  pl.BlockSpec((tq, hw), lambda b, i: (b * nq + i, 0)),
                  pl.BlockSpec((seq, hw), lambda b, i: (b, 0), pipeline_mode=pl.Buffered(1)),
                  pl.BlockSpec((nq, BRANCH_W, tq), lambda b, i: (b, 0, 0), pipeline_mode=pl.Buffered(1)),
                  pl.BlockSpec((tq, BRANCH_W), lambda b, i: (b * nq + i, 0))],
        out_specs=pl.BlockSpec((tq, BRANCH_W), lambda b, i: (b * nq + i, 0)),
        out_shape=jax.ShapeDtypeStruct((batch * seq, BRANCH_W), BF16),
        scratch_shapes=[stat, stat, stat,
                        pltpu.VMEM((MLA_HEADS * MLA_V, tq), F32),
                        pltpu.VMEM((MLA_HEADS, tq, tq), F32),
                        pltpu.VMEM((MLA_HEADS, tq, tq), BF16)],
        compiler_params=pltpu.CompilerParams(dimension_semantics=("parallel", "arbitrary"),
                                             vmem_limit_bytes=VMEM_LIMIT),
        name="mla_attn",
    )(fixed, qb, kb, vbt, zb)


def _merge_body(x_ref, ya_ref, yb_ref, yc_ref, ng_ref, wgate_ref, bgate_ref, wbr_ref, wout_ref, o_ref):
    x = x_ref[...]
    h = (x * _rms(x, D_MODEL) * ng_ref[...]).astype(BF16)
    merged = None
    for n, y_ref in enumerate((ya_ref, yb_ref, yc_ref)):
        cols = slice(n * D_MODEL, (n + 1) * D_MODEL)
        logits = jnp.dot(h, wgate_ref[:, cols], preferred_element_type=F32) + bgate_ref[:, cols]
        term = _sigmoid(logits) * jnp.dot(y_ref[...], wbr_ref[n], preferred_element_type=F32)
        merged = term if merged is None else merged + term
    o_ref[...] = x + jnp.dot(merged.astype(BF16), wout_ref[...], preferred_element_type=F32)


def _merge(x2, ya, yb, yc, p):
    t = x2.shape[0]
    tm = TOKEN_TILE
    row = lambda w: pl.BlockSpec((tm, w), lambda i: (i, 0))
    consts = [p["norm_g"], p["w_gate"], p["b_gate"], p["w_branch"], p["w_out"]]
    return pl.pallas_call(
        _merge_body,
        grid=(t // tm,),
        in_specs=[row(D_MODEL), row(BRANCH_W), row(BRANCH_W), row(BRANCH_W)]
                 + [_const_spec(c.shape) for c in consts],
        out_specs=row(D_MODEL),
        out_shape=jax.ShapeDtypeStruct((t, D_MODEL), F32),
        compiler_params=pltpu.CompilerParams(dimension_semantics=("parallel",),
                                             vmem_limit_bytes=VMEM_LIMIT),
        name="merge",
    )(x2, ya, yb, yc, *consts)


def _pad_heads(w, heads, width):
    lead = w.shape[:-1]
    w = w.reshape(lead + (heads, width))
    w = jnp.pad(w, [(0, 0)] * len(lead) + [(0, 0), (0, HEAD_PAD - width)])
    return w.reshape(lead + (heads * HEAD_PAD,))


def _pack_layer(l, norm_g, w_in, b_gate, sg_ln_g, sg_ln_b, sg_w, sg_b, mla_cq_g, mla_ckv_g, mla_w_uq,
                mla_w_ukv, mla_q_g, mla_k_g, gla_w_gate, gla_b_gate, gla_o_g, w_branch, w_out):
    w = w_in[l]
    d = w.shape[0]
    zeros = lambda n: jnp.zeros((d, n), w.dtype)
    w_mix = jnp.concatenate([
        w[:, :1920], zeros(MLA_NOPE), w[:, 1920:1952], zeros(LANES - MLA_QK),
        w[:, 1952:3504], zeros(LANES - GLA_GATE_RANK), w[:, 3504:GATE_SRC]], axis=1)
    ukv = mla_w_ukv[l].reshape(MLA_KV_RANK, MLA_HEADS, MLA_NOPE + MLA_V)
    w_uk = _pad_heads(ukv[:, :, :MLA_NOPE].reshape(MLA_KV_RANK, -1), MLA_HEADS, MLA_NOPE)
    w_uvt = ukv[:, :, MLA_NOPE:].reshape(MLA_KV_RANK, -1).T
    pad_gain = lambda g: jnp.pad(g, (0, HEAD_PAD - MLA_QK)).reshape(1, HEAD_PAD)
    q_g = pad_gain(mla_q_g[l]) * (MLA_QK ** -0.5 * math.log2(math.e))
    k_g = pad_gain(mla_k_g[l])
    bound = MLA_QK * jnp.max(jnp.abs(q_g)) * jnp.max(jnp.abs(k_g)) * SCORE_BOUND_MARGIN
    fixed = bound <= FIXED_REFERENCE_MAX_BOUND
    bias_lane = (jnp.arange(HEAD_PAD) == BIAS_LANE).reshape(1, HEAD_PAD)
    return {
        "norm_g": norm_g[l].reshape(1, -1),
        "w_mix": w_mix.astype(BF16),
        "w_gate": w[:, GATE_SRC:].astype(BF16),
        "sg_ln_g": sg_ln_g[l].reshape(1, -1),
        "sg_ln_b": sg_ln_b[l].reshape(1, -1),
        "sg_w": sg_w[l],
        "sg_bt": sg_b[l].T,
        "cq_g": mla_cq_g[l].reshape(1, -1),
        "ckv_g": mla_ckv_g[l].reshape(1, -1),
        "w_uq": _pad_heads(mla_w_uq[l], MLA_HEADS, MLA_QK).astype(BF16),
        "w_uk": w_uk.astype(BF16),
        "w_uvt": w_uvt.astype(BF16),
        "q_g": q_g,
        "k_g": k_g,
        "q_pad": jnp.where(bias_lane & fixed, -bound, 0.0).astype(F32),
        "k_pad": jnp.where(bias_lane, 1.0, 0.0).astype(F32),
        "fixed": fixed.astype(jnp.int32).reshape(1),
        "w_gu": jnp.pad(gla_w_gate[l], ((0, LANES - GLA_GATE_RANK), (0, 0))).astype(BF16),
        "b_gu": gla_b_gate[l].reshape(1, -1),
        "b_gate": b_gate[l].reshape(1, -1),
        "o_g": gla_o_g[l].reshape(1, -1),
        "w_branch": w_branch[l].astype(BF16),
        "w_out": w_out[l].astype(BF16),
    }


def kernel(x, positions, norm_g, w_in, b_gate, sg_ln_g, sg_ln_b, sg_w, sg_b, mla_cq_g, mla_ckv_g, mla_w_uq, mla_w_ukv, mla_q_g, mla_k_g, gla_w_gate, gla_b_gate, gla_o_g, w_branch, w_out):
    batch, seq, d = x.shape
    assert d == D_MODEL and seq % max(ATTN_TILE, GLA_TILE) == 0 and (batch * seq) % TOKEN_TILE == 0
    assert TOKEN_TILE % ATTN_TILE == 0 and TOKEN_TILE % SG_BLOCK == 0
    depth = w_in.shape[0]
    cos_t, sin_t = _rope_tables(positions)
    x2 = x.reshape(batch * seq, d)
    for l in range(depth):
        p = _pack_layer(l, norm_g, w_in, b_gate, sg_ln_g, sg_ln_b, sg_w, sg_b, mla_cq_g, mla_ckv_g,
                        mla_w_uq, mla_w_ukv, mla_q_g, mla_k_g, gla_w_gate, gla_b_gate, gla_o_g,
                        w_branch, w_out)
        ya, qb, kb, vbt, zb, qc, kc, vc, la, zc = _in_proj(x2, cos_t, sin_t, p)
        yc = _gla(qc, kc, vc, la, zc, p["o_g"], batch, seq)
        yb = _attn(p["fixed"], qb, kb, vbt, zb, batch, seq)
        x2 = _merge(x2, ya, yb, yc, p)
    return x2.reshape(batch, seq, d)
```

```python
import math

import jax
import jax.numpy as jnp
from jax import lax
from jax.experimental import pallas as pl
from jax.experimental.pallas import tpu as pltpu

F32 = jnp.float32
BF16 = jnp.bfloat16

D_MODEL = 1024
CHUNK = 64
BRANCH_W = 512
N_BRANCH = 3
EPS = 1e-6
SG_BLOCK = 128
SG_GROUPS = 4
MLA_HEADS = 8
MLA_NOPE = 64
MLA_ROPE = 32
MLA_QK = MLA_NOPE + MLA_ROPE
MLA_V = 64
MLA_Q_RANK = 256
MLA_KV_RANK = 128
ROPE_THETA = 10000.0
GLA_HEADS = 4
GLA_DK = 64
GLA_DV = 128
GLA_GATE_RANK = 16
GLA_TAU = 16.0

LANES = 128
SUBLANES = 8
HEAD_PAD = LANES
ROPE_HALF = MLA_ROPE // 2
X1_LO = 0
X2_LO = LANES // 2
BIAS_LANE = MLA_QK
GATE_LANE = BIAS_LANE + 1


def _head_lane_source():
    src = [-1] * LANES
    nope_lanes = list(range(ROPE_HALF, X2_LO)) + list(range(X2_LO + ROPE_HALF, MLA_QK))
    for f, lane in enumerate(nope_lanes):
        src[lane] = f
    for f in range(ROPE_HALF):
        src[X1_LO + f] = MLA_NOPE + f
        src[X2_LO + f] = MLA_NOPE + ROPE_HALF + f
    return src


HEAD_LANE_SOURCE = _head_lane_source()

A_U, A_V, A_Z = 0, 512, 1024
B_CQ, B_CKV, B_KR, B_Z = 1536, 1792, 1920, 2048
C_Q, C_K, C_V, C_Z = 2560, 2816, 3072, 3584
MIX_COLS_PAD = 4096
GATE_SRC = 4016

TOKEN_TILE = 512
ATTN_TILE = 256
ATTN_Q_TILE = 2 * ATTN_TILE
GLA_TILE = 256
ROWS_PER_STEP = 2
VMEM_LIMIT = 56 * 1024 * 1024

SCORE_BOUND_MARGIN = 1.02
FIXED_REFERENCE_MAX_BOUND = 40.0

NT_DIMS = (((1,), (1,)), ((), ()))
TN_DIMS = (((0,), (0,)), ((), ()))


def _const_spec(shape):
    zeros = (0,) * len(shape)
    return pl.BlockSpec(shape, lambda *_: zeros, pipeline_mode=pl.Buffered(1))


def _sigmoid(x):
    return 0.5 * (jnp.tanh(0.5 * x) + 1.0)


def _silu(x):
    return x * _sigmoid(x)


def _gelu_tanh(x):
    c = math.sqrt(2.0 / math.pi)
    return 0.5 * x * (1.0 + jnp.tanh(c * (x + 0.044715 * (x * x * x))))


def _log_sigmoid(x):
    return jnp.minimum(x, 0.0) - jnp.log(1.0 + jnp.exp(-jnp.abs(x)))


def _rms(x, width):
    return lax.rsqrt(jnp.sum(x * x, axis=-1, keepdims=True) * (1.0 / width) + EPS)


def _rope_body(pos_ref, cos_ref, sin_ref):
    lane = lax.broadcasted_iota(jnp.int32, (1, LANES), 1)
    in_x1 = (lane >= X1_LO) & (lane < X1_LO + ROPE_HALF)
    in_x2 = (lane >= X2_LO) & (lane < X2_LO + ROPE_HALF)
    fidx = jnp.where(in_x1, lane - X1_LO, lane - X2_LO).astype(F32)
    inv = 1.0 / jnp.exp(fidx * (2.0 / MLA_ROPE) * math.log(ROPE_THETA))
    ang = pos_ref[...].astype(F32) * inv
    cos = jnp.cos(ang)
    sin = jnp.sin(ang)
    rope = in_x1 | in_x2
    cos_ref[...] = jnp.where(rope, cos, jnp.where(lane < MLA_QK, 1.0, 0.0))
    sin_ref[...] = jnp.where(in_x1, -sin, jnp.where(in_x2, sin, 0.0))


def _rope_tables(positions):
    t = positions.size
    tm = 1024 if t % 1024 == 0 else ATTN_TILE
    return pl.pallas_call(
        _rope_body,
        grid=(t // tm,),
        in_specs=[pl.BlockSpec((tm, 1), lambda i: (i, 0))],
        out_specs=[pl.BlockSpec((tm, LANES), lambda i: (i, 0))] * 2,
        out_shape=[jax.ShapeDtypeStruct((t, LANES), F32)] * 2,
        name="rope_tables",
    )(positions.reshape(t, 1))


def _in_proj_body(x_ref, cos_ref, sin_ref, ng_ref, win_ref, lng_ref, lnb_ref, sgw_ref, sgbt_ref,
                  cqg_ref, ckvg_ref, wuq_ref, wuk_ref, wuvt_ref, qg_ref, kg_ref, qpad_ref, kpad_ref,
                  wgu_ref, bgu_ref,
                  ya_ref, qb_ref, kb_ref, vbt_ref, zb_ref, qc_ref, kc_ref, vc_ref, la_ref, zc_ref):
    tm = x_ref.shape[0]
    x = x_ref[...]
    h = (x * _rms(x, D_MODEL) * ng_ref[...]).astype(BF16)

    def proj(lo, width):
        return jnp.dot(h, win_ref[:, lo:lo + width], preferred_element_type=F32)

    u = _gelu_tanh(proj(A_U, BRANCH_W))
    v = _gelu_tanh(proj(A_V, BRANCH_W))
    mu = jnp.mean(v, axis=-1, keepdims=True)
    vc = v - mu
    vn = vc * lax.rsqrt(jnp.mean(vc * vc, axis=-1, keepdims=True) + EPS)
    vn = (vn * lng_ref[...] + lnb_ref[...]).astype(BF16)
    uz = u * _silu(proj(A_Z, BRANCH_W))
    ri = lax.broadcasted_iota(jnp.int32, (SG_BLOCK, SG_BLOCK), 0) // CHUNK
    ci = lax.broadcasted_iota(jnp.int32, (SG_BLOCK, SG_BLOCK), 1) // CHUNK
    chunk_causal = ri >= ci
    for g in range(SG_GROUPS):
        wg = jnp.where(chunk_causal, sgw_ref[g], 0.0).astype(BF16)
        bias = sgbt_ref[:, g:g + 1]
        cols = slice(g * LANES, (g + 1) * LANES)
        for n in range(tm // SG_BLOCK):
            rows = slice(n * SG_BLOCK, (n + 1) * SG_BLOCK)
            sv = jnp.dot(wg, vn[rows, cols], preferred_element_type=F32) + bias
            ya_ref[rows, cols] = (uz[rows, cols] * sv).astype(BF16)

    cq = proj(B_CQ, MLA_Q_RANK)
    cqn = (cq * _rms(cq, MLA_Q_RANK) * cqg_ref[...]).astype(BF16)
    q = jnp.dot(cqn, wuq_ref[...], preferred_element_type=F32)
    ckv = proj(B_CKV, MLA_KV_RANK)
    ckvn = (ckv * _rms(ckv, MLA_KV_RANK) * ckvg_ref[...]).astype(BF16)
    kn = jnp.dot(ckvn, wuk_ref[...], preferred_element_type=F32)
    for s in range(tm // ATTN_TILE):
        vbt_ref[s] = lax.dot_general(wuvt_ref[...], ckvn[s * ATTN_TILE:(s + 1) * ATTN_TILE], NT_DIMS,
                                     preferred_element_type=F32).astype(BF16)
    kr_glr = proj(B_KR, LANES)
    lane = lax.broadcasted_iota(jnp.int32, (1, LANES), 1)
    kr = jnp.where(lane < MLA_QK, kr_glr, 0.0)
    cos = cos_ref[...]
    sin = sin_ref[...]

    def norm_rope(xh, gain, pad_row):
        xn = xh * _rms(xh, MLA_QK) * gain
        partner = pltpu.roll(xn, LANES // 2, 1)
        return (xn * cos + partner * sin + pad_row).astype(BF16)

    half = BRANCH_W // 2

    def gated(ref, src, part):
        def run():
            ref[:, part * half:(part + 1) * half] = _silu(proj(src + part * half, half)).astype(BF16)
        return run

    def plain(ref, src, part, scale=None):
        def run():
            r = proj(src + part * half, half)
            ref[:, part * half:(part + 1) * half] = (r if scale is None else r * scale).astype(BF16)
        return run

    def decay():
        gl = jnp.dot(kr_glr.astype(BF16), wgu_ref[...], preferred_element_type=F32) + bgu_ref[...]
        la_ref[...] = _log_sigmoid(gl) * (1.0 / GLA_TAU)

    pieces = [gated(zb_ref, B_Z, 0), gated(zb_ref, B_Z, 1), plain(qc_ref, C_Q, 0, GLA_DK ** -0.5),
              plain(kc_ref, C_K, 0), plain(vc_ref, C_V, 0), plain(vc_ref, C_V, 1), decay,
              gated(zc_ref, C_Z, 0), gated(zc_ref, C_Z, 1)]

    for hd in range(MLA_HEADS):
        cols = slice(hd * HEAD_PAD, (hd + 1) * HEAD_PAD)
        pieces[hd]()
        qb_ref[:, cols] = norm_rope(q[:, cols], qg_ref[...], qpad_ref[...])
        kb_ref[:, cols] = norm_rope(kn[:, cols] + kr, kg_ref[...], kpad_ref[...])
    for piece in pieces[MLA_HEADS:]:
        piece()


def _in_proj(x2, cos_t, sin_t, p):
    t = x2.shape[0]
    tm = TOKEN_TILE
    row = lambda w: pl.BlockSpec((tm, w), lambda i: (i, 0))
    consts = [p["norm_g"], p["w_mix"], p["sg_ln_g"], p["sg_ln_b"], p["sg_w"], p["sg_bt"],
              p["cq_g"], p["ckv_g"], p["w_uq"], p["w_uk"], p["w_uvt"], p["q_g"], p["k_g"],
              p["q_pad"], p["k_pad"], p["w_gu"], p["b_gu"]]
    out_widths = [(BRANCH_W, BF16), (MLA_HEADS * HEAD_PAD, BF16), (MLA_HEADS * HEAD_PAD, BF16),
                  None, (BRANCH_W, BF16),
                  (GLA_HEADS * GLA_DK, BF16), (GLA_HEADS * GLA_DK, BF16), (GLA_HEADS * GLA_DV, BF16),
                  (GLA_HEADS * GLA_DK, F32), (BRANCH_W, BF16)]
    per_step = tm // ATTN_TILE
    vt_spec = pl.BlockSpec((per_step, MLA_HEADS * MLA_V, ATTN_TILE), lambda i: (i, 0, 0))
    vt_shape = jax.ShapeDtypeStruct((t // ATTN_TILE, MLA_HEADS * MLA_V, ATTN_TILE), BF16)
    return pl.pallas_call(
        _in_proj_body,
        grid=(t // tm,),
        in_specs=[row(D_MODEL), row(LANES), row(LANES)] + [_const_spec(c.shape) for c in consts],
        out_specs=[vt_spec if o is None else row(o[0]) for o in out_widths],
        out_shape=[vt_shape if o is None else jax.ShapeDtypeStruct((t, o[0]), o[1]) for o in out_widths],
        compiler_params=pltpu.CompilerParams(dimension_semantics=("parallel",),
                                             vmem_limit_bytes=VMEM_LIMIT),
        name="in_proj",
    )(x2, cos_t, sin_t, *consts)


def _gla_body(q_ref, k_ref, v_ref, la_ref, z_ref, og_ref, y_ref, state_ref):
    @pl.when(pl.program_id(1) == 0)
    def _():
        state_ref[...] = jnp.zeros_like(state_ref)

    n_rows, tc = q_ref.shape[0], q_ref.shape[1]
    kw = GLA_HEADS * GLA_DK
    stack = GLA_HEADS * CHUNK
    ri = lax.broadcasted_iota(jnp.int32, (tc, tc), 0)
    ci = lax.broadcasted_iota(jnp.int32, (tc, tc), 1)
    tri = jnp.where((ri // CHUNK == ci // CHUNK) & (ri >= ci), 1.0, 0.0).astype(BF16)
    si = lax.broadcasted_iota(jnp.int32, (stack, CHUNK), 0) % CHUNK
    sj = lax.broadcasted_iota(jnp.int32, (stack, CHUNK), 1)
    causal = si >= sj
    lane = lax.broadcasted_iota(jnp.int32, (1, kw), 1)
    cums = []
    for r in range(n_rows):
        la = la_ref[r]
        la_hi = la.astype(BF16)
        la_lo = (la - la_hi.astype(F32)).astype(BF16)
        cums.append(jnp.dot(tri, la_hi, preferred_element_type=F32)
                    + jnp.dot(tri, la_lo, preferred_element_type=F32))
    for c in range(tc // CHUNK):
        rows = slice(c * CHUNK, (c + 1) * CHUNK)
        for r in range(n_rows):
            b = cums[r][rows]
            b_last = b[CHUNK - 1:CHUNK, :]
            qt = q_ref[r, rows, :].astype(F32) * jnp.exp(b)
            kf = k_ref[r, rows, :].astype(F32)
            kt = (kf * jnp.exp(-b)).astype(BF16)
            ks = (kf * jnp.exp(b_last - b)).astype(BF16)
            dec = jnp.exp(b_last)
            q_stack = jnp.concatenate(
                [jnp.where((lane >= hd * GLA_DK) & (lane < (hd + 1) * GLA_DK), qt, 0.0).astype(BF16)
                 for hd in range(GLA_HEADS)], axis=0)
            v = v_ref[r, rows, :]
            att = lax.dot_general(q_stack, kt, NT_DIMS, preferred_element_type=F32)
            att = jnp.where(causal, att, 0.0).astype(BF16)
            st = state_ref[r]
            o_intra = jnp.dot(att, v, preferred_element_type=F32)
            o_inter = lax.dot_general(q_stack, st.astype(BF16), NT_DIMS, preferred_element_type=F32)
            state_ref[r] = st * dec + lax.dot_general(v, ks, TN_DIMS, preferred_element_type=F32)
            for hd in range(GLA_HEADS):
                srows = slice(hd * CHUNK, (hd + 1) * CHUNK)
                cols = slice(hd * GLA_DV, (hd + 1) * GLA_DV)
                o = o_intra[srows, cols] + o_inter[srows, cols]
                on = o * _rms(o, GLA_DV) * og_ref[...]
                y_ref[r, rows, cols] = (on * z_ref[r, rows, cols].astype(F32)).astype(BF16)


def _gla(qc, kc, vc, la, zc, o_g, batch, seq):
    tc = GLA_TILE
    rps = ROWS_PER_STEP
    kw = GLA_HEADS * GLA_DK
    vw = GLA_HEADS * GLA_DV
    row = lambda w: pl.BlockSpec((rps, tc, w), lambda b, s: (b, s, 0))
    rows3 = lambda a: a.reshape(batch, seq, a.shape[-1])
    return pl.pallas_call(
        _gla_body,
        grid=(batch // rps, seq // tc),
        in_specs=[row(kw), row(kw), row(vw), row(kw), row(vw), _const_spec(o_g.shape)],
        out_specs=row(vw),
        out_shape=jax.ShapeDtypeStruct((batch, seq, vw), BF16),
        scratch_shapes=[pltpu.VMEM((rps, vw, kw), F32)],
        compiler_params=pltpu.CompilerParams(dimension_semantics=("parallel", "arbitrary"),
                                             vmem_limit_bytes=VMEM_LIMIT),
        name="gla",
    )(rows3(qc), rows3(kc), rows3(vc), rows3(la), rows3(zc), o_g).reshape(batch * seq, vw)


def _attn_body(fixed_ref, q_ref, k_ref, vt_ref, z_ref, y_ref, m_ref, l_ref, alpha_ref, acc_ref, s_ref,
               p_ref):
    tq = q_ref.shape[0]
    tk = vt_ref.shape[2]
    i = pl.program_id(1)
    ki = lax.broadcasted_iota(jnp.int32, (tk, tq), 0) // CHUNK
    qi = lax.broadcasted_iota(jnp.int32, (tk, tq), 1) // CHUNK
    visible = {"low": ki <= qi, "high": ki + tk // CHUNK <= qi}
    slabs = MLA_V // SUBLANES

    def all_sublanes(x, op):
        for shift in (4, 2, 1):
            x = op(x, pltpu.roll(x, shift, 0))
        return x

    def reduce_rows(st, op):
        parts = op(st.reshape(4, tk // SUBLANES // 4, SUBLANES, tq), axis=1)
        return op(parts, axis=0)

    def scores(hd, j, diagonal=None):
        start = pl.multiple_of(j * tk, tk)
        cols = slice(hd * HEAD_PAD, (hd + 1) * HEAD_PAD)
        st = lax.dot_general(k_ref[pl.ds(start, tk), cols], q_ref[:, cols], NT_DIMS,
                             preferred_element_type=F32)
        if diagonal is not None:
            st = jnp.where(visible[diagonal], st, -jnp.inf)
        s_ref[hd] = st

    def softmax_running_max(hd):
        st = s_ref[hd].reshape(tk // SUBLANES, SUBLANES, tq)
        m_prev = m_ref[hd]
        m_new = jnp.maximum(m_prev, all_sublanes(reduce_rows(st, jnp.max), jnp.maximum))
        alpha = jnp.exp2(m_prev - m_new)
        p = jnp.exp2(st - m_new[None])
        l_ref[hd] = alpha * l_ref[hd] + all_sublanes(reduce_rows(p, jnp.sum), jnp.add)
        p_ref[hd] = p.reshape(tk, tq).astype(BF16)
        alpha_ref[hd] = alpha
        m_ref[hd] = m_new

    def values_rescaled(hd, j):
        rows = slice(hd * MLA_V, (hd + 1) * MLA_V)
        pv = jnp.dot(vt_ref[j, rows, :], p_ref[hd], preferred_element_type=F32)
        acc = acc_ref[rows, :].reshape(slabs, SUBLANES, tq) * alpha_ref[hd][None]
        acc_ref[rows, :] = acc.reshape(MLA_V, tq) + pv

    def softmax_fixed_reference(hd):
        p = jnp.exp2(s_ref[hd].reshape(tk // SUBLANES, SUBLANES, tq))
        l_ref[hd] = l_ref[hd] + all_sublanes(reduce_rows(p, jnp.sum), jnp.add)
        p_ref[hd] = p.reshape(tk, tq).astype(BF16)

    def values_plain(hd, j):
        rows = slice(hd * MLA_V, (hd + 1) * MLA_V)
        acc_ref[rows, :] = acc_ref[rows, :] + jnp.dot(vt_ref[j, rows, :], p_ref[hd],
                                                      preferred_element_type=F32)

    def sweep(softmax, values):
        def round_(prev_block, next_block, diagonal=None):
            heads = range(MLA_HEADS)
            if prev_block is not None:
                for hd in heads:
                    values(hd, prev_block)
            for hd in heads:
                softmax(hd)
            if next_block is not None:
                for hd in heads:
                    scores(hd, next_block, diagonal)

        low, high = 2 * i, 2 * i + 1
        l_ref[...] = jnp.zeros(l_ref.shape, F32)
        acc_ref[...] = jnp.zeros(acc_ref.shape, F32)
        for hd in range(MLA_HEADS):
            scores(hd, low, "low")
        round_(None, high, "high")

        @pl.when(i == 0)
        def _():
            round_(low, None)

        @pl.when(i > 0)
        def _():
            round_(low, 0)

            def body(t, carry):
                round_(jnp.where(t == 2, high, t - 3), t - 1)
                return carry

            lax.fori_loop(2, high, body, 0)
            round_(low - 2, None)

        for hd in range(MLA_HEADS):
            values(hd, jnp.where(i == 0, high, low - 1))

    @pl.when(fixed_ref[0] != 0)
    def _():
        sweep(softmax_fixed_reference, values_plain)

    @pl.when(fixed_ref[0] == 0)
    def _():
        m_ref[...] = jnp.full(m_ref.shape, -jnp.inf, F32)
        sweep(softmax_running_max, values_rescaled)

    for hd in range(MLA_HEADS):
        rows = slice(hd * MLA_V, (hd + 1) * MLA_V)
        acc = acc_ref[rows, :].reshape(slabs, SUBLANES, tq) / l_ref[hd][None]
        acc_ref[rows, :] = acc.reshape(MLA_V, tq)
    y_ref[...] = (acc_ref[...].T * z_ref[...].astype(F32)).astype(BF16)


def _attn(fixed, qb, kb, vbt, zb, batch, seq):
    tq, tk = ATTN_Q_TILE, ATTN_TILE
    nq = seq // tq
    hw = MLA_HEADS * HEAD_PAD
    stat = pltpu.VMEM((MLA_HEADS, SUBLANES, tq), F32)
    return pl.pallas_call(
        _attn_body,
        grid=(batch, nq),
        in_specs=[pl.BlockSpec(memory_space=pltpu.SMEM),
                  pl.BlockSpec((tq, hw), lambda b, i: (b * nq + i, 0)),
                  pl.BlockSpec((seq, hw), lambda b, i: (b, 0), pipeline_mode=pl.Buffered(1)),
                  pl.BlockSpec((seq // tk, BRANCH_W, tk), lambda b, i: (b, 0, 0),
                               pipeline_mode=pl.Buffered(1)),
                  pl.BlockSpec((tq, BRANCH_W), lambda b, i: (b * nq + i, 0))],
        out_specs=pl.BlockSpec((tq, BRANCH_W), lambda b, i: (b * nq + i, 0)),
        out_shape=jax.ShapeDtypeStruct((batch * seq, BRANCH_W), BF16),
        scratch_shapes=[stat, stat, stat,
                        pltpu.VMEM((MLA_HEADS * MLA_V, tq), F32),
                        pltpu.VMEM((MLA_HEADS, tk, tq), F32),
                        pltpu.VMEM((MLA_HEADS, tk, tq), BF16)],
        compiler_params=pltpu.CompilerParams(dimension_semantics=("parallel", "arbitrary"),
                                             vmem_limit_bytes=VMEM_LIMIT),
        name="mla_attn",
    )(fixed, qb, kb, vbt, zb)


def _merge_body(x_ref, ya_ref, yb_ref, yc_ref, ng_ref, wgate_ref, bgate_ref, wbr_ref, wout_ref, o_ref):
    x = x_ref[...]
    h = (x * _rms(x, D_MODEL) * ng_ref[...]).astype(BF16)
    merged = None
    for n, y_ref in enumerate((ya_ref, yb_ref, yc_ref)):
        cols = slice(n * D_MODEL, (n + 1) * D_MODEL)
        logits = jnp.dot(h, wgate_ref[:, cols], preferred_element_type=F32) + bgate_ref[:, cols]
        term = _sigmoid(logits) * jnp.dot(y_ref[...], wbr_ref[n], preferred_element_type=F32)
        merged = term if merged is None else merged + term
    o_ref[...] = x + jnp.dot(merged.astype(BF16), wout_ref[...], preferred_element_type=F32)


def _merge(x2, ya, yb, yc, p):
    t = x2.shape[0]
    tm = TOKEN_TILE
    row = lambda w: pl.BlockSpec((tm, w), lambda i: (i, 0))
    consts = [p["norm_g"], p["w_gate"], p["b_gate"], p["w_branch"], p["w_out"]]
    return pl.pallas_call(
        _merge_body,
        grid=(t // tm,),
        in_specs=[row(D_MODEL), row(BRANCH_W), row(BRANCH_W), row(BRANCH_W)]
                 + [_const_spec(c.shape) for c in consts],
        out_specs=row(D_MODEL),
        out_shape=jax.ShapeDtypeStruct((t, D_MODEL), F32),
        compiler_params=pltpu.CompilerParams(dimension_semantics=("parallel",),
                                             vmem_limit_bytes=VMEM_LIMIT),
        name="merge",
    )(x2, ya, yb, yc, *consts)


def _head_layout(w, heads):
    lead = w.shape[:-1]
    w = w.reshape(lead + (heads, MLA_QK))
    w = jnp.pad(w, [(0, 0)] * (len(lead) + 1) + [(0, 1)])
    src = jnp.asarray([MLA_QK if s < 0 else s for s in HEAD_LANE_SOURCE], jnp.int32)
    return jnp.take(w, src, axis=-1).reshape(lead + (heads * HEAD_PAD,))


def _pack_layer(l, norm_g, w_in, b_gate, sg_ln_g, sg_ln_b, sg_w, sg_b, mla_cq_g, mla_ckv_g, mla_w_uq,
                mla_w_ukv, mla_q_g, mla_k_g, gla_w_gate, gla_b_gate, gla_o_g, w_branch, w_out):
    w = w_in[l]
    d = w.shape[0]
    zeros = lambda n: jnp.zeros((d, n), w.dtype)
    kr_block = _head_layout(jnp.concatenate([zeros(MLA_NOPE), w[:, 1920:1952]], axis=1), 1)
    kr_block = kr_block.at[:, GATE_LANE:GATE_LANE + GLA_GATE_RANK].set(w[:, 3488:3504])
    w_mix = jnp.concatenate([w[:, :1920], kr_block, w[:, 1952:3488], w[:, 3504:GATE_SRC]], axis=1)
    ukv = mla_w_ukv[l].reshape(MLA_KV_RANK, MLA_HEADS, MLA_NOPE + MLA_V)
    w_uk = _head_layout(jnp.pad(ukv[:, :, :MLA_NOPE], ((0, 0), (0, 0), (0, MLA_ROPE)))
                        .reshape(MLA_KV_RANK, -1), MLA_HEADS)
    w_uvt = ukv[:, :, MLA_NOPE:].reshape(MLA_KV_RANK, -1).T
    pad_gain = lambda g: _head_layout(g, 1).reshape(1, HEAD_PAD)
    w_gu = jnp.zeros((LANES, GLA_HEADS * GLA_DK), F32).at[GATE_LANE:GATE_LANE + GLA_GATE_RANK].set(
        gla_w_gate[l])
    q_g = pad_gain(mla_q_g[l]) * (MLA_QK ** -0.5 * math.log2(math.e))
    k_g = pad_gain(mla_k_g[l])
    bound = MLA_QK * jnp.max(jnp.abs(q_g)) * jnp.max(jnp.abs(k_g)) * SCORE_BOUND_MARGIN
    fixed = bound <= FIXED_REFERENCE_MAX_BOUND
    bias_lane = (jnp.arange(HEAD_PAD) == BIAS_LANE).reshape(1, HEAD_PAD)
    return {
        "norm_g": norm_g[l].reshape(1, -1),
        "w_mix": w_mix.astype(BF16),
        "w_gate": w[:, GATE_SRC:].astype(BF16),
        "sg_ln_g": sg_ln_g[l].reshape(1, -1),
        "sg_ln_b": sg_ln_b[l].reshape(1, -1),
        "sg_w": sg_w[l],
        "sg_bt": sg_b[l].T,
        "cq_g": mla_cq_g[l].reshape(1, -1),
        "ckv_g": mla_ckv_g[l].reshape(1, -1),
        "w_uq": _head_layout(mla_w_uq[l], MLA_HEADS).astype(BF16),
        "w_uk": w_uk.astype(BF16),
        "w_uvt": w_uvt.astype(BF16),
        "q_g": q_g,
        "k_g": k_g,
        "q_pad": jnp.where(bias_lane & fixed, -bound, 0.0).astype(F32),
        "k_pad": jnp.where(bias_lane, 1.0, 0.0).astype(F32),
        "fixed": fixed.astype(jnp.int32).reshape(1),
        "w_gu": w_gu.astype(BF16),
        "b_gu": gla_b_gate[l].reshape(1, -1),
        "b_gate": b_gate[l].reshape(1, -1),
        "o_g": gla_o_g[l].reshape(1, -1),
        "w_branch": w_branch[l].astype(BF16),
        "w_out": w_out[l].astype(BF16),
    }


def kernel(x, positions, norm_g, w_in, b_gate, sg_ln_g, sg_ln_b, sg_w, sg_b, mla_cq_g, mla_ckv_g, mla_w_uq, mla_w_ukv, mla_q_g, mla_k_g, gla_w_gate, gla_b_gate, gla_o_g, w_branch, w_out):
    batch, seq, d = x.shape
    assert d == D_MODEL and seq % max(ATTN_Q_TILE, GLA_TILE) == 0 and (batch * seq) % TOKEN_TILE == 0
    assert batch % ROWS_PER_STEP == 0
    assert TOKEN_TILE % ATTN_TILE == 0 and TOKEN_TILE % SG_BLOCK == 0
    depth = w_in.shape[0]
    cos_t, sin_t = _rope_tables(positions)
    x2 = x.reshape(batch * seq, d)
    for l in range(depth):
        p = _pack_layer(l, norm_g, w_in, b_gate, sg_ln_g, sg_ln_b, sg_w, sg_b, mla_cq_g, mla_ckv_g,
                        mla_w_uq, mla_w_ukv, mla_q_g, mla_k_g, gla_w_gate, gla_b_gate, gla_o_g,
                        w_branch, w_out)
        ya, qb, kb, vbt, zb, qc, kc, vc, la, zc = _in_proj(x2, cos_t, sin_t, p)
        yc = _gla(qc, kc, vc, la, zc, p["o_g"], batch, seq)
        yb = _attn(p["fixed"], qb, kb, vbt, zb, batch, seq)
        x2 = _merge(x2, ya, yb, yc, p)
    return x2.reshape(batch, seq, d)
```

```python
import math

import jax
import jax.numpy as jnp
from jax import lax
from jax.experimental import pallas as pl
from jax.experimental.pallas import tpu as pltpu

F32 = jnp.float32
BF16 = jnp.bfloat16

D_MODEL = 1024
CHUNK = 64
BRANCH_W = 512
N_BRANCH = 3
EPS = 1e-6
SG_BLOCK = 128
SG_GROUPS = 4
MLA_HEADS = 8
MLA_NOPE = 64
MLA_ROPE = 32
MLA_QK = MLA_NOPE + MLA_ROPE
MLA_V = 64
MLA_Q_RANK = 256
MLA_KV_RANK = 128
ROPE_THETA = 10000.0
GLA_HEADS = 4
GLA_DK = 64
GLA_DV = 128
GLA_GATE_RANK = 16
GLA_TAU = 16.0

LANES = 128
SUBLANES = 8
HEAD_PAD = LANES
ROPE_HALF = MLA_ROPE // 2
X1_LO = 0
X2_LO = LANES // 2
BIAS_LANE = MLA_QK
GATE_LANE = BIAS_LANE + 1


def _head_lane_source():
    src = [-1] * LANES
    nope_lanes = list(range(ROPE_HALF, X2_LO)) + list(range(X2_LO + ROPE_HALF, MLA_QK))
    for f, lane in enumerate(nope_lanes):
        src[lane] = f
    for f in range(ROPE_HALF):
        src[X1_LO + f] = MLA_NOPE + f
        src[X2_LO + f] = MLA_NOPE + ROPE_HALF + f
    return src


HEAD_LANE_SOURCE = _head_lane_source()

A_U, A_V, A_Z = 0, 512, 1024
B_CQ, B_CKV, B_KR, B_Z = 1536, 1792, 1920, 2048
C_Q, C_K, C_V, C_Z = 2560, 2816, 3072, 3584
MIX_COLS_PAD = 4096
GATE_SRC = 4016

IN_TILE = 1024
MERGE_TILE = 512
ATTN_TILE = 256
ATTN_Q_TILE = 2 * ATTN_TILE
GLA_TILE = 256
ROWS_PER_STEP = 2
VMEM_LIMIT = 56 * 1024 * 1024

SCORE_BOUND_MARGIN = 1.02
FIXED_REFERENCE_MAX_BOUND = 40.0

NT_DIMS = (((1,), (1,)), ((), ()))
TN_DIMS = (((0,), (0,)), ((), ()))


def _const_spec(shape):
    zeros = (0,) * len(shape)
    return pl.BlockSpec(shape, lambda *_: zeros, pipeline_mode=pl.Buffered(1))


def _sigmoid(x):
    return 0.5 * (jnp.tanh(0.5 * x) + 1.0)


def _silu(x):
    return x * _sigmoid(x)


def _gelu_tanh(x):
    c = math.sqrt(2.0 / math.pi)
    return 0.5 * x * (1.0 + jnp.tanh(c * (x + 0.044715 * (x * x * x))))


def _log_sigmoid(x):
    return jnp.minimum(x, 0.0) - jnp.log(1.0 + jnp.exp(-jnp.abs(x)))


def _rms(x, width):
    return lax.rsqrt(jnp.sum(x * x, axis=-1, keepdims=True) * (1.0 / width) + EPS)


def _rope_body(pos_ref, cos_ref, sin_ref):
    lane = lax.broadcasted_iota(jnp.int32, (1, LANES), 1)
    in_x1 = (lane >= X1_LO) & (lane < X1_LO + ROPE_HALF)
    in_x2 = (lane >= X2_LO) & (lane < X2_LO + ROPE_HALF)
    fidx = jnp.where(in_x1, lane - X1_LO, lane - X2_LO).astype(F32)
    inv = 1.0 / jnp.exp(fidx * (2.0 / MLA_ROPE) * math.log(ROPE_THETA))
    ang = pos_ref[...].astype(F32) * inv
    cos = jnp.cos(ang)
    sin = jnp.sin(ang)
    rope = in_x1 | in_x2
    cos_ref[...] = jnp.where(rope, cos, jnp.where(lane < MLA_QK, 1.0, 0.0))
    sin_ref[...] = jnp.where(in_x1, -sin, jnp.where(in_x2, sin, 0.0))


def _rope_tables(positions):
    t = positions.size
    tm = 1024 if t % 1024 == 0 else ATTN_TILE
    return pl.pallas_call(
        _rope_body,
        grid=(t // tm,),
        in_specs=[pl.BlockSpec((tm, 1), lambda i: (i, 0))],
        out_specs=[pl.BlockSpec((tm, LANES), lambda i: (i, 0))] * 2,
        out_shape=[jax.ShapeDtypeStruct((t, LANES), F32)] * 2,
        name="rope_tables",
    )(positions.reshape(t, 1))


def _in_proj_body(x_ref, cos_ref, sin_ref, ng_ref, win_ref, lng_ref, lnb_ref, sgw_ref, sgbt_ref,
                  cqg_ref, ckvg_ref, wuq_ref, wuk_ref, wuvt_ref, qg_ref, kg_ref, qpad_ref, kpad_ref,
                  wgu_ref, bgu_ref,
                  ya_ref, qb_ref, kb_ref, vbt_ref, zb_ref, qc_ref, kc_ref, vc_ref, la_ref, zc_ref):
    tm = x_ref.shape[0]
    x = x_ref[...]
    h = (x * _rms(x, D_MODEL) * ng_ref[...]).astype(BF16)

    def proj(lo, width):
        return jnp.dot(h, win_ref[:, lo:lo + width], preferred_element_type=F32)

    u = _gelu_tanh(proj(A_U, BRANCH_W))
    v = _gelu_tanh(proj(A_V, BRANCH_W))
    mu = jnp.mean(v, axis=-1, keepdims=True)
    vc = v - mu
    vn = vc * lax.rsqrt(jnp.mean(vc * vc, axis=-1, keepdims=True) + EPS)
    vn = (vn * lng_ref[...] + lnb_ref[...]).astype(BF16)
    uz = u * _silu(proj(A_Z, BRANCH_W))
    ri = lax.broadcasted_iota(jnp.int32, (SG_BLOCK, SG_BLOCK), 0) // CHUNK
    ci = lax.broadcasted_iota(jnp.int32, (SG_BLOCK, SG_BLOCK), 1) // CHUNK
    chunk_causal = ri >= ci
    for g in range(SG_GROUPS):
        wg = jnp.where(chunk_causal, sgw_ref[g], 0.0).astype(BF16)
        bias = sgbt_ref[:, g:g + 1]
        cols = slice(g * LANES, (g + 1) * LANES)
        for n in range(tm // SG_BLOCK):
            rows = slice(n * SG_BLOCK, (n + 1) * SG_BLOCK)
            sv = jnp.dot(wg, vn[rows, cols], preferred_element_type=F32) + bias
            ya_ref[rows, cols] = (uz[rows, cols] * sv).astype(BF16)

    cq = proj(B_CQ, MLA_Q_RANK)
    cqn = (cq * _rms(cq, MLA_Q_RANK) * cqg_ref[...]).astype(BF16)
    q = jnp.dot(cqn, wuq_ref[...], preferred_element_type=F32)
    ckv = proj(B_CKV, MLA_KV_RANK)
    ckvn = (ckv * _rms(ckv, MLA_KV_RANK) * ckvg_ref[...]).astype(BF16)
    kn = jnp.dot(ckvn, wuk_ref[...], preferred_element_type=F32)
    for s in range(tm // ATTN_TILE):
        vbt_ref[s] = lax.dot_general(wuvt_ref[...], ckvn[s * ATTN_TILE:(s + 1) * ATTN_TILE], NT_DIMS,
                                     preferred_element_type=F32).astype(BF16)
    kr_glr = proj(B_KR, LANES)
    lane = lax.broadcasted_iota(jnp.int32, (1, LANES), 1)
    kr = jnp.where(lane < MLA_QK, kr_glr, 0.0)
    cos = cos_ref[...]
    sin = sin_ref[...]

    def norm_rope(xh, gain, pad_row):
        xn = xh * _rms(xh, MLA_QK) * gain
        partner = pltpu.roll(xn, LANES // 2, 1)
        return (xn * cos + partner * sin + pad_row).astype(BF16)

    half = BRANCH_W // 2

    def gated(ref, src, part):
        def run():
            ref[:, part * half:(part + 1) * half] = _silu(proj(src + part * half, half)).astype(BF16)
        return run

    def plain(ref, src, part, scale=None):
        def run():
            r = proj(src + part * half, half)
            ref[:, part * half:(part + 1) * half] = (r if scale is None else r * scale).astype(BF16)
        return run

    def decay():
        gl = jnp.dot(kr_glr.astype(BF16), wgu_ref[...], preferred_element_type=F32) + bgu_ref[...]
        la_ref[...] = _log_sigmoid(gl) * (1.0 / GLA_TAU)

    pieces = [gated(zb_ref, B_Z, 0), gated(zb_ref, B_Z, 1), plain(qc_ref, C_Q, 0, GLA_DK ** -0.5),
              plain(kc_ref, C_K, 0), plain(vc_ref, C_V, 0), plain(vc_ref, C_V, 1), decay,
              gated(zc_ref, C_Z, 0), gated(zc_ref, C_Z, 1)]

    for hd in range(MLA_HEADS):
        cols = slice(hd * HEAD_PAD, (hd + 1) * HEAD_PAD)
        pieces[hd]()
        qb_ref[:, cols] = norm_rope(q[:, cols], qg_ref[...], qpad_ref[...])
        kb_ref[:, cols] = norm_rope(kn[:, cols] + kr, kg_ref[...], kpad_ref[...])
    for piece in pieces[MLA_HEADS:]:
        piece()


def _in_proj(x2, cos_t, sin_t, p):
    t = x2.shape[0]
    tm = IN_TILE
    row = lambda w: pl.BlockSpec((tm, w), lambda i: (i, 0))
    consts = [p["norm_g"], p["w_mix"], p["sg_ln_g"], p["sg_ln_b"], p["sg_w"], p["sg_bt"],
              p["cq_g"], p["ckv_g"], p["w_uq"], p["w_uk"], p["w_uvt"], p["q_g"], p["k_g"],
              p["q_pad"], p["k_pad"], p["w_gu"], p["b_gu"]]
    out_widths = [(BRANCH_W, BF16), (MLA_HEADS * HEAD_PAD, BF16), (MLA_HEADS * HEAD_PAD, BF16),
                  None, (BRANCH_W, BF16),
                  (GLA_HEADS * GLA_DK, BF16), (GLA_HEADS * GLA_DK, BF16), (GLA_HEADS * GLA_DV, BF16),
                  (GLA_HEADS * GLA_DK, F32), (BRANCH_W, BF16)]
    per_step = tm // ATTN_TILE
    vt_spec = pl.BlockSpec((per_step, MLA_HEADS * MLA_V, ATTN_TILE), lambda i: (i, 0, 0))
    vt_shape = jax.ShapeDtypeStruct((t // ATTN_TILE, MLA_HEADS * MLA_V, ATTN_TILE), BF16)
    return pl.pallas_call(
        _in_proj_body,
        grid=(t // tm,),
        in_specs=[row(D_MODEL), row(LANES), row(LANES)] + [_const_spec(c.shape) for c in consts],
        out_specs=[vt_spec if o is None else row(o[0]) for o in out_widths],
        out_shape=[vt_shape if o is None else jax.ShapeDtypeStruct((t, o[0]), o[1]) for o in out_widths],
        compiler_params=pltpu.CompilerParams(dimension_semantics=("parallel",),
                                             vmem_limit_bytes=VMEM_LIMIT),
        name="in_proj",
    )(x2, cos_t, sin_t, *consts)


def _gla_body(q_ref, k_ref, v_ref, la_ref, z_ref, og_ref, y_ref, state_ref):
    @pl.when(pl.program_id(1) == 0)
    def _():
        state_ref[...] = jnp.zeros_like(state_ref)

    n_rows, tc = q_ref.shape[0], q_ref.shape[1]
    kw = GLA_HEADS * GLA_DK
    stack = GLA_HEADS * CHUNK
    ri = lax.broadcasted_iota(jnp.int32, (tc, tc), 0)
    ci = lax.broadcasted_iota(jnp.int32, (tc, tc), 1)
    tri = jnp.where((ri // CHUNK == ci // CHUNK) & (ri >= ci), 1.0, 0.0).astype(BF16)
    si = lax.broadcasted_iota(jnp.int32, (stack, CHUNK), 0) % CHUNK
    sj = lax.broadcasted_iota(jnp.int32, (stack, CHUNK), 1)
    causal = si >= sj
    lane = lax.broadcasted_iota(jnp.int32, (1, kw), 1)
    cums = []
    for r in range(n_rows):
        la = la_ref[r]
        la_hi = la.astype(BF16)
        la_lo = (la - la_hi.astype(F32)).astype(BF16)
        cums.append(jnp.dot(tri, la_hi, preferred_element_type=F32)
                    + jnp.dot(tri, la_lo, preferred_element_type=F32))
    for c in range(tc // CHUNK):
        rows = slice(c * CHUNK, (c + 1) * CHUNK)
        for r in range(n_rows):
            b = cums[r][rows]
            b_last = b[CHUNK - 1:CHUNK, :]
            qt = q_ref[r, rows, :].astype(F32) * jnp.exp(b)
            kf = k_ref[r, rows, :].astype(F32)
            kt = (kf * jnp.exp(-b)).astype(BF16)
            ks = (kf * jnp.exp(b_last - b)).astype(BF16)
            dec = jnp.exp(b_last)
            q_stack = jnp.concatenate(
                [jnp.where((lane >= hd * GLA_DK) & (lane < (hd + 1) * GLA_DK), qt, 0.0).astype(BF16)
                 for hd in range(GLA_HEADS)], axis=0)
            v = v_ref[r, rows, :]
            att = lax.dot_general(q_stack, kt, NT_DIMS, preferred_element_type=F32)
            att = jnp.where(causal, att, 0.0).astype(BF16)
            st = state_ref[r]
            o_intra = jnp.dot(att, v, preferred_element_type=F32)
            o_inter = lax.dot_general(q_stack, st.astype(BF16), NT_DIMS, preferred_element_type=F32)
            state_ref[r] = st * dec + lax.dot_general(v, ks, TN_DIMS, preferred_element_type=F32)
            for hd in range(GLA_HEADS):
                srows = slice(hd * CHUNK, (hd + 1) * CHUNK)
                cols = slice(hd * GLA_DV, (hd + 1) * GLA_DV)
                o = o_intra[srows, cols] + o_inter[srows, cols]
                on = o * _rms(o, GLA_DV) * og_ref[...]
                y_ref[r, rows, cols] = (on * z_ref[r, rows, cols].astype(F32)).astype(BF16)


def _gla(qc, kc, vc, la, zc, o_g, batch, seq):
    tc = GLA_TILE
    rps = ROWS_PER_STEP
    kw = GLA_HEADS * GLA_DK
    vw = GLA_HEADS * GLA_DV
    row = lambda w: pl.BlockSpec((rps, tc, w), lambda b, s: (b, s, 0))
    rows3 = lambda a: a.reshape(batch, seq, a.shape[-1])
    return pl.pallas_call(
        _gla_body,
        grid=(batch // rps, seq // tc),
        in_specs=[row(kw), row(kw), row(vw), row(kw), row(vw), _const_spec(o_g.shape)],
        out_specs=row(vw),
        out_shape=jax.ShapeDtypeStruct((batch, seq, vw), BF16),
        scratch_shapes=[pltpu.VMEM((rps, vw, kw), F32)],
        compiler_params=pltpu.CompilerParams(dimension_semantics=("parallel", "arbitrary"),
                                             vmem_limit_bytes=VMEM_LIMIT),
        name="gla",
    )(rows3(qc), rows3(kc), rows3(vc), rows3(la), rows3(zc), o_g).reshape(batch * seq, vw)


def _attn_body(fixed_ref, q_ref, k_ref, vt_ref, z_ref, y_ref, m_ref, l_ref, alpha_ref, acc_ref, s_ref,
               p_ref):
    tq = q_ref.shape[0]
    tk = vt_ref.shape[2]
    i = pl.program_id(1)
    ki = lax.broadcasted_iota(jnp.int32, (tk, tq), 0) // CHUNK
    qi = lax.broadcasted_iota(jnp.int32, (tk, tq), 1) // CHUNK
    visible = {"low": ki <= qi, "high": ki + tk // CHUNK <= qi}
    slabs = MLA_V // SUBLANES

    def all_sublanes(x, op):
        for shift in (4, 2, 1):
            x = op(x, pltpu.roll(x, shift, 0))
        return x

    def reduce_rows(st, op):
        parts = op(st.reshape(4, tk // SUBLANES // 4, SUBLANES, tq), axis=1)
        return op(parts, axis=0)

    def block_scores(hd, j, diagonal):
        start = pl.multiple_of(j * tk, tk)
        cols = slice(hd * HEAD_PAD, (hd + 1) * HEAD_PAD)
        st = lax.dot_general(k_ref[pl.ds(start, tk), cols], q_ref[:, cols], NT_DIMS,
                             preferred_element_type=F32)
        if diagonal is not None:
            st = jnp.where(visible[diagonal], st, -jnp.inf)
        return st

    heads = range(MLA_HEADS)
    low, high = 2 * i, 2 * i + 1
    l_ref[...] = jnp.zeros(l_ref.shape, F32)
    acc_ref[...] = jnp.zeros(acc_ref.shape, F32)


    def scores_exp(hd, j, slot, diagonal=None):
        p = jnp.exp2(block_scores(hd, j, diagonal).reshape(tk // SUBLANES, SUBLANES, tq))
        l_ref[hd] = l_ref[hd] + all_sublanes(reduce_rows(p, jnp.sum), jnp.add)
        p_ref[slot, hd] = p.reshape(tk, tq).astype(BF16)

    def values_plain(hd, j, slot):
        rows = slice(hd * MLA_V, (hd + 1) * MLA_V)
        acc_ref[rows, :] = acc_ref[rows, :] + jnp.dot(vt_ref[j, rows, :], p_ref[slot, hd],
                                                      preferred_element_type=F32)

    @pl.when(fixed_ref[0] != 0)
    def _():
        for hd in heads:
            scores_exp(hd, low, 0, "low")
        for hd in heads:
            values_plain(hd, low, 0)
        for hd in heads:
            scores_exp(hd, high, 1, "high")

        def pair(u, carry):
            for hd in heads:
                values_plain(hd, jnp.where(u == 0, high, 2 * u - 1), 1)
            for hd in heads:
                scores_exp(hd, 2 * u, 0)
            for hd in heads:
                values_plain(hd, 2 * u, 0)
            for hd in heads:
                scores_exp(hd, 2 * u + 1, 1)
            return carry

        lax.fori_loop(0, i, pair, 0)
        for hd in heads:
            values_plain(hd, jnp.where(i == 0, high, low - 1), 1)

    def softmax_running_max(hd):
        st = s_ref[hd].reshape(tk // SUBLANES, SUBLANES, tq)
        m_prev = m_ref[hd]
        m_new = jnp.maximum(m_prev, all_sublanes(reduce_rows(st, jnp.max), jnp.maximum))
        alpha = jnp.exp2(m_prev - m_new)
        p = jnp.exp2(st - m_new[None])
        l_ref[hd] = alpha * l_ref[hd] + all_sublanes(reduce_rows(p, jnp.sum), jnp.add)
        p_ref[0, hd] = p.reshape(tk, tq).astype(BF16)
        alpha_ref[hd] = alpha
        m_ref[hd] = m_new

    def values_rescaled(hd, j):
        rows = slice(hd * MLA_V, (hd + 1) * MLA_V)
        pv = jnp.dot(vt_ref[j, rows, :], p_ref[0, hd], preferred_element_type=F32)
        acc = acc_ref[rows, :].reshape(slabs, SUBLANES, tq) * alpha_ref[hd][None]
        acc_ref[rows, :] = acc.reshape(MLA_V, tq) + pv

    def round_(prev_block, next_block, diagonal=None):
        if prev_block is not None:
            for hd in heads:
                values_rescaled(hd, prev_block)
        for hd in heads:
            softmax_running_max(hd)
        if next_block is not None:
            for hd in heads:
                s_ref[hd] = block_scores(hd, next_block, diagonal)

    @pl.when(fixed_ref[0] == 0)
    def _():
        m_ref[...] = jnp.full(m_ref.shape, -jnp.inf, F32)
        for hd in heads:
            s_ref[hd] = block_scores(hd, low, "low")
        round_(None, high, "high")

        @pl.when(i == 0)
        def _():
            round_(low, None)

        @pl.when(i > 0)
        def _():
            round_(low, 0)

            def body(t, carry):
                round_(jnp.where(t == 2, high, t - 3), t - 1)
                return carry

            lax.fori_loop(2, high, body, 0)
            round_(low - 2, None)

        for hd in heads:
            values_rescaled(hd, jnp.where(i == 0, high, low - 1))

    for hd in range(MLA_HEADS):
        rows = slice(hd * MLA_V, (hd + 1) * MLA_V)
        acc = acc_ref[rows, :].reshape(slabs, SUBLANES, tq) / l_ref[hd][None]
        acc_ref[rows, :] = acc.reshape(MLA_V, tq)
    y_ref[...] = (acc_ref[...].T * z_ref[...].astype(F32)).astype(BF16)


def _attn(fixed, qb, kb, vbt, zb, batch, seq):
    tq, tk = ATTN_Q_TILE, ATTN_TILE
    nq = seq // tq
    hw = MLA_HEADS * HEAD_PAD
    stat = pltpu.VMEM((MLA_HEADS, SUBLANES, tq), F32)
    return pl.pallas_call(
        _attn_body,
        grid=(batch, nq),
        in_specs=[pl.BlockSpec(memory_space=pltpu.SMEM),
                  pl.BlockSpec((tq, hw), lambda b, i: (b * nq + i, 0)),
                  pl.BlockSpec((seq, hw), lambda b, i: (b, 0), pipeline_mode=pl.Buffered(1)),
                  pl.BlockSpec((seq // tk, BRANCH_W, tk), lambda b, i: (b, 0, 0),
                               pipeline_mode=pl.Buffered(1)),
                  pl.BlockSpec((tq, BRANCH_W), lambda b, i: (b * nq + i, 0))],
        out_specs=pl.BlockSpec((tq, BRANCH_W), lambda b, i: (b * nq + i, 0)),
        out_shape=jax.ShapeDtypeStruct((batch * seq, BRANCH_W), BF16),
        scratch_shapes=[stat, stat, stat,
                        pltpu.VMEM((MLA_HEADS * MLA_V, tq), F32),
                        pltpu.VMEM((MLA_HEADS, tk, tq), F32),
                        pltpu.VMEM((2, MLA_HEADS, tk, tq), BF16)],
        compiler_params=pltpu.CompilerParams(dimension_semantics=("parallel", "arbitrary"),
                                             vmem_limit_bytes=VMEM_LIMIT),
        name="mla_attn",
    )(fixed, qb, kb, vbt, zb)


def _merge_body(x_ref, ya_ref, yb_ref, yc_ref, ng_ref, wgate_ref, bgate_ref, wbr_ref, wout_ref, o_ref):
    x = x_ref[...]
    h = (x * _rms(x, D_MODEL) * ng_ref[...]).astype(BF16)
    merged = None
    for n, y_ref in enumerate((ya_ref, yb_ref, yc_ref)):
        cols = slice(n * D_MODEL, (n + 1) * D_MODEL)
        logits = jnp.dot(h, wgate_ref[:, cols], preferred_element_type=F32) + bgate_ref[:, cols]
        term = _sigmoid(logits) * jnp.dot(y_ref[...], wbr_ref[n], preferred_element_type=F32)
        merged = term if merged is None else merged + term
    o_ref[...] = x + jnp.dot(merged.astype(BF16), wout_ref[...], preferred_element_type=F32)


def _merge(x2, ya, yb, yc, p):
    t = x2.shape[0]
    tm = MERGE_TILE
    row = lambda w: pl.BlockSpec((tm, w), lambda i: (i, 0))
    consts = [p["norm_g"], p["w_gate"], p["b_gate"], p["w_branch"], p["w_out"]]
    return pl.pallas_call(
        _merge_body,
        grid=(t // tm,),
        in_specs=[row(D_MODEL), row(BRANCH_W), row(BRANCH_W), row(BRANCH_W)]
                 + [_const_spec(c.shape) for c in consts],
        out_specs=row(D_MODEL),
        out_shape=jax.ShapeDtypeStruct((t, D_MODEL), F32),
        compiler_params=pltpu.CompilerParams(dimension_semantics=("parallel",),
                                             vmem_limit_bytes=VMEM_LIMIT),
        name="merge",
    )(x2, ya, yb, yc, *consts)


def _head_layout(w, heads):
    lead = w.shape[:-1]
    w = w.reshape(lead + (heads, MLA_QK))
    w = jnp.pad(w, [(0, 0)] * (len(lead) + 1) + [(0, 1)])
    src = jnp.asarray([MLA_QK if s < 0 else s for s in HEAD_LANE_SOURCE], jnp.int32)
    return jnp.take(w, src, axis=-1).reshape(lead + (heads * HEAD_PAD,))


def _pack_layer(l, norm_g, w_in, b_gate, sg_ln_g, sg_ln_b, sg_w, sg_b, mla_cq_g, mla_ckv_g, mla_w_uq,
                mla_w_ukv, mla_q_g, mla_k_g, gla_w_gate, gla_b_gate, gla_o_g, w_branch, w_out):
    w = w_in[l]
    d = w.shape[0]
    zeros = lambda n: jnp.zeros((d, n), w.dtype)
    kr_block = _head_layout(jnp.concatenate([zeros(MLA_NOPE), w[:, 1920:1952]], axis=1), 1)
    kr_block = kr_block.at[:, GATE_LANE:GATE_LANE + GLA_GATE_RANK].set(w[:, 3488:3504])
    w_mix = jnp.concatenate([w[:, :1920], kr_block, w[:, 1952:3488], w[:, 3504:GATE_SRC]], axis=1)
    ukv = mla_w_ukv[l].reshape(MLA_KV_RANK, MLA_HEADS, MLA_NOPE + MLA_V)
    w_uk = _head_layout(jnp.pad(ukv[:, :, :MLA_NOPE], ((0, 0), (0, 0), (0, MLA_ROPE)))
                        .reshape(MLA_KV_RANK, -1), MLA_HEADS)
    w_uvt = ukv[:, :, MLA_NOPE:].reshape(MLA_KV_RANK, -1).T
    pad_gain = lambda g: _head_layout(g, 1).reshape(1, HEAD_PAD)
    w_gu = jnp.zeros((LANES, GLA_HEADS * GLA_DK), F32).at[GATE_LANE:GATE_LANE + GLA_GATE_RANK].set(
        gla_w_gate[l])
    q_g = pad_gain(mla_q_g[l]) * (MLA_QK ** -0.5 * math.log2(math.e))
    k_g = pad_gain(mla_k_g[l])
    bound = MLA_QK * jnp.max(jnp.abs(q_g)) * jnp.max(jnp.abs(k_g)) * SCORE_BOUND_MARGIN
    fixed = bound <= FIXED_REFERENCE_MAX_BOUND
    bias_lane = (jnp.arange(HEAD_PAD) == BIAS_LANE).reshape(1, HEAD_PAD)
    return {
        "norm_g": norm_g[l].reshape(1, -1),
        "w_mix": w_mix.astype(BF16),
        "w_gate": w[:, GATE_SRC:].astype(BF16),
        "sg_ln_g": sg_ln_g[l].reshape(1, -1),
        "sg_ln_b": sg_ln_b[l].reshape(1, -1),
        "sg_w": sg_w[l],
        "sg_bt": sg_b[l].T,
        "cq_g": mla_cq_g[l].reshape(1, -1),
        "ckv_g": mla_ckv_g[l].reshape(1, -1),
        "w_uq": _head_layout(mla_w_uq[l], MLA_HEADS).astype(BF16),
        "w_uk": w_uk.astype(BF16),
        "w_uvt": w_uvt.astype(BF16),
        "q_g": q_g,
        "k_g": k_g,
        "q_pad": jnp.where(bias_lane & fixed, -bound, 0.0).astype(F32),
        "k_pad": jnp.where(bias_lane, 1.0, 0.0).astype(F32),
        "fixed": fixed.astype(jnp.int32).reshape(1),
        "w_gu": w_gu.astype(BF16),
        "b_gu": gla_b_gate[l].reshape(1, -1),
        "b_gate": b_gate[l].reshape(1, -1),
        "o_g": gla_o_g[l].reshape(1, -1),
        "w_branch": w_branch[l].astype(BF16),
        "w_out": w_out[l].astype(BF16),
    }


def kernel(x, positions, norm_g, w_in, b_gate, sg_ln_g, sg_ln_b, sg_w, sg_b, mla_cq_g, mla_ckv_g, mla_w_uq, mla_w_ukv, mla_q_g, mla_k_g, gla_w_gate, gla_b_gate, gla_o_g, w_branch, w_out):
    batch, seq, d = x.shape
    assert d == D_MODEL and seq % max(ATTN_Q_TILE, GLA_TILE) == 0
    assert (batch * seq) % IN_TILE == 0 and (batch * seq) % MERGE_TILE == 0
    assert batch % ROWS_PER_STEP == 0
    assert IN_TILE % ATTN_TILE == 0 and IN_TILE % SG_BLOCK == 0
    depth = w_in.shape[0]
    cos_t, sin_t = _rope_tables(positions)
    x2 = x.reshape(batch * seq, d)
    for l in range(depth):
        p = _pack_layer(l, norm_g, w_in, b_gate, sg_ln_g, sg_ln_b, sg_w, sg_b, mla_cq_g, mla_ckv_g,
                        mla_w_uq, mla_w_ukv, mla_q_g, mla_k_g, gla_w_gate, gla_b_gate, gla_o_g,
                        w_branch, w_out)
        ya, qb, kb, vbt, zb, qc, kc, vc, la, zc = _in_proj(x2, cos_t, sin_t, p)
        yc = _gla(qc, kc, vc, la, zc, p["o_g"], batch, seq)
        yb = _attn(p["fixed"], qb, kb, vbt, zb, batch, seq)
        x2 = _merge(x2, ya, yb, yc, p)
    return x2.reshape(batch, seq, d)
```

```python
import functools
import math

import jax
import jax.numpy as jnp
from jax import lax
from jax.experimental import pallas as pl
from jax.experimental.pallas import tpu as pltpu

F32 = jnp.float32
BF16 = jnp.bfloat16

D_MODEL = 1024
CHUNK = 64
BRANCH_W = 512
N_BRANCH = 3
EPS = 1e-6
SG_BLOCK = 128
SG_GROUPS = 4
MLA_HEADS = 8
MLA_NOPE = 64
MLA_ROPE = 32
MLA_QK = MLA_NOPE + MLA_ROPE
MLA_V = 64
MLA_Q_RANK = 256
MLA_KV_RANK = 128
ROPE_THETA = 10000.0
GLA_HEADS = 4
GLA_DK = 64
GLA_DV = 128
GLA_GATE_RANK = 16
GLA_TAU = 16.0

LANES = 128
SUBLANES = 8
HEAD_PAD = LANES
ROPE_HALF = MLA_ROPE // 2
X1_LO = 0
X2_LO = LANES // 2
BIAS_LANE = MLA_QK
GATE_LANE = BIAS_LANE + 1


def _head_lane_source():
    src = [-1] * LANES
    nope_lanes = list(range(ROPE_HALF, X2_LO)) + list(range(X2_LO + ROPE_HALF, MLA_QK))
    for f, lane in enumerate(nope_lanes):
        src[lane] = f
    for f in range(ROPE_HALF):
        src[X1_LO + f] = MLA_NOPE + f
        src[X2_LO + f] = MLA_NOPE + ROPE_HALF + f
    return src


HEAD_LANE_SOURCE = _head_lane_source()

A_U, A_V, A_Z = 0, 512, 1024
B_CQ, B_CKV, B_KR, B_Z = 1536, 1792, 1920, 2048
C_Q, C_K, C_V, C_Z = 2560, 2816, 3072, 3584
MIX_COLS_PAD = 4096
GATE_SRC = 4016

IN_TILE = 1024
MERGE_TILE = 512
ATTN_TILE = 256
ATTN_Q_TILE = 2 * ATTN_TILE
GLA_TILE = 256
ROWS_PER_STEP = 2
VMEM_LIMIT = 56 * 1024 * 1024

SCORE_BOUND_MARGIN = 1.02
FIXED_REFERENCE_MAX_BOUND = 40.0

NT_DIMS = (((1,), (1,)), ((), ()))
TN_DIMS = (((0,), (0,)), ((), ()))


def _layer_spec(stacked, layer):
    tail = stacked.shape[1:]
    index = (layer,) + (0,) * len(tail)
    return pl.BlockSpec((None,) + tail, lambda *_: index, pipeline_mode=pl.Buffered(1))


def _sigmoid(x):
    return 0.5 * (jnp.tanh(0.5 * x) + 1.0)


def _silu(x):
    return x * _sigmoid(x)


def _gelu_tanh(x):
    c = math.sqrt(2.0 / math.pi)
    return 0.5 * x * (1.0 + jnp.tanh(c * (x + 0.044715 * (x * x * x))))


def _log_sigmoid(x):
    return jnp.minimum(x, 0.0) - jnp.log(1.0 + jnp.exp(-jnp.abs(x)))


def _rms(x, width):
    return lax.rsqrt(jnp.sum(x * x, axis=-1, keepdims=True) * (1.0 / width) + EPS)


def _rope_body(pos_ref, cos_ref, sin_ref):
    tm = pos_ref.shape[-1]
    fidx = lax.broadcasted_iota(jnp.int32, (ROPE_HALF, 1), 0).astype(F32)
    inv = 1.0 / jnp.exp(fidx * (2.0 / MLA_ROPE) * math.log(ROPE_THETA))
    ang = inv * pos_ref[0].astype(F32)
    cos = jnp.cos(ang)
    sin = jnp.sin(ang)
    nope_a = X2_LO - (X1_LO + ROPE_HALF)
    nope_b = MLA_QK - (X2_LO + ROPE_HALF)
    ones = lambda n: jnp.ones((n, tm), F32)
    zeros = lambda n: jnp.zeros((n, tm), F32)
    cos_t = jnp.concatenate([cos, ones(nope_a), cos, ones(nope_b), zeros(LANES - MLA_QK)], axis=0)
    sin_t = jnp.concatenate([-sin, zeros(nope_a), sin, zeros(nope_b + LANES - MLA_QK)], axis=0)
    cos_ref[...] = cos_t.T
    sin_ref[...] = sin_t.T


def _rope_tables(positions):
    t = positions.size
    tm = 1024 if t % 1024 == 0 else ATTN_TILE
    return pl.pallas_call(
        _rope_body,
        grid=(t // tm,),
        in_specs=[pl.BlockSpec((1, 1, tm), lambda i: (i, 0, 0))],
        out_specs=[pl.BlockSpec((tm, LANES), lambda i: (i, 0))] * 2,
        out_shape=[jax.ShapeDtypeStruct((t, LANES), F32)] * 2,
        name="rope_tables",
    )(positions.reshape(t // tm, 1, tm))


def _in_proj_body(x_ref, cos_ref, sin_ref, ng_ref, win_ref, lng_ref, lnb_ref, sgw_ref, sgbt_ref,
                  cqg_ref, ckvg_ref, wuq_ref, wuk_ref, wuvt_ref, qg_ref, kg_ref, qpad_ref, kpad_ref,
                  wgu_ref, bgu_ref,
                  ya_ref, qb_ref, kb_ref, vbt_ref, zb_ref, qc_ref, kc_ref, vc_ref, la_ref, zc_ref):
    tm = x_ref.shape[0]
    x = x_ref[...]
    h = (x * _rms(x, D_MODEL) * ng_ref[...]).astype(BF16)

    def proj(lo, width):
        return jnp.dot(h, win_ref[:, lo:lo + width], preferred_element_type=F32)

    u = _gelu_tanh(proj(A_U, BRANCH_W))
    v = _gelu_tanh(proj(A_V, BRANCH_W))
    mu = jnp.mean(v, axis=-1, keepdims=True)
    vc = v - mu
    vn = vc * lax.rsqrt(jnp.mean(vc * vc, axis=-1, keepdims=True) + EPS)
    vn = (vn * lng_ref[...] + lnb_ref[...]).astype(BF16)
    uz = u * _silu(proj(A_Z, BRANCH_W))
    ri = lax.broadcasted_iota(jnp.int32, (SG_BLOCK, SG_BLOCK), 0) // CHUNK
    ci = lax.broadcasted_iota(jnp.int32, (SG_BLOCK, SG_BLOCK), 1) // CHUNK
    chunk_causal = ri >= ci
    for g in range(SG_GROUPS):
        wg = jnp.where(chunk_causal, sgw_ref[g], 0.0).astype(BF16)
        bias = sgbt_ref[:, g:g + 1]
        cols = slice(g * LANES, (g + 1) * LANES)
        for n in range(tm // SG_BLOCK):
            rows = slice(n * SG_BLOCK, (n + 1) * SG_BLOCK)
            sv = jnp.dot(wg, vn[rows, cols], preferred_element_type=F32) + bias
            ya_ref[rows, cols] = (uz[rows, cols] * sv).astype(BF16)

    cq = proj(B_CQ, MLA_Q_RANK)
    cqn = (cq * _rms(cq, MLA_Q_RANK) * cqg_ref[...]).astype(BF16)
    q = jnp.dot(cqn, wuq_ref[...], preferred_element_type=F32)
    ckv = proj(B_CKV, MLA_KV_RANK)
    ckvn = (ckv * _rms(ckv, MLA_KV_RANK) * ckvg_ref[...]).astype(BF16)
    kn = jnp.dot(ckvn, wuk_ref[...], preferred_element_type=F32)
    for s in range(tm // ATTN_TILE):
        vbt_ref[s] = lax.dot_general(wuvt_ref[...], ckvn[s * ATTN_TILE:(s + 1) * ATTN_TILE], NT_DIMS,
                                     preferred_element_type=F32).astype(BF16)
    kr_glr = proj(B_KR, LANES)
    lane = lax.broadcasted_iota(jnp.int32, (1, LANES), 1)
    kr = jnp.where(lane < MLA_QK, kr_glr, 0.0)
    cos = cos_ref[...]
    sin = sin_ref[...]

    def norm_rope(xh, gain, pad_row):
        xn = xh * _rms(xh, MLA_QK) * gain
        partner = pltpu.roll(xn, LANES // 2, 1)
        return (xn * cos + partner * sin + pad_row).astype(BF16)

    half = BRANCH_W // 2

    def gated(ref, src, part):
        def run():
            ref[:, part * half:(part + 1) * half] = _silu(proj(src + part * half, half)).astype(BF16)
        return run

    def plain(ref, src, part, scale=None):
        def run():
            r = proj(src + part * half, half)
            ref[:, part * half:(part + 1) * half] = (r if scale is None else r * scale).astype(BF16)
        return run

    def decay():
        gl = jnp.dot(kr_glr.astype(BF16), wgu_ref[...], preferred_element_type=F32) + bgu_ref[...]
        la_ref[...] = _log_sigmoid(gl) * (1.0 / GLA_TAU)

    pieces = [gated(zb_ref, B_Z, 0), gated(zb_ref, B_Z, 1), plain(qc_ref, C_Q, 0, GLA_DK ** -0.5),
              plain(kc_ref, C_K, 0), plain(vc_ref, C_V, 0), plain(vc_ref, C_V, 1), decay,
              gated(zc_ref, C_Z, 0), gated(zc_ref, C_Z, 1)]

    for hd in range(MLA_HEADS):
        cols = slice(hd * HEAD_PAD, (hd + 1) * HEAD_PAD)
        pieces[hd]()
        qb_ref[:, cols] = norm_rope(q[:, cols], qg_ref[...], qpad_ref[...])
        kb_ref[:, cols] = norm_rope(kn[:, cols] + kr, kg_ref[...], kpad_ref[...])
    for piece in pieces[MLA_HEADS:]:
        piece()


def _in_proj(x2, cos_t, sin_t, p, layer):
    t = x2.shape[0]
    tm = IN_TILE
    row = lambda w: pl.BlockSpec((tm, w), lambda i: (i, 0))
    consts = [p["norm_g"], p["w_mix"], p["sg_ln_g"], p["sg_ln_b"], p["sg_w"], p["sg_bt"],
              p["cq_g"], p["ckv_g"], p["w_uq"], p["w_uk"], p["w_uvt"], p["q_g"], p["k_g"],
              p["q_pad"], p["k_pad"], p["w_gu"], p["b_gu"]]
    out_widths = [(BRANCH_W, BF16), (MLA_HEADS * HEAD_PAD, BF16), (MLA_HEADS * HEAD_PAD, BF16),
                  None, (BRANCH_W, BF16),
                  (GLA_HEADS * GLA_DK, BF16), (GLA_HEADS * GLA_DK, BF16), (GLA_HEADS * GLA_DV, BF16),
                  (GLA_HEADS * GLA_DK, F32), (BRANCH_W, BF16)]
    per_step = tm // ATTN_TILE
    vt_spec = pl.BlockSpec((per_step, MLA_HEADS * MLA_V, ATTN_TILE), lambda i: (i, 0, 0))
    vt_shape = jax.ShapeDtypeStruct((t // ATTN_TILE, MLA_HEADS * MLA_V, ATTN_TILE), BF16)
    return pl.pallas_call(
        _in_proj_body,
        grid=(t // tm,),
        in_specs=[row(D_MODEL), row(LANES), row(LANES)] + [_layer_spec(c, layer) for c in consts],
        out_specs=[vt_spec if o is None else row(o[0]) for o in out_widths],
        out_shape=[vt_shape if o is None else jax.ShapeDtypeStruct((t, o[0]), o[1]) for o in out_widths],
        compiler_params=pltpu.CompilerParams(dimension_semantics=("parallel",),
                                             vmem_limit_bytes=VMEM_LIMIT),
        name="in_proj",
    )(x2, cos_t, sin_t, *consts)


def _gla_body(q_ref, k_ref, v_ref, la_ref, z_ref, og_ref, y_ref, state_ref):
    @pl.when(pl.program_id(1) == 0)
    def _():
        state_ref[...] = jnp.zeros_like(state_ref)

    n_rows, tc = q_ref.shape[0], q_ref.shape[1]
    kw = GLA_HEADS * GLA_DK
    stack = GLA_HEADS * CHUNK
    ri = lax.broadcasted_iota(jnp.int32, (tc, tc), 0)
    ci = lax.broadcasted_iota(jnp.int32, (tc, tc), 1)
    tri = jnp.where((ri // CHUNK == ci // CHUNK) & (ri >= ci), 1.0, 0.0).astype(BF16)
    si = lax.broadcasted_iota(jnp.int32, (stack, CHUNK), 0) % CHUNK
    sj = lax.broadcasted_iota(jnp.int32, (stack, CHUNK), 1)
    causal = si >= sj
    lane = lax.broadcasted_iota(jnp.int32, (1, kw), 1)
    cums = []
    for r in range(n_rows):
        la = la_ref[r]
        la_hi = la.astype(BF16)
        la_lo = (la - la_hi.astype(F32)).astype(BF16)
        cums.append(jnp.dot(tri, la_hi, preferred_element_type=F32)
                    + jnp.dot(tri, la_lo, preferred_element_type=F32))
    for c in range(tc // CHUNK):
        rows = slice(c * CHUNK, (c + 1) * CHUNK)
        for r in range(n_rows):
            b = cums[r][rows]
            b_last = b[CHUNK - 1:CHUNK, :]
            qt = q_ref[r, rows, :].astype(F32) * jnp.exp(b)
            kf = k_ref[r, rows, :].astype(F32)
            kt = (kf * jnp.exp(-b)).astype(BF16)
            ks = (kf * jnp.exp(b_last - b)).astype(BF16)
            dec = jnp.exp(b_last)
            q_stack = jnp.concatenate(
                [jnp.where((lane >= hd * GLA_DK) & (lane < (hd + 1) * GLA_DK), qt, 0.0).astype(BF16)
                 for hd in range(GLA_HEADS)], axis=0)
            v = v_ref[r, rows, :]
            att = lax.dot_general(q_stack, kt, NT_DIMS, preferred_element_type=F32)
            att = jnp.where(causal, att, 0.0).astype(BF16)
            st = state_ref[r]
            o_intra = jnp.dot(att, v, preferred_element_type=F32)
            o_inter = lax.dot_general(q_stack, st.astype(BF16), NT_DIMS, preferred_element_type=F32)
            state_ref[r] = st * dec + lax.dot_general(v, ks, TN_DIMS, preferred_element_type=F32)
            for hd in range(GLA_HEADS):
                srows = slice(hd * CHUNK, (hd + 1) * CHUNK)
                cols = slice(hd * GLA_DV, (hd + 1) * GLA_DV)
                o = o_intra[srows, cols] + o_inter[srows, cols]
                on = o * _rms(o, GLA_DV) * og_ref[...]
                y_ref[r, rows, cols] = (on * z_ref[r, rows, cols].astype(F32)).astype(BF16)


def _gla(qc, kc, vc, la, zc, o_g, layer, batch, seq):
    tc = GLA_TILE
    rps = ROWS_PER_STEP
    kw = GLA_HEADS * GLA_DK
    vw = GLA_HEADS * GLA_DV
    row = lambda w: pl.BlockSpec((rps, tc, w), lambda b, s: (b, s, 0))
    rows3 = lambda a: a.reshape(batch, seq, a.shape[-1])
    return pl.pallas_call(
        _gla_body,
        grid=(batch // rps, seq // tc),
        in_specs=[row(kw), row(kw), row(vw), row(kw), row(vw), _layer_spec(o_g, layer)],
        out_specs=row(vw),
        out_shape=jax.ShapeDtypeStruct((batch, seq, vw), BF16),
        scratch_shapes=[pltpu.VMEM((rps, vw, kw), F32)],
        compiler_params=pltpu.CompilerParams(dimension_semantics=("parallel", "arbitrary"),
                                             vmem_limit_bytes=VMEM_LIMIT),
        name="gla",
    )(rows3(qc), rows3(kc), rows3(vc), rows3(la), rows3(zc), o_g).reshape(batch * seq, vw)


def _attn_body(layer, fixed_ref, q_ref, k_ref, vt_ref, z_ref, y_ref, m_ref, l_ref, alpha_ref, acc_ref,
               s_ref, p_ref):
    tq = q_ref.shape[0]
    tk = vt_ref.shape[2]
    i = pl.program_id(1)
    ki = lax.broadcasted_iota(jnp.int32, (tk, tq), 0) // CHUNK
    qi = lax.broadcasted_iota(jnp.int32, (tk, tq), 1) // CHUNK
    visible = {"low": ki <= qi, "high": ki + tk // CHUNK <= qi}
    slabs = MLA_V // SUBLANES

    def all_sublanes(x, op):
        for shift in (4, 2, 1):
            x = op(x, pltpu.roll(x, shift, 0))
        return x

    def reduce_rows(st, op):
        parts = op(st.reshape(4, tk // SUBLANES // 4, SUBLANES, tq), axis=1)
        return op(parts, axis=0)

    def block_scores(hd, j, diagonal):
        start = pl.multiple_of(j * tk, tk)
        cols = slice(hd * HEAD_PAD, (hd + 1) * HEAD_PAD)
        st = lax.dot_general(k_ref[pl.ds(start, tk), cols], q_ref[:, cols], NT_DIMS,
                             preferred_element_type=F32)
        if diagonal is not None:
            st = jnp.where(visible[diagonal], st, -jnp.inf)
        return st

    heads = range(MLA_HEADS)
    low, high = 2 * i, 2 * i + 1
    l_ref[...] = jnp.zeros(l_ref.shape, F32)
    acc_ref[...] = jnp.zeros(acc_ref.shape, F32)


    def scores_exp(hd, j, slot, diagonal=None):
        p = jnp.exp2(block_scores(hd, j, diagonal).reshape(tk // SUBLANES, SUBLANES, tq))
        l_ref[hd] = l_ref[hd] + all_sublanes(reduce_rows(p, jnp.sum), jnp.add)
        p_ref[slot, hd] = p.reshape(tk, tq).astype(BF16)

    def values_plain(hd, j, slot):
        rows = slice(hd * MLA_V, (hd + 1) * MLA_V)
        acc_ref[rows, :] = acc_ref[rows, :] + jnp.dot(vt_ref[j, rows, :], p_ref[slot, hd],
                                                      preferred_element_type=F32)

    @pl.when(fixed_ref[layer] != 0)
    def _():
        for hd in heads:
            scores_exp(hd, low, 0, "low")
        for hd in heads:
            values_plain(hd, low, 0)
        for hd in heads:
            scores_exp(hd, high, 1, "high")

        def pair(u, carry):
            for hd in heads:
                values_plain(hd, jnp.where(u == 0, high, 2 * u - 1), 1)
            for hd in heads:
                scores_exp(hd, 2 * u, 0)
            for hd in heads:
                values_plain(hd, 2 * u, 0)
            for hd in heads:
                scores_exp(hd, 2 * u + 1, 1)
            return carry

        lax.fori_loop(0, i, pair, 0)
        for hd in heads:
            values_plain(hd, jnp.where(i == 0, high, low - 1), 1)

    def softmax_running_max(hd):
        st = s_ref[hd].reshape(tk // SUBLANES, SUBLANES, tq)
        m_prev = m_ref[hd]
        m_new = jnp.maximum(m_prev, all_sublanes(reduce_rows(st, jnp.max), jnp.maximum))
        alpha = jnp.exp2(m_prev - m_new)
        p = jnp.exp2(st - m_new[None])
        l_ref[hd] = alpha * l_ref[hd] + all_sublanes(reduce_rows(p, jnp.sum), jnp.add)
        p_ref[0, hd] = p.reshape(tk, tq).astype(BF16)
        alpha_ref[hd] = alpha
        m_ref[hd] = m_new

    def values_rescaled(hd, j):
        rows = slice(hd * MLA_V, (hd + 1) * MLA_V)
        pv = jnp.dot(vt_ref[j, rows, :], p_ref[0, hd], preferred_element_type=F32)
        acc = acc_ref[rows, :].reshape(slabs, SUBLANES, tq) * alpha_ref[hd][None]
        acc_ref[rows, :] = acc.reshape(MLA_V, tq) + pv

    def round_(prev_block, next_block, diagonal=None):
        if prev_block is not None:
            for hd in heads:
                values_rescaled(hd, prev_block)
        for hd in heads:
            softmax_running_max(hd)
        if next_block is not None:
            for hd in heads:
                s_ref[hd] = block_scores(hd, next_block, diagonal)

    @pl.when(fixed_ref[layer] == 0)
    def _():
        m_ref[...] = jnp.full(m_ref.shape, -jnp.inf, F32)
        for hd in heads:
            s_ref[hd] = block_scores(hd, low, "low")
        round_(None, high, "high")

        @pl.when(i == 0)
        def _():
            round_(low, None)

        @pl.when(i > 0)
        def _():
            round_(low, 0)

            def body(t, carry):
                round_(jnp.where(t == 2, high, t - 3), t - 1)
                return carry

            lax.fori_loop(2, high, body, 0)
            round_(low - 2, None)

        for hd in heads:
            values_rescaled(hd, jnp.where(i == 0, high, low - 1))

    for hd in range(MLA_HEADS):
        rows = slice(hd * MLA_V, (hd + 1) * MLA_V)
        acc = acc_ref[rows, :].reshape(slabs, SUBLANES, tq) / l_ref[hd][None]
        acc_ref[rows, :] = acc.reshape(MLA_V, tq)
    y_ref[...] = (acc_ref[...].T * z_ref[...].astype(F32)).astype(BF16)


def _attn(fixed, qb, kb, vbt, zb, layer, batch, seq):
    tq, tk = ATTN_Q_TILE, ATTN_TILE
    nq = seq // tq
    hw = MLA_HEADS * HEAD_PAD
    stat = pltpu.VMEM((MLA_HEADS, SUBLANES, tq), F32)
    return pl.pallas_call(
        functools.partial(_attn_body, layer),
        grid=(batch, nq),
        in_specs=[pl.BlockSpec(memory_space=pltpu.SMEM),
                  pl.BlockSpec((tq, hw), lambda b, i: (b * nq + i, 0)),
                  pl.BlockSpec((seq, hw), lambda b, i: (b, 0)),
                  pl.BlockSpec((seq // tk, BRANCH_W, tk), lambda b, i: (b, 0, 0)),
                  pl.BlockSpec((tq, BRANCH_W), lambda b, i: (b * nq + i, 0))],
        out_specs=pl.BlockSpec((tq, BRANCH_W), lambda b, i: (b * nq + i, 0)),
        out_shape=jax.ShapeDtypeStruct((batch * seq, BRANCH_W), BF16),
        scratch_shapes=[stat, stat, stat,
                        pltpu.VMEM((MLA_HEADS * MLA_V, tq), F32),
                        pltpu.VMEM((MLA_HEADS, tk, tq), F32),
                        pltpu.VMEM((2, MLA_HEADS, tk, tq), BF16)],
        compiler_params=pltpu.CompilerParams(dimension_semantics=("parallel", "arbitrary"),
                                             vmem_limit_bytes=VMEM_LIMIT),
        name="mla_attn",
    )(fixed, qb, kb, vbt, zb)


def _merge_body(x_ref, ya_ref, yb_ref, yc_ref, ng_ref, wgate_ref, bgate_ref, wbr_ref, wout_ref, o_ref):
    x = x_ref[...]
    h = (x * _rms(x, D_MODEL) * ng_ref[...]).astype(BF16)
    merged = None
    for n, y_ref in enumerate((ya_ref, yb_ref, yc_ref)):
        cols = slice(n * D_MODEL, (n + 1) * D_MODEL)
        logits = jnp.dot(h, wgate_ref[:, cols], preferred_element_type=F32) + bgate_ref[:, cols]
        term = _sigmoid(logits) * jnp.dot(y_ref[...], wbr_ref[n], preferred_element_type=F32)
        merged = term if merged is None else merged + term
    o_ref[...] = x + jnp.dot(merged.astype(BF16), wout_ref[...], preferred_element_type=F32)


def _merge(x2, ya, yb, yc, p, layer):
    t = x2.shape[0]
    tm = MERGE_TILE
    row = lambda w: pl.BlockSpec((tm, w), lambda i: (i, 0))
    consts = [p["norm_g"], p["w_gate"], p["b_gate"], p["w_branch"], p["w_out"]]
    return pl.pallas_call(
        _merge_body,
        grid=(t // tm,),
        in_specs=[row(D_MODEL), row(BRANCH_W), row(BRANCH_W), row(BRANCH_W)]
                 + [_layer_spec(c, layer) for c in consts],
        out_specs=row(D_MODEL),
        out_shape=jax.ShapeDtypeStruct((t, D_MODEL), F32),
        compiler_params=pltpu.CompilerParams(dimension_semantics=("parallel",),
                                             vmem_limit_bytes=VMEM_LIMIT),
        name="merge",
    )(x2, ya, yb, yc, *consts)


def _head_layout(w, heads):
    lead = w.shape[:-1]
    w = w.reshape(lead + (heads, MLA_QK))
    w = jnp.pad(w, [(0, 0)] * (len(lead) + 1) + [(0, 1)])
    src = jnp.asarray([MLA_QK if s < 0 else s for s in HEAD_LANE_SOURCE], jnp.int32)
    return jnp.take(w, src, axis=-1).reshape(lead + (heads * HEAD_PAD,))


def _pack_params(norm_g, w_in, b_gate, sg_ln_g, sg_ln_b, sg_w, sg_b, mla_cq_g, mla_ckv_g, mla_w_uq,
                 mla_w_ukv, mla_q_g, mla_k_g, gla_w_gate, gla_b_gate, gla_o_g, w_branch, w_out):
    depth, d = w_in.shape[0], w_in.shape[1]
    w = w_in
    row = lambda g: g.reshape(depth, 1, -1)
    kr_block = _head_layout(
        jnp.concatenate([jnp.zeros((depth, d, MLA_NOPE), w.dtype), w[:, :, 1920:1952]], axis=2), 1)
    kr_block = kr_block.at[:, :, GATE_LANE:GATE_LANE + GLA_GATE_RANK].set(w[:, :, 3488:3504])
    w_mix = jnp.concatenate([w[:, :, :1920], kr_block, w[:, :, 1952:3488], w[:, :, 3504:GATE_SRC]],
                            axis=2)
    ukv = mla_w_ukv.reshape(depth, MLA_KV_RANK, MLA_HEADS, MLA_NOPE + MLA_V)
    w_uk = _head_layout(jnp.pad(ukv[..., :MLA_NOPE], ((0, 0), (0, 0), (0, 0), (0, MLA_ROPE)))
                        .reshape(depth, MLA_KV_RANK, -1), MLA_HEADS)
    w_uvt = jnp.swapaxes(ukv[..., MLA_NOPE:].reshape(depth, MLA_KV_RANK, -1), 1, 2)
    w_gu = jnp.zeros((depth, LANES, GLA_HEADS * GLA_DK), F32).at[
        :, GATE_LANE:GATE_LANE + GLA_GATE_RANK].set(gla_w_gate)
    q_g = row(_head_layout(mla_q_g, 1)) * (MLA_QK ** -0.5 * math.log2(math.e))
    k_g = row(_head_layout(mla_k_g, 1))
    bound = (MLA_QK * SCORE_BOUND_MARGIN) * (jnp.max(jnp.abs(q_g), axis=(1, 2), keepdims=True)
                                             * jnp.max(jnp.abs(k_g), axis=(1, 2), keepdims=True))
    fixed = bound <= FIXED_REFERENCE_MAX_BOUND
    bias_lane = (jnp.arange(HEAD_PAD) == BIAS_LANE).reshape(1, 1, HEAD_PAD)
    return {
        "norm_g": row(norm_g),
        "w_mix": w_mix.astype(BF16),
        "w_gate": w[:, :, GATE_SRC:].astype(BF16),
        "sg_ln_g": row(sg_ln_g),
        "sg_ln_b": row(sg_ln_b),
        "sg_w": sg_w,
        "sg_bt": jnp.swapaxes(sg_b, 1, 2),
        "cq_g": row(mla_cq_g),
        "ckv_g": row(mla_ckv_g),
        "w_uq": _head_layout(mla_w_uq, MLA_HEADS).astype(BF16),
        "w_uk": w_uk.astype(BF16),
        "w_uvt": w_uvt.astype(BF16),
        "q_g": q_g,
        "k_g": k_g,
        "q_pad": jnp.where(bias_lane & fixed, -bound, 0.0).astype(F32),
        "k_pad": jnp.broadcast_to(jnp.where(bias_lane, 1.0, 0.0).astype(F32), (depth, 1, HEAD_PAD)),
        "fixed": fixed.astype(jnp.int32).reshape(depth),
        "w_gu": w_gu.astype(BF16),
        "b_gu": row(gla_b_gate),
        "b_gate": row(b_gate),
        "o_g": row(gla_o_g),
        "w_branch": w_branch.astype(BF16),
        "w_out": w_out.astype(BF16),
    }


def kernel(x, positions, norm_g, w_in, b_gate, sg_ln_g, sg_ln_b, sg_w, sg_b, mla_cq_g, mla_ckv_g, mla_w_uq, mla_w_ukv, mla_q_g, mla_k_g, gla_w_gate, gla_b_gate, gla_o_g, w_branch, w_out):
    batch, seq, d = x.shape
    assert d == D_MODEL and seq % max(ATTN_Q_TILE, GLA_TILE) == 0
    assert (batch * seq) % IN_TILE == 0 and (batch * seq) % MERGE_TILE == 0
    assert batch % ROWS_PER_STEP == 0
    assert IN_TILE % ATTN_TILE == 0 and IN_TILE % SG_BLOCK == 0
    depth = w_in.shape[0]
    cos_t, sin_t = _rope_tables(positions)
    x2 = x.reshape(batch * seq, d)
    p = _pack_params(norm_g, w_in, b_gate, sg_ln_g, sg_ln_b, sg_w, sg_b, mla_cq_g, mla_ckv_g,
                     mla_w_uq, mla_w_ukv, mla_q_g, mla_k_g, gla_w_gate, gla_b_gate, gla_o_g,
                     w_branch, w_out)
    for l in range(depth):
        ya, qb, kb, vbt, zb, qc, kc, vc, la, zc = _in_proj(x2, cos_t, sin_t, p, l)
        yc = _gla(qc, kc, vc, la, zc, p["o_g"], l, batch, seq)
        yb = _attn(p["fixed"], qb, kb, vbt, zb, l, batch, seq)
        x2 = _merge(x2, ya, yb, yc, p, l)
    return x2.reshape(batch, seq, d)
```

```python
import functools
import math

import jax
import jax.numpy as jnp
from jax import lax
from jax.experimental import pallas as pl
from jax.experimental.pallas import tpu as pltpu

F32 = jnp.float32
BF16 = jnp.bfloat16

D_MODEL = 1024
CHUNK = 64
BRANCH_W = 512
N_BRANCH = 3
EPS = 1e-6
SG_BLOCK = 128
SG_GROUPS = 4
MLA_HEADS = 8
MLA_NOPE = 64
MLA_ROPE = 32
MLA_QK = MLA_NOPE + MLA_ROPE
MLA_V = 64
MLA_Q_RANK = 256
MLA_KV_RANK = 128
ROPE_THETA = 10000.0
GLA_HEADS = 4
GLA_DK = 64
GLA_DV = 128
GLA_GATE_RANK = 16
GLA_TAU = 16.0

LANES = 128
SUBLANES = 8
HEAD_PAD = LANES
ROPE_HALF = MLA_ROPE // 2
X1_LO = 0
X2_LO = LANES // 2
BIAS_LANE = MLA_QK
GATE_LANE = BIAS_LANE + 1


def _head_lane_source():
    src = [-1] * LANES
    nope_lanes = list(range(ROPE_HALF, X2_LO)) + list(range(X2_LO + ROPE_HALF, MLA_QK))
    for f, lane in enumerate(nope_lanes):
        src[lane] = f
    for f in range(ROPE_HALF):
        src[X1_LO + f] = MLA_NOPE + f
        src[X2_LO + f] = MLA_NOPE + ROPE_HALF + f
    return src


HEAD_LANE_SOURCE = _head_lane_source()

A_U, A_V, A_Z = 0, 512, 1024
B_CQ, B_CKV, B_KR, B_Z = 1536, 1792, 1920, 2048
C_Q, C_K, C_V, C_Z = 2560, 2816, 3072, 3584
MIX_COLS_PAD = 4096
GATE_SRC = 4016

IN_TILE = 1024
MERGE_TILE = 512
ATTN_TILE = 256
ATTN_Q_TILE = 2 * ATTN_TILE
GLA_TILE = 512
ROWS_PER_STEP = 2
VMEM_LIMIT = 56 * 1024 * 1024

SCORE_BOUND_MARGIN = 1.02
FIXED_REFERENCE_MAX_BOUND = 40.0

NT_DIMS = (((1,), (1,)), ((), ()))
TN_DIMS = (((0,), (0,)), ((), ()))


def _layer_spec(stacked, layer):
    tail = stacked.shape[1:]
    index = (layer,) + (0,) * len(tail)
    return pl.BlockSpec((None,) + tail, lambda *_: index, pipeline_mode=pl.Buffered(1))


def _sigmoid(x):
    return 0.5 * (jnp.tanh(0.5 * x) + 1.0)


def _silu(x):
    return x * _sigmoid(x)


def _gelu_tanh(x):
    c = math.sqrt(2.0 / math.pi)
    return 0.5 * x * (1.0 + jnp.tanh(c * (x + 0.044715 * (x * x * x))))


def _log_sigmoid(x):
    return jnp.minimum(x, 0.0) - jnp.log(1.0 + jnp.exp(-jnp.abs(x)))


def _rms(x, width):
    return lax.rsqrt(jnp.sum(x * x, axis=-1, keepdims=True) * (1.0 / width) + EPS)


def _rope_body(pos_ref, cos_ref, sin_ref):
    tm = pos_ref.shape[-1]
    fidx = lax.broadcasted_iota(jnp.int32, (ROPE_HALF, 1), 0).astype(F32)
    inv = 1.0 / jnp.exp(fidx * (2.0 / MLA_ROPE) * math.log(ROPE_THETA))
    ang = inv * pos_ref[0].astype(F32)
    cos = jnp.cos(ang)
    sin = jnp.sin(ang)
    nope_a = X2_LO - (X1_LO + ROPE_HALF)
    nope_b = MLA_QK - (X2_LO + ROPE_HALF)
    ones = lambda n: jnp.ones((n, tm), F32)
    zeros = lambda n: jnp.zeros((n, tm), F32)
    cos_t = jnp.concatenate([cos, ones(nope_a), cos, ones(nope_b), zeros(LANES - MLA_QK)], axis=0)
    sin_t = jnp.concatenate([-sin, zeros(nope_a), sin, zeros(nope_b + LANES - MLA_QK)], axis=0)
    cos_ref[...] = cos_t.T
    sin_ref[...] = sin_t.T


def _rope_tables(positions):
    t = positions.size
    tm = 1024 if t % 1024 == 0 else ATTN_TILE
    return pl.pallas_call(
        _rope_body,
        grid=(t // tm,),
        in_specs=[pl.BlockSpec((1, 1, tm), lambda i: (i, 0, 0))],
        out_specs=[pl.BlockSpec((tm, LANES), lambda i: (i, 0))] * 2,
        out_shape=[jax.ShapeDtypeStruct((t, LANES), F32)] * 2,
        name="rope_tables",
    )(positions.reshape(t // tm, 1, tm))


def _in_proj_body(x_ref, cos_ref, sin_ref, ng_ref, win_ref, lng_ref, lnb_ref, sgw_ref, sgbt_ref,
                  cqg_ref, ckvg_ref, wuq_ref, wuk_ref, wuvt_ref, qg_ref, kg_ref, qpad_ref, kpad_ref,
                  wgu_ref, bgu_ref,
                  ya_ref, qb_ref, kb_ref, vbt_ref, zb_ref, qc_ref, kc_ref, vc_ref, la_ref, zc_ref):
    tm = x_ref.shape[0]
    x = x_ref[...]
    h = (x * _rms(x, D_MODEL) * ng_ref[...]).astype(BF16)

    def proj(lo, width):
        return jnp.dot(h, win_ref[:, lo:lo + width], preferred_element_type=F32)

    u = _gelu_tanh(proj(A_U, BRANCH_W))
    v = _gelu_tanh(proj(A_V, BRANCH_W))
    mu = jnp.mean(v, axis=-1, keepdims=True)
    vc = v - mu
    vn = vc * lax.rsqrt(jnp.mean(vc * vc, axis=-1, keepdims=True) + EPS)
    vn = (vn * lng_ref[...] + lnb_ref[...]).astype(BF16)
    uz = u * _silu(proj(A_Z, BRANCH_W))
    ri = lax.broadcasted_iota(jnp.int32, (SG_BLOCK, SG_BLOCK), 0) // CHUNK
    ci = lax.broadcasted_iota(jnp.int32, (SG_BLOCK, SG_BLOCK), 1) // CHUNK
    chunk_causal = ri >= ci
    for g in range(SG_GROUPS):
        wg = jnp.where(chunk_causal, sgw_ref[g], 0.0).astype(BF16)
        bias = sgbt_ref[:, g:g + 1]
        cols = slice(g * LANES, (g + 1) * LANES)
        for n in range(tm // SG_BLOCK):
            rows = slice(n * SG_BLOCK, (n + 1) * SG_BLOCK)
            sv = jnp.dot(wg, vn[rows, cols], preferred_element_type=F32) + bias
            ya_ref[rows, cols] = (uz[rows, cols] * sv).astype(BF16)

    cq = proj(B_CQ, MLA_Q_RANK)
    cqn = (cq * _rms(cq, MLA_Q_RANK) * cqg_ref[...]).astype(BF16)
    q = jnp.dot(cqn, wuq_ref[...], preferred_element_type=F32)
    ckv = proj(B_CKV, MLA_KV_RANK)
    ckvn = (ckv * _rms(ckv, MLA_KV_RANK) * ckvg_ref[...]).astype(BF16)
    kn = jnp.dot(ckvn, wuk_ref[...], preferred_element_type=F32)
    for s in range(tm // ATTN_TILE):
        vbt_ref[s] = lax.dot_general(wuvt_ref[...], ckvn[s * ATTN_TILE:(s + 1) * ATTN_TILE], NT_DIMS,
                                     preferred_element_type=F32).astype(BF16)
    kr_glr = proj(B_KR, LANES)
    lane = lax.broadcasted_iota(jnp.int32, (1, LANES), 1)
    kr = jnp.where(lane < MLA_QK, kr_glr, 0.0)
    cos = cos_ref[...]
    sin = sin_ref[...]

    def norm_rope(xh, gain, pad_row):
        xn = xh * _rms(xh, MLA_QK) * gain
        partner = pltpu.roll(xn, LANES // 2, 1)
        return (xn * cos + partner * sin + pad_row).astype(BF16)

    half = BRANCH_W // 2

    def gated(ref, src, part):
        def run():
            ref[:, part * half:(part + 1) * half] = _silu(proj(src + part * half, half)).astype(BF16)
        return run

    def plain(ref, src, part, scale=None):
        def run():
            r = proj(src + part * half, half)
            ref[:, part * half:(part + 1) * half] = (r if scale is None else r * scale).astype(BF16)
        return run

    def decay():
        gl = jnp.dot(kr_glr.astype(BF16), wgu_ref[...], preferred_element_type=F32) + bgu_ref[...]
        la_ref[...] = _log_sigmoid(gl) * (1.0 / GLA_TAU)

    pieces = [gated(zb_ref, B_Z, 0), gated(zb_ref, B_Z, 1), plain(qc_ref, C_Q, 0, GLA_DK ** -0.5),
              plain(kc_ref, C_K, 0), plain(vc_ref, C_V, 0), plain(vc_ref, C_V, 1), decay,
              gated(zc_ref, C_Z, 0), gated(zc_ref, C_Z, 1)]

    for hd in range(MLA_HEADS):
        cols = slice(hd * HEAD_PAD, (hd + 1) * HEAD_PAD)
        pieces[hd]()
        qb_ref[:, cols] = norm_rope(q[:, cols], qg_ref[...], qpad_ref[...])
        kb_ref[:, cols] = norm_rope(kn[:, cols] + kr, kg_ref[...], kpad_ref[...])
    for piece in pieces[MLA_HEADS:]:
        piece()


def _in_proj(x2, cos_t, sin_t, p, layer):
    t = x2.shape[0]
    tm = IN_TILE
    row = lambda w: pl.BlockSpec((tm, w), lambda i: (i, 0))
    consts = [p["norm_g"], p["w_mix"], p["sg_ln_g"], p["sg_ln_b"], p["sg_w"], p["sg_bt"],
              p["cq_g"], p["ckv_g"], p["w_uq"], p["w_uk"], p["w_uvt"], p["q_g"], p["k_g"],
              p["q_pad"], p["k_pad"], p["w_gu"], p["b_gu"]]
    out_widths = [(BRANCH_W, BF16), (MLA_HEADS * HEAD_PAD, BF16), (MLA_HEADS * HEAD_PAD, BF16),
                  None, (BRANCH_W, BF16),
                  (GLA_HEADS * GLA_DK, BF16), (GLA_HEADS * GLA_DK, BF16), (GLA_HEADS * GLA_DV, BF16),
                  (GLA_HEADS * GLA_DK, F32), (BRANCH_W, BF16)]
    per_step = tm // ATTN_TILE
    vt_spec = pl.BlockSpec((per_step, MLA_HEADS * MLA_V, ATTN_TILE), lambda i: (i, 0, 0))
    vt_shape = jax.ShapeDtypeStruct((t // ATTN_TILE, MLA_HEADS * MLA_V, ATTN_TILE), BF16)
    return pl.pallas_call(
        _in_proj_body,
        grid=(t // tm,),
        in_specs=[row(D_MODEL), row(LANES), row(LANES)] + [_layer_spec(c, layer) for c in consts],
        out_specs=[vt_spec if o is None else row(o[0]) for o in out_widths],
        out_shape=[vt_shape if o is None else jax.ShapeDtypeStruct((t, o[0]), o[1]) for o in out_widths],
        compiler_params=pltpu.CompilerParams(dimension_semantics=("parallel",),
                                             vmem_limit_bytes=VMEM_LIMIT),
        name="in_proj",
    )(x2, cos_t, sin_t, *consts)


def _gla_body(q_ref, k_ref, v_ref, la_ref, z_ref, og_ref, y_ref, state_ref):
    @pl.when(pl.program_id(1) == 0)
    def _():
        state_ref[...] = jnp.zeros_like(state_ref)

    n_rows, tc = q_ref.shape[0], q_ref.shape[1]
    kw = GLA_HEADS * GLA_DK
    stack = GLA_HEADS * CHUNK
    ri = lax.broadcasted_iota(jnp.int32, (tc, tc), 0)
    ci = lax.broadcasted_iota(jnp.int32, (tc, tc), 1)
    tri = jnp.where((ri // CHUNK == ci // CHUNK) & (ri >= ci), 1.0, 0.0).astype(BF16)
    si = lax.broadcasted_iota(jnp.int32, (stack, CHUNK), 0) % CHUNK
    sj = lax.broadcasted_iota(jnp.int32, (stack, CHUNK), 1)
    causal = si >= sj
    lane = lax.broadcasted_iota(jnp.int32, (1, kw), 1)
    cums = []
    for r in range(n_rows):
        la = la_ref[r]
        la_hi = la.astype(BF16)
        la_lo = (la - la_hi.astype(F32)).astype(BF16)
        cums.append(jnp.dot(tri, la_hi, preferred_element_type=F32)
                    + jnp.dot(tri, la_lo, preferred_element_type=F32))
    for c in range(tc // CHUNK):
        rows = slice(c * CHUNK, (c + 1) * CHUNK)
        for r in range(n_rows):
            b = cums[r][rows]
            b_last = b[CHUNK - 1:CHUNK, :]
            qt = q_ref[r, rows, :].astype(F32) * jnp.exp(b)
            kf = k_ref[r, rows, :].astype(F32)
            kt = (kf * jnp.exp(-b)).astype(BF16)
            ks = (kf * jnp.exp(b_last - b)).astype(BF16)
            dec = jnp.exp(b_last)
            q_stack = jnp.concatenate(
                [jnp.where((lane >= hd * GLA_DK) & (lane < (hd + 1) * GLA_DK), qt, 0.0).astype(BF16)
                 for hd in range(GLA_HEADS)], axis=0)
            v = v_ref[r, rows, :]
            att = lax.dot_general(q_stack, kt, NT_DIMS, preferred_element_type=F32)
            att = jnp.where(causal, att, 0.0).astype(BF16)
            st = state_ref[r]
            o_inter = lax.dot_general(q_stack, st.astype(BF16), NT_DIMS, preferred_element_type=F32)
            state_ref[r] = st * dec + lax.dot_general(v, ks, TN_DIMS, preferred_element_type=F32)
            for hd in range(GLA_HEADS):
                srows = slice(hd * CHUNK, (hd + 1) * CHUNK)
                cols = slice(hd * GLA_DV, (hd + 1) * GLA_DV)
                o = (jnp.dot(att[srows], v[:, cols], preferred_element_type=F32)
                     + o_inter[srows, cols])
                on = o * _rms(o, GLA_DV) * og_ref[...]
                y_ref[r, rows, cols] = (on * z_ref[r, rows, cols].astype(F32)).astype(BF16)


def _gla(qc, kc, vc, la, zc, o_g, layer, batch, seq):
    tc = GLA_TILE
    rps = ROWS_PER_STEP
    kw = GLA_HEADS * GLA_DK
    vw = GLA_HEADS * GLA_DV
    row = lambda w: pl.BlockSpec((rps, tc, w), lambda b, s: (b, s, 0))
    rows3 = lambda a: a.reshape(batch, seq, a.shape[-1])
    return pl.pallas_call(
        _gla_body,
        grid=(batch // rps, seq // tc),
        in_specs=[row(kw), row(kw), row(vw), row(kw), row(vw), _layer_spec(o_g, layer)],
        out_specs=row(vw),
        out_shape=jax.ShapeDtypeStruct((batch, seq, vw), BF16),
        scratch_shapes=[pltpu.VMEM((rps, vw, kw), F32)],
        compiler_params=pltpu.CompilerParams(dimension_semantics=("parallel", "arbitrary"),
                                             vmem_limit_bytes=VMEM_LIMIT),
        name="gla",
    )(rows3(qc), rows3(kc), rows3(vc), rows3(la), rows3(zc), o_g).reshape(batch * seq, vw)


def _attn_body(layer, fixed_ref, q_ref, k_ref, vt_ref, z_ref, y_ref, m_ref, l_ref, alpha_ref, acc_ref,
               s_ref, p_ref):
    tq = q_ref.shape[0]
    tk = vt_ref.shape[2]
    i = pl.program_id(1)
    ki = lax.broadcasted_iota(jnp.int32, (tk, tq), 0) // CHUNK
    qi = lax.broadcasted_iota(jnp.int32, (tk, tq), 1) // CHUNK
    visible = {"low": ki <= qi, "high": ki + tk // CHUNK <= qi}
    slabs = MLA_V // SUBLANES

    def all_sublanes(x, op):
        for shift in (4, 2, 1):
            x = op(x, pltpu.roll(x, shift, 0))
        return x

    def reduce_rows(st, op):
        parts = op(st.reshape(4, tk // SUBLANES // 4, SUBLANES, st.shape[-1]), axis=1)
        return op(parts, axis=0)

    def block_scores(hd, j, diagonal, queries=slice(None)):
        start = pl.multiple_of(j * tk, tk)
        cols = slice(hd * HEAD_PAD, (hd + 1) * HEAD_PAD)
        st = lax.dot_general(k_ref[pl.ds(start, tk), cols], q_ref[queries, cols], NT_DIMS,
                             preferred_element_type=F32)
        if diagonal is not None:
            st = jnp.where(visible[diagonal][:, queries], st, -jnp.inf)
        return st

    heads = range(MLA_HEADS)
    low, high = 2 * i, 2 * i + 1
    l_ref[...] = jnp.zeros(l_ref.shape, F32)
    acc_ref[...] = jnp.zeros(acc_ref.shape, F32)

    def scores_exp(hd, j, slot, diagonal=None, queries=slice(None)):
        st = block_scores(hd, j, diagonal, queries)
        p = jnp.exp2(st.reshape(tk // SUBLANES, SUBLANES, st.shape[-1]))
        l_ref[hd, :, queries] = l_ref[hd, :, queries] + all_sublanes(reduce_rows(p, jnp.sum), jnp.add)
        p_ref[slot, hd, :, queries] = p.reshape(st.shape).astype(BF16)

    def values_plain(hd, j, slot, queries=slice(None)):
        rows = slice(hd * MLA_V, (hd + 1) * MLA_V)
        acc_ref[rows, queries] = acc_ref[rows, queries] + jnp.dot(
            vt_ref[j, rows, :], p_ref[slot, hd, :, queries], preferred_element_type=F32)

    @pl.when(fixed_ref[layer] != 0)
    def _():
        upper = slice(tq // 2, tq)
        for hd in heads:
            scores_exp(hd, high, 0, "high", upper)
        for hd in heads:
            values_plain(hd, high, 0, upper)
        for hd in heads:
            scores_exp(hd, low, 1, "low")

        def pair(u, carry):
            for hd in heads:
                values_plain(hd, jnp.where(u == 0, low, 2 * u - 1), 1)
            for hd in heads:
                scores_exp(hd, 2 * u, 0)
            for hd in heads:
                values_plain(hd, 2 * u, 0)
            for hd in heads:
                scores_exp(hd, 2 * u + 1, 1)
            return carry

        lax.fori_loop(0, i, pair, 0)
        for hd in heads:
            values_plain(hd, jnp.where(i == 0, low, low - 1), 1)

    def softmax_running_max(hd):
        st = s_ref[hd].reshape(tk // SUBLANES, SUBLANES, tq)
        m_prev = m_ref[hd]
        m_new = jnp.maximum(m_prev, all_sublanes(reduce_rows(st, jnp.max), jnp.maximum))
        alpha = jnp.exp2(m_prev - m_new)
        p = jnp.exp2(st - m_new[None])
        l_ref[hd] = alpha * l_ref[hd] + all_sublanes(reduce_rows(p, jnp.sum), jnp.add)
        p_ref[0, hd] = p.reshape(tk, tq).astype(BF16)
        alpha_ref[hd] = alpha
        m_ref[hd] = m_new

    def values_rescaled(hd, j):
        rows = slice(hd * MLA_V, (hd + 1) * MLA_V)
        pv = jnp.dot(vt_ref[j, rows, :], p_ref[0, hd], preferred_element_type=F32)
        acc = acc_ref[rows, :].reshape(slabs, SUBLANES, tq) * alpha_ref[hd][None]
        acc_ref[rows, :] = acc.reshape(MLA_V, tq) + pv

    def round_(prev_block, next_block, diagonal=None):
        if prev_block is not None:
            for hd in heads:
                values_rescaled(hd, prev_block)
        for hd in heads:
            softmax_running_max(hd)
        if next_block is not None:
            for hd in heads:
                s_ref[hd] = block_scores(hd, next_block, diagonal)

    @pl.when(fixed_ref[layer] == 0)
    def _():
        m_ref[...] = jnp.full(m_ref.shape, -jnp.inf, F32)
        for hd in heads:
            s_ref[hd] = block_scores(hd, low, "low")
        round_(None, high, "high")

        @pl.when(i == 0)
        def _():
            round_(low, None)

        @pl.when(i > 0)
        def _():
            round_(low, 0)

            def body(t, carry):
                round_(jnp.where(t == 2, high, t - 3), t - 1)
                return carry

            lax.fori_loop(2, high, body, 0)
            round_(low - 2, None)

        for hd in heads:
            values_rescaled(hd, jnp.where(i == 0, high, low - 1))

    for hd in range(MLA_HEADS):
        rows = slice(hd * MLA_V, (hd + 1) * MLA_V)
        acc = acc_ref[rows, :].reshape(slabs, SUBLANES, tq) / l_ref[hd][None]
        acc_ref[rows, :] = acc.reshape(MLA_V, tq)
    y_ref[...] = (acc_ref[...].T * z_ref[...].astype(F32)).astype(BF16)


def _attn(fixed, qb, kb, vbt, zb, layer, batch, seq):
    tq, tk = ATTN_Q_TILE, ATTN_TILE
    nq = seq // tq
    hw = MLA_HEADS * HEAD_PAD
    stat = pltpu.VMEM((MLA_HEADS, SUBLANES, tq), F32)
    return pl.pallas_call(
        functools.partial(_attn_body, layer),
        grid=(batch, nq),
        in_specs=[pl.BlockSpec(memory_space=pltpu.SMEM),
                  pl.BlockSpec((tq, hw), lambda b, i: (b * nq + i, 0)),
                  pl.BlockSpec((seq, hw), lambda b, i: (b, 0)),
                  pl.BlockSpec((seq // tk, BRANCH_W, tk), lambda b, i: (b, 0, 0)),
                  pl.BlockSpec((tq, BRANCH_W), lambda b, i: (b * nq + i, 0))],
        out_specs=pl.BlockSpec((tq, BRANCH_W), lambda b, i: (b * nq + i, 0)),
        out_shape=jax.ShapeDtypeStruct((batch * seq, BRANCH_W), BF16),
        scratch_shapes=[stat, stat, stat,
                        pltpu.VMEM((MLA_HEADS * MLA_V, tq), F32),
                        pltpu.VMEM((MLA_HEADS, tk, tq), F32),
                        pltpu.VMEM((2, MLA_HEADS, tk, tq), BF16)],
        compiler_params=pltpu.CompilerParams(dimension_semantics=("parallel", "arbitrary"),
                                             vmem_limit_bytes=VMEM_LIMIT),
        name="mla_attn",
    )(fixed, qb, kb, vbt, zb)


def _merge_body(x_ref, ya_ref, yb_ref, yc_ref, ng_ref, wgate_ref, bgate_ref, wbr_ref, wout_ref, o_ref):
    x = x_ref[...]
    h = (x * _rms(x, D_MODEL) * ng_ref[...]).astype(BF16)
    merged = None
    for n, y_ref in enumerate((ya_ref, yb_ref, yc_ref)):
        cols = slice(n * D_MODEL, (n + 1) * D_MODEL)
        logits = jnp.dot(h, wgate_ref[:, cols], preferred_element_type=F32) + bgate_ref[:, cols]
        term = _sigmoid(logits) * jnp.dot(y_ref[...], wbr_ref[n], preferred_element_type=F32)
        merged = term if merged is None else merged + term
    o_ref[...] = x + jnp.dot(merged.astype(BF16), wout_ref[...], preferred_element_type=F32)


def _merge(x2, ya, yb, yc, p, layer):
    t = x2.shape[0]
    tm = MERGE_TILE
    row = lambda w: pl.BlockSpec((tm, w), lambda i: (i, 0))
    consts = [p["norm_g"], p["w_gate"], p["b_gate"], p["w_branch"], p["w_out"]]
    return pl.pallas_call(
        _merge_body,
        grid=(t // tm,),
        in_specs=[row(D_MODEL), row(BRANCH_W), row(BRANCH_W), row(BRANCH_W)]
                 + [_layer_spec(c, layer) for c in consts],
        out_specs=row(D_MODEL),
        out_shape=jax.ShapeDtypeStruct((t, D_MODEL), F32),
        compiler_params=pltpu.CompilerParams(dimension_semantics=("parallel",),
                                             vmem_limit_bytes=VMEM_LIMIT),
        name="merge",
    )(x2, ya, yb, yc, *consts)


def _head_layout(w, heads):
    lead = w.shape[:-1]
    w = w.reshape(lead + (heads, MLA_QK))
    w = jnp.pad(w, [(0, 0)] * (len(lead) + 1) + [(0, 1)])
    src = jnp.asarray([MLA_QK if s < 0 else s for s in HEAD_LANE_SOURCE], jnp.int32)
    return jnp.take(w, src, axis=-1).reshape(lead + (heads * HEAD_PAD,))


def _pack_params(norm_g, w_in, b_gate, sg_ln_g, sg_ln_b, sg_w, sg_b, mla_cq_g, mla_ckv_g, mla_w_uq,
                 mla_w_ukv, mla_q_g, mla_k_g, gla_w_gate, gla_b_gate, gla_o_g, w_branch, w_out):
    depth, d = w_in.shape[0], w_in.shape[1]
    w = w_in
    row = lambda g: g.reshape(depth, 1, -1)
    kr_block = _head_layout(
        jnp.concatenate([jnp.zeros((depth, d, MLA_NOPE), w.dtype), w[:, :, 1920:1952]], axis=2), 1)
    kr_block = kr_block.at[:, :, GATE_LANE:GATE_LANE + GLA_GATE_RANK].set(w[:, :, 3488:3504])
    w_mix = jnp.concatenate([w[:, :, :1920], kr_block, w[:, :, 1952:3488], w[:, :, 3504:GATE_SRC]],
                            axis=2)
    ukv = mla_w_ukv.reshape(depth, MLA_KV_RANK, MLA_HEADS, MLA_NOPE + MLA_V)
    w_uk = _head_layout(jnp.pad(ukv[..., :MLA_NOPE], ((0, 0), (0, 0), (0, 0), (0, MLA_ROPE)))
                        .reshape(depth, MLA_KV_RANK, -1), MLA_HEADS)
    w_uvt = jnp.swapaxes(ukv[..., MLA_NOPE:].reshape(depth, MLA_KV_RANK, -1), 1, 2)
    w_gu = jnp.zeros((depth, LANES, GLA_HEADS * GLA_DK), F32).at[
        :, GATE_LANE:GATE_LANE + GLA_GATE_RANK].set(gla_w_gate)
    q_g = row(_head_layout(mla_q_g, 1)) * (MLA_QK ** -0.5 * math.log2(math.e))
    k_g = row(_head_layout(mla_k_g, 1))
    bound = (MLA_QK * SCORE_BOUND_MARGIN) * (jnp.max(jnp.abs(q_g), axis=(1, 2), keepdims=True)
                                             * jnp.max(jnp.abs(k_g), axis=(1, 2), keepdims=True))
    fixed = bound <= FIXED_REFERENCE_MAX_BOUND
    bias_lane = (jnp.arange(HEAD_PAD) == BIAS_LANE).reshape(1, 1, HEAD_PAD)
    return {
        "norm_g": row(norm_g),
        "w_mix": w_mix.astype(BF16),
        "w_gate": w[:, :, GATE_SRC:].astype(BF16),
        "sg_ln_g": row(sg_ln_g),
        "sg_ln_b": row(sg_ln_b),
        "sg_w": sg_w,
        "sg_bt": jnp.swapaxes(sg_b, 1, 2),
        "cq_g": row(mla_cq_g),
        "ckv_g": row(mla_ckv_g),
        "w_uq": _head_layout(mla_w_uq, MLA_HEADS).astype(BF16),
        "w_uk": w_uk.astype(BF16),
        "w_uvt": w_uvt.astype(BF16),
        "q_g": q_g,
        "k_g": k_g,
        "q_pad": jnp.where(bias_lane & fixed, -bound, 0.0).astype(F32),
        "k_pad": jnp.broadcast_to(jnp.where(bias_lane, 1.0, 0.0).astype(F32), (depth, 1, HEAD_PAD)),
        "fixed": fixed.astype(jnp.int32).reshape(depth),
        "w_gu": w_gu.astype(BF16),
        "b_gu": row(gla_b_gate),
        "b_gate": row(b_gate),
        "o_g": row(gla_o_g),
        "w_branch": w_branch.astype(BF16),
        "w_out": w_out.astype(BF16),
    }


def kernel(x, positions, norm_g, w_in, b_gate, sg_ln_g, sg_ln_b, sg_w, sg_b, mla_cq_g, mla_ckv_g, mla_w_uq, mla_w_ukv, mla_q_g, mla_k_g, gla_w_gate, gla_b_gate, gla_o_g, w_branch, w_out):
    batch, seq, d = x.shape
    assert d == D_MODEL and seq % max(ATTN_Q_TILE, GLA_TILE) == 0
    assert (batch * seq) % IN_TILE == 0 and (batch * seq) % MERGE_TILE == 0
    assert batch % ROWS_PER_STEP == 0
    assert IN_TILE % ATTN_TILE == 0 and IN_TILE % SG_BLOCK == 0
    depth = w_in.shape[0]
    cos_t, sin_t = _rope_tables(positions)
    x2 = x.reshape(batch * seq, d)
    p = _pack_params(norm_g, w_in, b_gate, sg_ln_g, sg_ln_b, sg_w, sg_b, mla_cq_g, mla_ckv_g,
                     mla_w_uq, mla_w_ukv, mla_q_g, mla_k_g, gla_w_gate, gla_b_gate, gla_o_g,
                     w_branch, w_out)
    for l in range(depth):
        ya, qb, kb, vbt, zb, qc, kc, vc, la, zc = _in_proj(x2, cos_t, sin_t, p, l)
        yc = _gla(qc, kc, vc, la, zc, p["o_g"], l, batch, seq)
        yb = _attn(p["fixed"], qb, kb, vbt, zb, l, batch, seq)
        x2 = _merge(x2, ya, yb, yc, p, l)
    return x2.reshape(batch, seq, d)
```

```python
import functools
import math

import jax
import jax.numpy as jnp
from jax import lax
from jax.experimental import pallas as pl
from jax.experimental.pallas import tpu as pltpu

F32 = jnp.float32
BF16 = jnp.bfloat16

D_MODEL = 1024
CHUNK = 64
BRANCH_W = 512
N_BRANCH = 3
EPS = 1e-6
SG_BLOCK = 128
SG_GROUPS = 4
MLA_HEADS = 8
MLA_NOPE = 64
MLA_ROPE = 32
MLA_QK = MLA_NOPE + MLA_ROPE
MLA_V = 64
MLA_Q_RANK = 256
MLA_KV_RANK = 128
ROPE_THETA = 10000.0
GLA_HEADS = 4
GLA_DK = 64
GLA_DV = 128
GLA_GATE_RANK = 16
GLA_TAU = 16.0

LANES = 128
SUBLANES = 8
HEAD_PAD = LANES
ROPE_HALF = MLA_ROPE // 2
X1_LO = 0
X2_LO = LANES // 2
BIAS_LANE = MLA_QK
GATE_LANE = BIAS_LANE + 1


def _head_lane_source():
    src = [-1] * LANES
    nope_lanes = list(range(ROPE_HALF, X2_LO)) + list(range(X2_LO + ROPE_HALF, MLA_QK))
    for f, lane in enumerate(nope_lanes):
        src[lane] = f
    for f in range(ROPE_HALF):
        src[X1_LO + f] = MLA_NOPE + f
        src[X2_LO + f] = MLA_NOPE + ROPE_HALF + f
    return src


HEAD_LANE_SOURCE = _head_lane_source()

A_U, A_V, A_Z = 0, BRANCH_W, 2 * BRANCH_W
B_CQ = 3 * BRANCH_W
B_CKV = B_CQ + MLA_Q_RANK
B_KR = B_CKV + MLA_KV_RANK
B_Z = B_KR + LANES
C_Q = B_Z + BRANCH_W
C_K = C_Q + GLA_HEADS * GLA_DK
C_V = C_K + GLA_HEADS * GLA_DK
C_Z = C_V + GLA_HEADS * GLA_DV
SRC_KR = B_KR
SRC_ZB = SRC_KR + MLA_ROPE
SRC_GC = SRC_ZB + BRANCH_W + 2 * GLA_HEADS * GLA_DK + GLA_HEADS * GLA_DV
SRC_ZC = SRC_GC + GLA_GATE_RANK
GATE_SRC = SRC_ZC + BRANCH_W

IN_TILE = 1024
MERGE_TILE = 512
ATTN_TILE = 256
ATTN_Q_TILE = 2 * ATTN_TILE
GLA_TILE = 512
ROWS_PER_STEP = 2
VMEM_LIMIT = 56 * 1024 * 1024

SCORE_BOUND_MARGIN = 1.02
FIXED_REFERENCE_MAX_BOUND = 40.0

NT_DIMS = (((1,), (1,)), ((), ()))
TN_DIMS = (((0,), (0,)), ((), ()))


def _layer_spec(stacked, layer):
    tail = stacked.shape[1:]
    index = (layer,) + (0,) * len(tail)
    return pl.BlockSpec((None,) + tail, lambda *_: index, pipeline_mode=pl.Buffered(1))


def _sigmoid(x):
    return 0.5 * (jnp.tanh(0.5 * x) + 1.0)


def _silu(x):
    return x * _sigmoid(x)


def _gelu_tanh(x):
    c = math.sqrt(2.0 / math.pi)
    return 0.5 * x * (1.0 + jnp.tanh(c * (x + 0.044715 * (x * x * x))))


def _log_sigmoid(x):
    return jnp.minimum(x, 0.0) - jnp.log(1.0 + jnp.exp(-jnp.abs(x)))


def _rms(x, width):
    return lax.rsqrt(jnp.sum(x * x, axis=-1, keepdims=True) * (1.0 / width) + EPS)


def _rope_body(pos_ref, cos_ref, sin_ref):
    tm = pos_ref.shape[-1]
    fidx = lax.broadcasted_iota(jnp.int32, (ROPE_HALF, 1), 0).astype(F32)
    inv = 1.0 / jnp.exp(fidx * (2.0 / MLA_ROPE) * math.log(ROPE_THETA))
    ang = inv * pos_ref[0].astype(F32)
    cos = jnp.cos(ang)
    sin = jnp.sin(ang)
    nope_a = X2_LO - (X1_LO + ROPE_HALF)
    nope_b = MLA_QK - (X2_LO + ROPE_HALF)
    ones = lambda n: jnp.ones((n, tm), F32)
    zeros = lambda n: jnp.zeros((n, tm), F32)
    cos_t = jnp.concatenate([cos, ones(nope_a), cos, ones(nope_b), zeros(LANES - MLA_QK)], axis=0)
    sin_t = jnp.concatenate([-sin, zeros(nope_a), sin, zeros(nope_b + LANES - MLA_QK)], axis=0)
    cos_ref[...] = cos_t.T
    sin_ref[...] = sin_t.T


def _rope_tables(positions):
    t = positions.size
    tm = 1024 if t % 1024 == 0 else ATTN_TILE
    return pl.pallas_call(
        _rope_body,
        grid=(t // tm,),
        in_specs=[pl.BlockSpec((1, 1, tm), lambda i: (i, 0, 0))],
        out_specs=[pl.BlockSpec((tm, LANES), lambda i: (i, 0))] * 2,
        out_shape=[jax.ShapeDtypeStruct((t, LANES), F32)] * 2,
        name="rope_tables",
    )(positions.reshape(t // tm, 1, tm))


def _in_proj_body(x_ref, cos_ref, sin_ref, ng_ref, win_ref, lng_ref, lnb_ref, sgw_ref, sgbt_ref,
                  cqg_ref, ckvg_ref, wuq_ref, wuk_ref, wuvt_ref, qg_ref, kg_ref, qpad_ref, kpad_ref,
                  wgu_ref, bgu_ref,
                  ya_ref, qb_ref, kb_ref, vbt_ref, zb_ref, qc_ref, kc_ref, vc_ref, la_ref, zc_ref):
    tm = x_ref.shape[0]
    x = x_ref[...]
    h = (x * _rms(x, D_MODEL) * ng_ref[...]).astype(BF16)

    def proj(lo, width):
        return jnp.dot(h, win_ref[:, lo:lo + width], preferred_element_type=F32)

    u = _gelu_tanh(proj(A_U, BRANCH_W))
    v = _gelu_tanh(proj(A_V, BRANCH_W))
    mu = jnp.mean(v, axis=-1, keepdims=True)
    vc = v - mu
    vn = vc * lax.rsqrt(jnp.mean(vc * vc, axis=-1, keepdims=True) + EPS)
    vn = (vn * lng_ref[...] + lnb_ref[...]).astype(BF16)
    uz = u * _silu(proj(A_Z, BRANCH_W))
    ri = lax.broadcasted_iota(jnp.int32, (SG_BLOCK, SG_BLOCK), 0) // CHUNK
    ci = lax.broadcasted_iota(jnp.int32, (SG_BLOCK, SG_BLOCK), 1) // CHUNK
    chunk_causal = ri >= ci
    for g in range(SG_GROUPS):
        wg = jnp.where(chunk_causal, sgw_ref[g], 0.0).astype(BF16)
        bias = sgbt_ref[:, g:g + 1]
        cols = slice(g * LANES, (g + 1) * LANES)
        for n in range(tm // SG_BLOCK):
            rows = slice(n * SG_BLOCK, (n + 1) * SG_BLOCK)
            sv = jnp.dot(wg, vn[rows, cols], preferred_element_type=F32) + bias
            ya_ref[rows, cols] = (uz[rows, cols] * sv).astype(BF16)

    cq = proj(B_CQ, MLA_Q_RANK)
    cqn = (cq * _rms(cq, MLA_Q_RANK) * cqg_ref[...]).astype(BF16)
    q = jnp.dot(cqn, wuq_ref[...], preferred_element_type=F32)
    ckv = proj(B_CKV, MLA_KV_RANK)
    ckvn = (ckv * _rms(ckv, MLA_KV_RANK) * ckvg_ref[...]).astype(BF16)
    kn = jnp.dot(ckvn, wuk_ref[...], preferred_element_type=F32)
    for s in range(tm // ATTN_TILE):
        vbt_ref[s] = lax.dot_general(wuvt_ref[...], ckvn[s * ATTN_TILE:(s + 1) * ATTN_TILE], NT_DIMS,
                                     preferred_element_type=F32).astype(BF16)
    kr_glr = proj(B_KR, LANES)
    lane = lax.broadcasted_iota(jnp.int32, (1, LANES), 1)
    kr = jnp.where(lane < MLA_QK, kr_glr, 0.0)
    cos = cos_ref[...]
    sin = sin_ref[...]

    def norm_rope(xh, gain, pad_row):
        xn = xh * _rms(xh, MLA_QK) * gain
        partner = pltpu.roll(xn, LANES // 2, 1)
        return (xn * cos + partner * sin + pad_row).astype(BF16)

    for hd in range(MLA_HEADS):
        cols = slice(hd * HEAD_PAD, (hd + 1) * HEAD_PAD)
        qb_ref[:, cols] = norm_rope(q[:, cols], qg_ref[...], qpad_ref[...])
        kb_ref[:, cols] = norm_rope(kn[:, cols] + kr, kg_ref[...], kpad_ref[...])
    zb_ref[...] = _silu(proj(B_Z, BRANCH_W)).astype(BF16)

    qc_ref[...] = (proj(C_Q, GLA_HEADS * GLA_DK) * (GLA_DK ** -0.5)).astype(BF16)
    kc_ref[...] = proj(C_K, GLA_HEADS * GLA_DK).astype(BF16)
    vc_ref[...] = proj(C_V, GLA_HEADS * GLA_DV).astype(BF16)
    gl = jnp.dot(kr_glr.astype(BF16), wgu_ref[...], preferred_element_type=F32) + bgu_ref[...]
    la_ref[...] = _log_sigmoid(gl) * (1.0 / GLA_TAU)
    zc_ref[...] = _silu(proj(C_Z, BRANCH_W)).astype(BF16)


def _in_proj(x2, cos_t, sin_t, p, layer):
    t = x2.shape[0]
    tm = IN_TILE
    row = lambda w: pl.BlockSpec((tm, w), lambda i: (i, 0))
    consts = [p["norm_g"], p["w_mix"], p["sg_ln_g"], p["sg_ln_b"], p["sg_w"], p["sg_bt"],
              p["cq_g"], p["ckv_g"], p["w_uq"], p["w_uk"], p["w_uvt"], p["q_g"], p["k_g"],
              p["q_pad"], p["k_pad"], p["w_gu"], p["b_gu"]]
    out_widths = [(BRANCH_W, BF16), (MLA_HEADS * HEAD_PAD, BF16), (MLA_HEADS * HEAD_PAD, BF16),
                  None, (BRANCH_W, BF16),
                  (GLA_HEADS * GLA_DK, BF16), (GLA_HEADS * GLA_DK, BF16), (GLA_HEADS * GLA_DV, BF16),
                  (GLA_HEADS * GLA_DK, F32), (BRANCH_W, BF16)]
    per_step = tm // ATTN_TILE
    vt_spec = pl.BlockSpec((per_step, MLA_HEADS * MLA_V, ATTN_TILE), lambda i: (i, 0, 0))
    vt_shape = jax.ShapeDtypeStruct((t // ATTN_TILE, MLA_HEADS * MLA_V, ATTN_TILE), BF16)
    return pl.pallas_call(
        _in_proj_body,
        grid=(t // tm,),
        in_specs=[row(D_MODEL), row(LANES), row(LANES)] + [_layer_spec(c, layer) for c in consts],
        out_specs=[vt_spec if o is None else row(o[0]) for o in out_widths],
        out_shape=[vt_shape if o is None else jax.ShapeDtypeStruct((t, o[0]), o[1]) for o in out_widths],
        compiler_params=pltpu.CompilerParams(dimension_semantics=("parallel",),
                                             vmem_limit_bytes=VMEM_LIMIT),
        name="in_proj",
    )(x2, cos_t, sin_t, *consts)


def _gla_body(q_ref, k_ref, v_ref, la_ref, z_ref, og_ref, y_ref, state_ref):
    @pl.when(pl.program_id(1) == 0)
    def _():
        state_ref[...] = jnp.zeros_like(state_ref)

    n_rows, tc = q_ref.shape[0], q_ref.shape[1]
    kw = GLA_HEADS * GLA_DK
    stack = GLA_HEADS * CHUNK
    ri = lax.broadcasted_iota(jnp.int32, (tc, tc), 0)
    ci = lax.broadcasted_iota(jnp.int32, (tc, tc), 1)
    tri = jnp.where((ri // CHUNK == ci // CHUNK) & (ri >= ci), 1.0, 0.0).astype(BF16)
    si = lax.broadcasted_iota(jnp.int32, (stack, CHUNK), 0) % CHUNK
    sj = lax.broadcasted_iota(jnp.int32, (stack, CHUNK), 1)
    causal = si >= sj
    lane = lax.broadcasted_iota(jnp.int32, (1, kw), 1)
    cums = []
    for r in range(n_rows):
        la = la_ref[r]
        la_hi = la.astype(BF16)
        la_lo = (la - la_hi.astype(F32)).astype(BF16)
        cums.append(jnp.dot(tri, la_hi, preferred_element_type=F32)
                    + jnp.dot(tri, la_lo, preferred_element_type=F32))
    for c in range(tc // CHUNK):
        rows = slice(c * CHUNK, (c + 1) * CHUNK)
        for r in range(n_rows):
            b = cums[r][rows]
            b_last = b[CHUNK - 1:CHUNK, :]
            qt = q_ref[r, rows, :].astype(F32) * jnp.exp(b)
            kf = k_ref[r, rows, :].astype(F32)
            kt = (kf * jnp.exp(-b)).astype(BF16)
            ks = (kf * jnp.exp(b_last - b)).astype(BF16)
            dec = jnp.exp(b_last)
            q_stack = jnp.concatenate(
                [jnp.where((lane >= hd * GLA_DK) & (lane < (hd + 1) * GLA_DK), qt, 0.0).astype(BF16)
                 for hd in range(GLA_HEADS)], axis=0)
            v = v_ref[r, rows, :]
            att = lax.dot_general(q_stack, kt, NT_DIMS, preferred_element_type=F32)
            att = jnp.where(causal, att, 0.0).astype(BF16)
            st = state_ref[r]
            o_inter = lax.dot_general(q_stack, st.astype(BF16), NT_DIMS, preferred_element_type=F32)
            state_ref[r] = st * dec + lax.dot_general(v, ks, TN_DIMS, preferred_element_type=F32)
            for hd in range(GLA_HEADS):
                srows = slice(hd * CHUNK, (hd + 1) * CHUNK)
                cols = slice(hd * GLA_DV, (hd + 1) * GLA_DV)
                o = (jnp.dot(att[srows], v[:, cols], preferred_element_type=F32)
                     + o_inter[srows, cols])
                on = o * _rms(o, GLA_DV) * og_ref[...]
                y_ref[r, rows, cols] = (on * z_ref[r, rows, cols].astype(F32)).astype(BF16)


def _gla(qc, kc, vc, la, zc, o_g, layer, batch, seq):
    tc = GLA_TILE
    rps = ROWS_PER_STEP
    kw = GLA_HEADS * GLA_DK
    vw = GLA_HEADS * GLA_DV
    row = lambda w: pl.BlockSpec((rps, tc, w), lambda b, s: (b, s, 0))
    rows3 = lambda a: a.reshape(batch, seq, a.shape[-1])
    return pl.pallas_call(
        _gla_body,
        grid=(batch // rps, seq // tc),
        in_specs=[row(kw), row(kw), row(vw), row(kw), row(vw), _layer_spec(o_g, layer)],
        out_specs=row(vw),
        out_shape=jax.ShapeDtypeStruct((batch, seq, vw), BF16),
        scratch_shapes=[pltpu.VMEM((rps, vw, kw), F32)],
        compiler_params=pltpu.CompilerParams(dimension_semantics=("parallel", "arbitrary"),
                                             vmem_limit_bytes=VMEM_LIMIT),
        name="gla",
    )(rows3(qc), rows3(kc), rows3(vc), rows3(la), rows3(zc), o_g).reshape(batch * seq, vw)


def _attn_body(layer, fixed_ref, q_ref, k_ref, vt_ref, z_ref, y_ref, m_ref, l_ref, alpha_ref, acc_ref,
               s_ref, p_ref):
    tq = q_ref.shape[0]
    tk = vt_ref.shape[2]
    i = pl.program_id(1)
    ki = lax.broadcasted_iota(jnp.int32, (tk, tq), 0) // CHUNK
    qi = lax.broadcasted_iota(jnp.int32, (tk, tq), 1) // CHUNK
    visible = {"low": ki <= qi, "high": ki + tk // CHUNK <= qi}
    slabs = MLA_V // SUBLANES

    def all_sublanes(x, op):
        for shift in (4, 2, 1):
            x = op(x, pltpu.roll(x, shift, 0))
        return x

    def reduce_rows(st, op):
        parts = op(st.reshape(4, tk // SUBLANES // 4, SUBLANES, st.shape[-1]), axis=1)
        return op(parts, axis=0)

    def block_scores(hd, j, diagonal, queries=slice(None)):
        start = pl.multiple_of(j * tk, tk)
        cols = slice(hd * HEAD_PAD, (hd + 1) * HEAD_PAD)
        st = lax.dot_general(k_ref[pl.ds(start, tk), cols], q_ref[queries, cols], NT_DIMS,
                             preferred_element_type=F32)
        if diagonal is not None:
            st = jnp.where(visible[diagonal][:, queries], st, -jnp.inf)
        return st

    heads = range(MLA_HEADS)
    low, high = 2 * i, 2 * i + 1
    l_ref[...] = jnp.zeros(l_ref.shape, F32)
    acc_ref[...] = jnp.zeros(acc_ref.shape, F32)

    def scores_exp(hd, j, slot, diagonal=None, queries=slice(None)):
        st = block_scores(hd, j, diagonal, queries)
        p = jnp.exp2(st.reshape(tk // SUBLANES, SUBLANES, st.shape[-1]))
        l_ref[hd, :, queries] = l_ref[hd, :, queries] + all_sublanes(reduce_rows(p, jnp.sum), jnp.add)
        p_ref[slot, hd, :, queries] = p.reshape(st.shape).astype(BF16)

    def values_plain(hd, j, slot, queries=slice(None)):
        rows = slice(hd * MLA_V, (hd + 1) * MLA_V)
        acc_ref[rows, queries] = acc_ref[rows, queries] + jnp.dot(
            vt_ref[j, rows, :], p_ref[slot, hd, :, queries], preferred_element_type=F32)

    @pl.when(fixed_ref[layer] != 0)
    def _():
        upper = slice(tq // 2, tq)
        for hd in heads:
            scores_exp(hd, high, 0, "high", upper)
        for hd in heads:
            values_plain(hd, high, 0, upper)
        for hd in heads:
            scores_exp(hd, low, 1, "low")

        def pair(u, carry):
            for hd in heads:
                values_plain(hd, jnp.where(u == 0, low, 2 * u - 1), 1)
            for hd in heads:
                scores_exp(hd, 2 * u, 0)
            for hd in heads:
                values_plain(hd, 2 * u, 0)
            for hd in heads:
                scores_exp(hd, 2 * u + 1, 1)
            return carry

        lax.fori_loop(0, i, pair, 0)
        for hd in heads:
            values_plain(hd, jnp.where(i == 0, low, low - 1), 1)

    def softmax_running_max(hd):
        st = s_ref[hd].reshape(tk // SUBLANES, SUBLANES, tq)
        m_prev = m_ref[hd]
        m_new = jnp.maximum(m_prev, all_sublanes(reduce_rows(st, jnp.max), jnp.maximum))
        alpha = jnp.exp2(m_prev - m_new)
        p = jnp.exp2(st - m_new[None])
        l_ref[hd] = alpha * l_ref[hd] + all_sublanes(reduce_rows(p, jnp.sum), jnp.add)
        p_ref[0, hd] = p.reshape(tk, tq).astype(BF16)
        alpha_ref[hd] = alpha
        m_ref[hd] = m_new

    def values_rescaled(hd, j):
        rows = slice(hd * MLA_V, (hd + 1) * MLA_V)
        pv = jnp.dot(vt_ref[j, rows, :], p_ref[0, hd], preferred_element_type=F32)
        acc = acc_ref[rows, :].reshape(slabs, SUBLANES, tq) * alpha_ref[hd][None]
        acc_ref[rows, :] = acc.reshape(MLA_V, tq) + pv

    def round_(prev_block, next_block, diagonal=None):
        if prev_block is not None:
            for hd in heads:
                values_rescaled(hd, prev_block)
        for hd in heads:
            softmax_running_max(hd)
        if next_block is not None:
            for hd in heads:
                s_ref[hd] = block_scores(hd, next_block, diagonal)

    @pl.when(fixed_ref[layer] == 0)
    def _():
        m_ref[...] = jnp.full(m_ref.shape, -jnp.inf, F32)
        for hd in heads:
            s_ref[hd] = block_scores(hd, low, "low")
        round_(None, high, "high")

        @pl.when(i == 0)
        def _():
            round_(low, None)

        @pl.when(i > 0)
        def _():
            round_(low, 0)

            def body(t, carry):
                round_(jnp.where(t == 2, high, t - 3), t - 1)
                return carry

            lax.fori_loop(2, high, body, 0)
            round_(low - 2, None)

        for hd in heads:
            values_rescaled(hd, jnp.where(i == 0, high, low - 1))

    for hd in range(MLA_HEADS):
        rows = slice(hd * MLA_V, (hd + 1) * MLA_V)
        acc = acc_ref[rows, :].reshape(slabs, SUBLANES, tq) / l_ref[hd][None]
        acc_ref[rows, :] = acc.reshape(MLA_V, tq)
    y_ref[...] = (acc_ref[...].T * z_ref[...].astype(F32)).astype(BF16)


def _attn(fixed, qb, kb, vbt, zb, layer, batch, seq):
    tq, tk = ATTN_Q_TILE, ATTN_TILE
    nq = seq // tq
    hw = MLA_HEADS * HEAD_PAD
    stat = pltpu.VMEM((MLA_HEADS, SUBLANES, tq), F32)
    return pl.pallas_call(
        functools.partial(_attn_body, layer),
        grid=(batch, nq),
        in_specs=[pl.BlockSpec(memory_space=pltpu.SMEM),
                  pl.BlockSpec((tq, hw), lambda b, i: (b * nq + i, 0)),
                  pl.BlockSpec((seq, hw), lambda b, i: (b, 0)),
                  pl.BlockSpec((seq // tk, BRANCH_W, tk), lambda b, i: (b, 0, 0)),
                  pl.BlockSpec((tq, BRANCH_W), lambda b, i: (b * nq + i, 0))],
        out_specs=pl.BlockSpec((tq, BRANCH_W), lambda b, i: (b * nq + i, 0)),
        out_shape=jax.ShapeDtypeStruct((batch * seq, BRANCH_W), BF16),
        scratch_shapes=[stat, stat, stat,
                        pltpu.VMEM((MLA_HEADS * MLA_V, tq), F32),
                        pltpu.VMEM((MLA_HEADS, tk, tq), F32),
                        pltpu.VMEM((2, MLA_HEADS, tk, tq), BF16)],
        compiler_params=pltpu.CompilerParams(dimension_semantics=("parallel", "arbitrary"),
                                             vmem_limit_bytes=VMEM_LIMIT),
        name="mla_attn",
    )(fixed, qb, kb, vbt, zb)


def _merge_body(x_ref, ya_ref, yb_ref, yc_ref, ng_ref, wgate_ref, bgate_ref, wbr_ref, wout_ref, o_ref):
    x = x_ref[...]
    h = (x * _rms(x, D_MODEL) * ng_ref[...]).astype(BF16)
    merged = None
    for n, y_ref in enumerate((ya_ref, yb_ref, yc_ref)):
        cols = slice(n * D_MODEL, (n + 1) * D_MODEL)
        logits = jnp.dot(h, wgate_ref[:, cols], preferred_element_type=F32) + bgate_ref[:, cols]
        term = _sigmoid(logits) * jnp.dot(y_ref[...], wbr_ref[n], preferred_element_type=F32)
        merged = term if merged is None else merged + term
    o_ref[...] = x + jnp.dot(merged.astype(BF16), wout_ref[...], preferred_element_type=F32)


def _merge(x2, ya, yb, yc, p, layer):
    t = x2.shape[0]
    tm = MERGE_TILE
    row = lambda w: pl.BlockSpec((tm, w), lambda i: (i, 0))
    consts = [p["norm_g"], p["w_gate"], p["b_gate"], p["w_branch"], p["w_out"]]
    return pl.pallas_call(
        _merge_body,
        grid=(t // tm,),
        in_specs=[row(D_MODEL), row(BRANCH_W), row(BRANCH_W), row(BRANCH_W)]
                 + [_layer_spec(c, layer) for c in consts],
        out_specs=row(D_MODEL),
        out_shape=jax.ShapeDtypeStruct((t, D_MODEL), F32),
        compiler_params=pltpu.CompilerParams(dimension_semantics=("parallel",),
                                             vmem_limit_bytes=VMEM_LIMIT),
        name="merge",
    )(x2, ya, yb, yc, *consts)


def _head_layout(w, heads):
    lead = w.shape[:-1]
    w = w.reshape(lead + (heads, MLA_QK))
    w = jnp.pad(w, [(0, 0)] * (len(lead) + 1) + [(0, 1)])
    src = jnp.asarray([MLA_QK if s < 0 else s for s in HEAD_LANE_SOURCE], jnp.int32)
    return jnp.take(w, src, axis=-1).reshape(lead + (heads * HEAD_PAD,))


def _pack_params(norm_g, w_in, b_gate, sg_ln_g, sg_ln_b, sg_w, sg_b, mla_cq_g, mla_ckv_g, mla_w_uq,
                 mla_w_ukv, mla_q_g, mla_k_g, gla_w_gate, gla_b_gate, gla_o_g, w_branch, w_out):
    depth, d = w_in.shape[0], w_in.shape[1]
    w = w_in
    row = lambda g: g.reshape(depth, 1, -1)
    kr_block = _head_layout(
        jnp.concatenate([jnp.zeros((depth, d, MLA_NOPE), w.dtype), w[:, :, SRC_KR:SRC_ZB]], axis=2), 1)
    kr_block = kr_block.at[:, :, GATE_LANE:GATE_LANE + GLA_GATE_RANK].set(w[:, :, SRC_GC:SRC_ZC])
    w_mix = jnp.concatenate([w[:, :, :SRC_KR], kr_block, w[:, :, SRC_ZB:SRC_GC],
                             w[:, :, SRC_ZC:GATE_SRC]], axis=2)
    ukv = mla_w_ukv.reshape(depth, MLA_KV_RANK, MLA_HEADS, MLA_NOPE + MLA_V)
    w_uk = _head_layout(jnp.pad(ukv[..., :MLA_NOPE], ((0, 0), (0, 0), (0, 0), (0, MLA_ROPE)))
                        .reshape(depth, MLA_KV_RANK, -1), MLA_HEADS)
    w_uvt = jnp.swapaxes(ukv[..., MLA_NOPE:].reshape(depth, MLA_KV_RANK, -1), 1, 2)
    w_gu = jnp.zeros((depth, LANES, GLA_HEADS * GLA_DK), F32).at[
        :, GATE_LANE:GATE_LANE + GLA_GATE_RANK].set(gla_w_gate)
    q_g = row(_head_layout(mla_q_g, 1)) * (MLA_QK ** -0.5 * math.log2(math.e))
    k_g = row(_head_layout(mla_k_g, 1))
    bound = (MLA_QK * SCORE_BOUND_MARGIN) * (jnp.max(jnp.abs(q_g), axis=(1, 2), keepdims=True)
                                             * jnp.max(jnp.abs(k_g), axis=(1, 2), keepdims=True))
    fixed = bound <= FIXED_REFERENCE_MAX_BOUND
    bias_lane = (jnp.arange(HEAD_PAD) == BIAS_LANE).reshape(1, 1, HEAD_PAD)
    return {
        "norm_g": row(norm_g),
        "w_mix": w_mix.astype(BF16),
        "w_gate": w[:, :, GATE_SRC:].astype(BF16),
        "sg_ln_g": row(sg_ln_g),
        "sg_ln_b": row(sg_ln_b),
        "sg_w": sg_w,
        "sg_bt": jnp.swapaxes(sg_b, 1, 2),
        "cq_g": row(mla_cq_g),
        "ckv_g": row(mla_ckv_g),
        "w_uq": _head_layout(mla_w_uq, MLA_HEADS).astype(BF16),
        "w_uk": w_uk.astype(BF16),
        "w_uvt": w_uvt.astype(BF16),
        "q_g": q_g,
        "k_g": k_g,
        "q_pad": jnp.where(bias_lane & fixed, -bound, 0.0).astype(F32),
        "k_pad": jnp.broadcast_to(jnp.where(bias_lane, 1.0, 0.0).astype(F32), (depth, 1, HEAD_PAD)),
        "fixed": fixed.astype(jnp.int32).reshape(depth),
        "w_gu": w_gu.astype(BF16),
        "b_gu": row(gla_b_gate),
        "b_gate": row(b_gate),
        "o_g": row(gla_o_g),
        "w_branch": w_branch.astype(BF16),
        "w_out": w_out.astype(BF16),
    }


def kernel(x, positions, norm_g, w_in, b_gate, sg_ln_g, sg_ln_b, sg_w, sg_b, mla_cq_g, mla_ckv_g, mla_w_uq, mla_w_ukv, mla_q_g, mla_k_g, gla_w_gate, gla_b_gate, gla_o_g, w_branch, w_out):
    batch, seq, d = x.shape
    assert d == D_MODEL and seq % max(ATTN_Q_TILE, GLA_TILE) == 0
    assert (batch * seq) % IN_TILE == 0 and (batch * seq) % MERGE_TILE == 0
    assert batch % ROWS_PER_STEP == 0
    assert IN_TILE % ATTN_TILE == 0 and IN_TILE % SG_BLOCK == 0
    depth = w_in.shape[0]
    cos_t, sin_t = _rope_tables(positions)
    x2 = x.reshape(batch * seq, d)
    p = _pack_params(norm_g, w_in, b_gate, sg_ln_g, sg_ln_b, sg_w, sg_b, mla_cq_g, mla_ckv_g,
                     mla_w_uq, mla_w_ukv, mla_q_g, mla_k_g, gla_w_gate, gla_b_gate, gla_o_g,
                     w_branch, w_out)
    for l in range(depth):
        ya, qb, kb, vbt, zb, qc, kc, vc, la, zc = _in_proj(x2, cos_t, sin_t, p, l)
        yc = _gla(qc, kc, vc, la, zc, p["o_g"], l, batch, seq)
        yb = _attn(p["fixed"], qb, kb, vbt, zb, l, batch, seq)
        x2 = _merge(x2, ya, yb, yc, p, l)
    return x2.reshape(batch, seq, d)
```

```python
import functools
import math

import jax
import jax.numpy as jnp
from jax import lax
from jax.experimental import pallas as pl
from jax.experimental.pallas import tpu as pltpu

F32 = jnp.float32
BF16 = jnp.bfloat16

D_MODEL = 1024
CHUNK = 64
BRANCH_W = 512
N_BRANCH = 3
EPS = 1e-6
SG_BLOCK = 128
SG_GROUPS = 4
MLA_HEADS = 8
MLA_NOPE = 64
MLA_ROPE = 32
MLA_QK = MLA_NOPE + MLA_ROPE
MLA_V = 64
MLA_Q_RANK = 256
MLA_KV_RANK = 128
ROPE_THETA = 10000.0
GLA_HEADS = 4
GLA_DK = 64
GLA_DV = 128
GLA_GATE_RANK = 16
GLA_TAU = 16.0

LANES = 128
SUBLANES = 8
HEAD_PAD = LANES
ROPE_HALF = MLA_ROPE // 2
X1_LO = 0
X2_LO = LANES // 2
BIAS_LANE = MLA_QK
GATE_LANE = BIAS_LANE + 1


def _head_lane_source():
    src = [-1] * LANES
    nope_lanes = list(range(ROPE_HALF, X2_LO)) + list(range(X2_LO + ROPE_HALF, MLA_QK))
    for f, lane in enumerate(nope_lanes):
        src[lane] = f
    for f in range(ROPE_HALF):
        src[X1_LO + f] = MLA_NOPE + f
        src[X2_LO + f] = MLA_NOPE + ROPE_HALF + f
    return src


HEAD_LANE_SOURCE = _head_lane_source()

A_U, A_V, A_Z = 0, BRANCH_W, 2 * BRANCH_W
B_CQ = 3 * BRANCH_W
B_CKV = B_CQ + MLA_Q_RANK
B_KR = B_CKV + MLA_KV_RANK
B_Z = B_KR + LANES
C_Q = B_Z + BRANCH_W
C_K = C_Q + GLA_HEADS * GLA_DK
C_V = C_K + GLA_HEADS * GLA_DK
C_Z = C_V + GLA_HEADS * GLA_DV
SRC_KR = B_KR
SRC_ZB = SRC_KR + MLA_ROPE
SRC_GC = SRC_ZB + BRANCH_W + 2 * GLA_HEADS * GLA_DK + GLA_HEADS * GLA_DV
SRC_ZC = SRC_GC + GLA_GATE_RANK
GATE_SRC = SRC_ZC + BRANCH_W

IN_TILE = 1024
MERGE_TILE = 512
ATTN_TILE = 256
ATTN_Q_TILE = 2 * ATTN_TILE
GLA_TILE = 512
GLA_CUMSUM_SPAN = 256
ROWS_PER_STEP = 2
VMEM_LIMIT = 56 * 1024 * 1024

SCORE_BOUND_MARGIN = 1.02
FIXED_REFERENCE_MAX_BOUND = 40.0

NT_DIMS = (((1,), (1,)), ((), ()))
TN_DIMS = (((0,), (0,)), ((), ()))


def _layer_spec(stacked, layer):
    tail = stacked.shape[1:]
    index = (layer,) + (0,) * len(tail)
    return pl.BlockSpec((None,) + tail, lambda *_: index, pipeline_mode=pl.Buffered(1))


def _sigmoid(x):
    return 0.5 * (jnp.tanh(0.5 * x) + 1.0)


def _silu(x):
    return x * _sigmoid(x)


def _gelu_tanh(x):
    c = math.sqrt(2.0 / math.pi)
    return 0.5 * x * (1.0 + jnp.tanh(c * (x + 0.044715 * (x * x * x))))


def _log_sigmoid(x):
    return jnp.minimum(x, 0.0) - jnp.log(1.0 + jnp.exp(-jnp.abs(x)))


def _rms(x, width):
    return lax.rsqrt(jnp.sum(x * x, axis=-1, keepdims=True) * (1.0 / width) + EPS)


def _rope_body(pos_ref, cos_ref, sin_ref):
    tm = pos_ref.shape[-1]
    fidx = lax.broadcasted_iota(jnp.int32, (ROPE_HALF, 1), 0).astype(F32)
    inv = 1.0 / jnp.exp(fidx * (2.0 / MLA_ROPE) * math.log(ROPE_THETA))
    ang = inv * pos_ref[0].astype(F32)
    cos = jnp.cos(ang)
    sin = jnp.sin(ang)
    nope_a = X2_LO - (X1_LO + ROPE_HALF)
    nope_b = MLA_QK - (X2_LO + ROPE_HALF)
    ones = lambda n: jnp.ones((n, tm), F32)
    zeros = lambda n: jnp.zeros((n, tm), F32)
    cos_t = jnp.concatenate([cos, ones(nope_a), cos, ones(nope_b), zeros(LANES - MLA_QK)], axis=0)
    sin_t = jnp.concatenate([-sin, zeros(nope_a), sin, zeros(nope_b + LANES - MLA_QK)], axis=0)
    cos_ref[...] = cos_t.T
    sin_ref[...] = sin_t.T


def _rope_tables(positions):
    t = positions.size
    tm = 1024 if t % 1024 == 0 else ATTN_TILE
    return pl.pallas_call(
        _rope_body,
        grid=(t // tm,),
        in_specs=[pl.BlockSpec((1, 1, tm), lambda i: (i, 0, 0))],
        out_specs=[pl.BlockSpec((tm, LANES), lambda i: (i, 0))] * 2,
        out_shape=[jax.ShapeDtypeStruct((t, LANES), F32)] * 2,
        name="rope_tables",
    )(positions.reshape(t // tm, 1, tm))


def _in_proj_body(x_ref, cos_ref, sin_ref, ng_ref, win_ref, lng_ref, lnb_ref, sgw_ref, sgbt_ref,
                  cqg_ref, ckvg_ref, wuq_ref, wuk_ref, wuvt_ref, qg_ref, kg_ref, qpad_ref, kpad_ref,
                  wgu_ref, bgu_ref,
                  ya_ref, qb_ref, kb_ref, vbt_ref, zb_ref, qc_ref, kc_ref, vc_ref, la_ref, zc_ref):
    tm = x_ref.shape[0]
    x = x_ref[...]
    h = (x * _rms(x, D_MODEL) * ng_ref[...]).astype(BF16)

    def proj(lo, width):
        return jnp.dot(h, win_ref[:, lo:lo + width], preferred_element_type=F32)

    u = _gelu_tanh(proj(A_U, BRANCH_W))
    v = _gelu_tanh(proj(A_V, BRANCH_W))
    mu = jnp.mean(v, axis=-1, keepdims=True)
    vc = v - mu
    vn = vc * lax.rsqrt(jnp.mean(vc * vc, axis=-1, keepdims=True) + EPS)
    vn = (vn * lng_ref[...] + lnb_ref[...]).astype(BF16)
    uz = u * _silu(proj(A_Z, BRANCH_W))
    ri = lax.broadcasted_iota(jnp.int32, (SG_BLOCK, SG_BLOCK), 0) // CHUNK
    ci = lax.broadcasted_iota(jnp.int32, (SG_BLOCK, SG_BLOCK), 1) // CHUNK
    chunk_causal = ri >= ci
    for g in range(SG_GROUPS):
        wg = jnp.where(chunk_causal, sgw_ref[g], 0.0).astype(BF16)
        bias = sgbt_ref[:, g:g + 1]
        cols = slice(g * LANES, (g + 1) * LANES)
        for n in range(tm // SG_BLOCK):
            rows = slice(n * SG_BLOCK, (n + 1) * SG_BLOCK)
            sv = jnp.dot(wg, vn[rows, cols], preferred_element_type=F32) + bias
            ya_ref[rows, cols] = (uz[rows, cols] * sv).astype(BF16)

    cq = proj(B_CQ, MLA_Q_RANK)
    cqn = (cq * _rms(cq, MLA_Q_RANK) * cqg_ref[...]).astype(BF16)
    q = jnp.dot(cqn, wuq_ref[...], preferred_element_type=F32)
    ckv = proj(B_CKV, MLA_KV_RANK)
    ckvn = (ckv * _rms(ckv, MLA_KV_RANK) * ckvg_ref[...]).astype(BF16)
    kn = jnp.dot(ckvn, wuk_ref[...], preferred_element_type=F32)
    for s in range(tm // ATTN_TILE):
        vbt_ref[s] = lax.dot_general(wuvt_ref[...], ckvn[s * ATTN_TILE:(s + 1) * ATTN_TILE], NT_DIMS,
                                     preferred_element_type=F32).astype(BF16)
    kr_glr = proj(B_KR, LANES)
    lane = lax.broadcasted_iota(jnp.int32, (1, LANES), 1)
    kr = jnp.where(lane < MLA_QK, kr_glr, 0.0)
    cos = cos_ref[...]
    sin = sin_ref[...]

    def norm_rope(xh, gain, pad_row):
        xn = xh * _rms(xh, MLA_QK) * gain
        partner = pltpu.roll(xn, LANES // 2, 1)
        return (xn * cos + partner * sin + pad_row).astype(BF16)

    for hd in range(MLA_HEADS):
        cols = slice(hd * HEAD_PAD, (hd + 1) * HEAD_PAD)
        qb_ref[:, cols] = norm_rope(q[:, cols], qg_ref[...], qpad_ref[...])
        kb_ref[:, cols] = norm_rope(kn[:, cols] + kr, kg_ref[...], kpad_ref[...])
    zb_ref[...] = _silu(proj(B_Z, BRANCH_W)).astype(BF16)

    qc_ref[...] = (proj(C_Q, GLA_HEADS * GLA_DK) * (GLA_DK ** -0.5)).astype(BF16)
    kc_ref[...] = proj(C_K, GLA_HEADS * GLA_DK).astype(BF16)
    vc_ref[...] = proj(C_V, GLA_HEADS * GLA_DV).astype(BF16)
    gl = jnp.dot(kr_glr.astype(BF16), wgu_ref[...], preferred_element_type=F32) + bgu_ref[...]
    la_ref[...] = _log_sigmoid(gl) * (1.0 / GLA_TAU)
    zc_ref[...] = _silu(proj(C_Z, BRANCH_W)).astype(BF16)


def _in_proj(x2, cos_t, sin_t, p, layer):
    t = x2.shape[0]
    tm = IN_TILE
    row = lambda w: pl.BlockSpec((tm, w), lambda i: (i, 0))
    consts = [p["norm_g"], p["w_mix"], p["sg_ln_g"], p["sg_ln_b"], p["sg_w"], p["sg_bt"],
              p["cq_g"], p["ckv_g"], p["w_uq"], p["w_uk"], p["w_uvt"], p["q_g"], p["k_g"],
              p["q_pad"], p["k_pad"], p["w_gu"], p["b_gu"]]
    out_widths = [(BRANCH_W, BF16), (MLA_HEADS * HEAD_PAD, BF16), (MLA_HEADS * HEAD_PAD, BF16),
                  None, (BRANCH_W, BF16),
                  (GLA_HEADS * GLA_DK, BF16), (GLA_HEADS * GLA_DK, BF16), (GLA_HEADS * GLA_DV, BF16),
                  (GLA_HEADS * GLA_DK, F32), (BRANCH_W, BF16)]
    per_step = tm // ATTN_TILE
    vt_spec = pl.BlockSpec((per_step, MLA_HEADS * MLA_V, ATTN_TILE), lambda i: (i, 0, 0))
    vt_shape = jax.ShapeDtypeStruct((t // ATTN_TILE, MLA_HEADS * MLA_V, ATTN_TILE), BF16)
    return pl.pallas_call(
        _in_proj_body,
        grid=(t // tm,),
        in_specs=[row(D_MODEL), row(LANES), row(LANES)] + [_layer_spec(c, layer) for c in consts],
        out_specs=[vt_spec if o is None else row(o[0]) for o in out_widths],
        out_shape=[vt_shape if o is None else jax.ShapeDtypeStruct((t, o[0]), o[1]) for o in out_widths],
        compiler_params=pltpu.CompilerParams(dimension_semantics=("parallel",),
                                             vmem_limit_bytes=VMEM_LIMIT),
        name="in_proj",
    )(x2, cos_t, sin_t, *consts)


def _gla_body(q_ref, k_ref, v_ref, la_ref, z_ref, og_ref, y_ref, state_ref):
    @pl.when(pl.program_id(1) == 0)
    def _():
        state_ref[...] = jnp.zeros_like(state_ref)

    n_rows, tc = q_ref.shape[0], q_ref.shape[1]
    kw = GLA_HEADS * GLA_DK
    stack = GLA_HEADS * CHUNK
    span = GLA_CUMSUM_SPAN
    ri = lax.broadcasted_iota(jnp.int32, (span, span), 0)
    ci = lax.broadcasted_iota(jnp.int32, (span, span), 1)
    tri = jnp.where((ri // CHUNK == ci // CHUNK) & (ri >= ci), 1.0, 0.0).astype(BF16)
    si = lax.broadcasted_iota(jnp.int32, (stack, CHUNK), 0) % CHUNK
    sj = lax.broadcasted_iota(jnp.int32, (stack, CHUNK), 1)
    causal = si >= sj
    lane = lax.broadcasted_iota(jnp.int32, (1, kw), 1)
    cums = []
    for r in range(n_rows):
        parts = []
        for s in range(tc // span):
            la = la_ref[r, s * span:(s + 1) * span, :]
            la_hi = la.astype(BF16)
            la_lo = (la - la_hi.astype(F32)).astype(BF16)
            parts.append(jnp.dot(tri, la_hi, preferred_element_type=F32)
                         + jnp.dot(tri, la_lo, preferred_element_type=F32))
        cums.append(jnp.concatenate(parts, axis=0))
    for c in range(tc // CHUNK):
        rows = slice(c * CHUNK, (c + 1) * CHUNK)
        for r in range(n_rows):
            b = cums[r][rows]
            b_last = b[CHUNK - 1:CHUNK, :]
            qt = q_ref[r, rows, :].astype(F32) * jnp.exp(b)
            kf = k_ref[r, rows, :].astype(F32)
            kt = (kf * jnp.exp(-b)).astype(BF16)
            ks = (kf * jnp.exp(b_last - b)).astype(BF16)
            dec = jnp.exp(b_last)
            q_stack = jnp.concatenate(
                [jnp.where((lane >= hd * GLA_DK) & (lane < (hd + 1) * GLA_DK), qt, 0.0).astype(BF16)
                 for hd in range(GLA_HEADS)], axis=0)
            v = v_ref[r, rows, :]
            att = lax.dot_general(q_stack, kt, NT_DIMS, preferred_element_type=F32)
            att = jnp.where(causal, att, 0.0).astype(BF16)
            st = state_ref[r]
            o_inter = lax.dot_general(q_stack, st.astype(BF16), NT_DIMS, preferred_element_type=F32)
            state_ref[r] = st * dec + lax.dot_general(v, ks, TN_DIMS, preferred_element_type=F32)
            for hd in range(GLA_HEADS):
                srows = slice(hd * CHUNK, (hd + 1) * CHUNK)
                cols = slice(hd * GLA_DV, (hd + 1) * GLA_DV)
                o = (jnp.dot(att[srows], v[:, cols], preferred_element_type=F32)
                     + o_inter[srows, cols])
                on = o * _rms(o, GLA_DV) * og_ref[...]
                y_ref[r, rows, cols] = (on * z_ref[r, rows, cols].astype(F32)).astype(BF16)


def _gla(qc, kc, vc, la, zc, o_g, layer, batch, seq):
    tc = GLA_TILE
    rps = ROWS_PER_STEP
    kw = GLA_HEADS * GLA_DK
    vw = GLA_HEADS * GLA_DV
    row = lambda w: pl.BlockSpec((rps, tc, w), lambda b, s: (b, s, 0))
    rows3 = lambda a: a.reshape(batch, seq, a.shape[-1])
    return pl.pallas_call(
        _gla_body,
        grid=(batch // rps, seq // tc),
        in_specs=[row(kw), row(kw), row(vw), row(kw), row(vw), _layer_spec(o_g, layer)],
        out_specs=row(vw),
        out_shape=jax.ShapeDtypeStruct((batch, seq, vw), BF16),
        scratch_shapes=[pltpu.VMEM((rps, vw, kw), F32)],
        compiler_params=pltpu.CompilerParams(dimension_semantics=("parallel", "arbitrary"),
                                             vmem_limit_bytes=VMEM_LIMIT),
        name="gla",
    )(rows3(qc), rows3(kc), rows3(vc), rows3(la), rows3(zc), o_g).reshape(batch * seq, vw)


def _attn_body(layer, fixed_ref, q_ref, k_ref, vt_ref, z_ref, y_ref, m_ref, l_ref, alpha_ref, acc_ref,
               s_ref, p_ref):
    tq = q_ref.shape[0]
    tk = vt_ref.shape[2]
    i = pl.program_id(1)
    ki = lax.broadcasted_iota(jnp.int32, (tk, tq), 0) // CHUNK
    qi = lax.broadcasted_iota(jnp.int32, (tk, tq), 1) // CHUNK
    visible = {"low": ki <= qi, "high": ki + tk // CHUNK <= qi}
    slabs = MLA_V // SUBLANES

    def all_sublanes(x, op):
        for shift in (4, 2, 1):
            x = op(x, pltpu.roll(x, shift, 0))
        return x

    def reduce_rows(st, op):
        parts = op(st.reshape(4, tk // SUBLANES // 4, SUBLANES, st.shape[-1]), axis=1)
        return op(parts, axis=0)

    def block_scores(hd, j, diagonal, queries=slice(None)):
        start = pl.multiple_of(j * tk, tk)
        cols = slice(hd * HEAD_PAD, (hd + 1) * HEAD_PAD)
        st = lax.dot_general(k_ref[pl.ds(start, tk), cols], q_ref[queries, cols], NT_DIMS,
                             preferred_element_type=F32)
        if diagonal is not None:
            st = jnp.where(visible[diagonal][:, queries], st, -jnp.inf)
        return st

    heads = range(MLA_HEADS)
    low, high = 2 * i, 2 * i + 1
    l_ref[...] = jnp.zeros(l_ref.shape, F32)
    acc_ref[...] = jnp.zeros(acc_ref.shape, F32)

    def scores_exp(hd, j, slot, diagonal=None, queries=slice(None)):
        st = block_scores(hd, j, diagonal, queries)
        p = jnp.exp2(st.reshape(tk // SUBLANES, SUBLANES, st.shape[-1]))
        l_ref[hd, :, queries] = l_ref[hd, :, queries] + all_sublanes(reduce_rows(p, jnp.sum), jnp.add)
        p_ref[slot, hd, :, queries] = p.reshape(st.shape).astype(BF16)

    def values_plain(hd, j, slot, queries=slice(None)):
        rows = slice(hd * MLA_V, (hd + 1) * MLA_V)
        acc_ref[rows, queries] = acc_ref[rows, queries] + jnp.dot(
            vt_ref[j, rows, :], p_ref[slot, hd, :, queries], preferred_element_type=F32)

    @pl.when(fixed_ref[layer] != 0)
    def _():
        upper = slice(tq // 2, tq)
        for hd in heads:
            scores_exp(hd, high, 0, "high", upper)
        for hd in heads:
            values_plain(hd, high, 0, upper)
        for hd in heads:
            scores_exp(hd, low, 1, "low")

        def pair(u, carry):
            for hd in heads:
                values_plain(hd, jnp.where(u == 0, low, 2 * u - 1), 1)
            for hd in heads:
                scores_exp(hd, 2 * u, 0)
            for hd in heads:
                values_plain(hd, 2 * u, 0)
            for hd in heads:
                scores_exp(hd, 2 * u + 1, 1)
            return carry

        lax.fori_loop(0, i, pair, 0)
        for hd in heads:
            values_plain(hd, jnp.where(i == 0, low, low - 1), 1)

    def softmax_running_max(hd):
        st = s_ref[hd].reshape(tk // SUBLANES, SUBLANES, tq)
        m_prev = m_ref[hd]
        m_new = jnp.maximum(m_prev, all_sublanes(reduce_rows(st, jnp.max), jnp.maximum))
        alpha = jnp.exp2(m_prev - m_new)
        p = jnp.exp2(st - m_new[None])
        l_ref[hd] = alpha * l_ref[hd] + all_sublanes(reduce_rows(p, jnp.sum), jnp.add)
        p_ref[0, hd] = p.reshape(tk, tq).astype(BF16)
        alpha_ref[hd] = alpha
        m_ref[hd] = m_new

    def values_rescaled(hd, j):
        rows = slice(hd * MLA_V, (hd + 1) * MLA_V)
        pv = jnp.dot(vt_ref[j, rows, :], p_ref[0, hd], preferred_element_type=F32)
        acc = acc_ref[rows, :].reshape(slabs, SUBLANES, tq) * alpha_ref[hd][None]
        acc_ref[rows, :] = acc.reshape(MLA_V, tq) + pv

    def round_(prev_block, next_block, diagonal=None):
        if prev_block is not None:
            for hd in heads:
                values_rescaled(hd, prev_block)
        for hd in heads:
            softmax_running_max(hd)
        if next_block is not None:
            for hd in heads:
                s_ref[hd] = block_scores(hd, next_block, diagonal)

    @pl.when(fixed_ref[layer] == 0)
    def _():
        m_ref[...] = jnp.full(m_ref.shape, -jnp.inf, F32)
        for hd in heads:
            s_ref[hd] = block_scores(hd, low, "low")
        round_(None, high, "high")

        @pl.when(i == 0)
        def _():
            round_(low, None)

        @pl.when(i > 0)
        def _():
            round_(low, 0)

            def body(t, carry):
                round_(jnp.where(t == 2, high, t - 3), t - 1)
                return carry

            lax.fori_loop(2, high, body, 0)
            round_(low - 2, None)

        for hd in heads:
            values_rescaled(hd, jnp.where(i == 0, high, low - 1))

    for hd in range(MLA_HEADS):
        rows = slice(hd * MLA_V, (hd + 1) * MLA_V)
        acc = acc_ref[rows, :].reshape(slabs, SUBLANES, tq) / l_ref[hd][None]
        acc_ref[rows, :] = acc.reshape(MLA_V, tq)
    y_ref[...] = (acc_ref[...].T * z_ref[...].astype(F32)).astype(BF16)


def _attn(fixed, qb, kb, vbt, zb, layer, batch, seq):
    tq, tk = ATTN_Q_TILE, ATTN_TILE
    nq = seq // tq
    hw = MLA_HEADS * HEAD_PAD
    stat = pltpu.VMEM((MLA_HEADS, SUBLANES, tq), F32)
    return pl.pallas_call(
        functools.partial(_attn_body, layer),
        grid=(batch, nq),
        in_specs=[pl.BlockSpec(memory_space=pltpu.SMEM),
                  pl.BlockSpec((tq, hw), lambda b, i: (b * nq + i, 0)),
                  pl.BlockSpec((seq, hw), lambda b, i: (b, 0)),
                  pl.BlockSpec((seq // tk, BRANCH_W, tk), lambda b, i: (b, 0, 0)),
                  pl.BlockSpec((tq, BRANCH_W), lambda b, i: (b * nq + i, 0))],
        out_specs=pl.BlockSpec((tq, BRANCH_W), lambda b, i: (b * nq + i, 0)),
        out_shape=jax.ShapeDtypeStruct((batch * seq, BRANCH_W), BF16),
        scratch_shapes=[stat, stat, stat,
                        pltpu.VMEM((MLA_HEADS * MLA_V, tq), F32),
                        pltpu.VMEM((MLA_HEADS, tk, tq), F32),
                        pltpu.VMEM((2, MLA_HEADS, tk, tq), BF16)],
        compiler_params=pltpu.CompilerParams(dimension_semantics=("parallel", "arbitrary"),
                                             vmem_limit_bytes=VMEM_LIMIT),
        name="mla_attn",
    )(fixed, qb, kb, vbt, zb)


def _merge_body(x_ref, ya_ref, yb_ref, yc_ref, ng_ref, wgate_ref, bgate_ref, wbr_ref, wout_ref, o_ref):
    x = x_ref[...]
    h = (x * _rms(x, D_MODEL) * ng_ref[...]).astype(BF16)
    merged = None
    for n, y_ref in enumerate((ya_ref, yb_ref, yc_ref)):
        cols = slice(n * D_MODEL, (n + 1) * D_MODEL)
        logits = jnp.dot(h, wgate_ref[:, cols], preferred_element_type=F32) + bgate_ref[:, cols]
        term = _sigmoid(logits) * jnp.dot(y_ref[...], wbr_ref[n], preferred_element_type=F32)
        merged = term if merged is None else merged + term
    o_ref[...] = x + jnp.dot(merged.astype(BF16), wout_ref[...], preferred_element_type=F32)


def _merge(x2, ya, yb, yc, p, layer):
    t = x2.shape[0]
    tm = MERGE_TILE
    row = lambda w: pl.BlockSpec((tm, w), lambda i: (i, 0))
    consts = [p["norm_g"], p["w_gate"], p["b_gate"], p["w_branch"], p["w_out"]]
    return pl.pallas_call(
        _merge_body,
        grid=(t // tm,),
        in_specs=[row(D_MODEL), row(BRANCH_W), row(BRANCH_W), row(BRANCH_W)]
                 + [_layer_spec(c, layer) for c in consts],
        out_specs=row(D_MODEL),
        out_shape=jax.ShapeDtypeStruct((t, D_MODEL), F32),
        compiler_params=pltpu.CompilerParams(dimension_semantics=("parallel",),
                                             vmem_limit_bytes=VMEM_LIMIT),
        name="merge",
    )(x2, ya, yb, yc, *consts)


def _head_layout(w, heads):
    lead = w.shape[:-1]
    w = w.reshape(lead + (heads, MLA_QK))
    w = jnp.pad(w, [(0, 0)] * (len(lead) + 1) + [(0, 1)])
    src = jnp.asarray([MLA_QK if s < 0 else s for s in HEAD_LANE_SOURCE], jnp.int32)
    return jnp.take(w, src, axis=-1).reshape(lead + (heads * HEAD_PAD,))


def _pack_params(norm_g, w_in, b_gate, sg_ln_g, sg_ln_b, sg_w, sg_b, mla_cq_g, mla_ckv_g, mla_w_uq,
                 mla_w_ukv, mla_q_g, mla_k_g, gla_w_gate, gla_b_gate, gla_o_g, w_branch, w_out):
    depth, d = w_in.shape[0], w_in.shape[1]
    w = w_in
    row = lambda g: g.reshape(depth, 1, -1)
    kr_block = _head_layout(
        jnp.concatenate([jnp.zeros((depth, d, MLA_NOPE), w.dtype), w[:, :, SRC_KR:SRC_ZB]], axis=2), 1)
    kr_block = kr_block.at[:, :, GATE_LANE:GATE_LANE + GLA_GATE_RANK].set(w[:, :, SRC_GC:SRC_ZC])
    w_mix = jnp.concatenate([w[:, :, :SRC_KR], kr_block, w[:, :, SRC_ZB:SRC_GC],
                             w[:, :, SRC_ZC:GATE_SRC]], axis=2)
    ukv = mla_w_ukv.reshape(depth, MLA_KV_RANK, MLA_HEADS, MLA_NOPE + MLA_V)
    w_uk = _head_layout(jnp.pad(ukv[..., :MLA_NOPE], ((0, 0), (0, 0), (0, 0), (0, MLA_ROPE)))
                        .reshape(depth, MLA_KV_RANK, -1), MLA_HEADS)
    w_uvt = jnp.swapaxes(ukv[..., MLA_NOPE:].reshape(depth, MLA_KV_RANK, -1), 1, 2)
    w_gu = jnp.zeros((depth, LANES, GLA_HEADS * GLA_DK), F32).at[
        :, GATE_LANE:GATE_LANE + GLA_GATE_RANK].set(gla_w_gate)
    q_g = row(_head_layout(mla_q_g, 1)) * (MLA_QK ** -0.5 * math.log2(math.e))
    k_g = row(_head_layout(mla_k_g, 1))
    bound = (MLA_QK * SCORE_BOUND_MARGIN) * (jnp.max(jnp.abs(q_g), axis=(1, 2), keepdims=True)
                                             * jnp.max(jnp.abs(k_g), axis=(1, 2), keepdims=True))
    fixed = bound <= FIXED_REFERENCE_MAX_BOUND
    bias_lane = (jnp.arange(HEAD_PAD) == BIAS_LANE).reshape(1, 1, HEAD_PAD)
    return {
        "norm_g": row(norm_g),
        "w_mix": w_mix.astype(BF16),
        "w_gate": w[:, :, GATE_SRC:].astype(BF16),
        "sg_ln_g": row(sg_ln_g),
        "sg_ln_b": row(sg_ln_b),
        "sg_w": sg_w,
        "sg_bt": jnp.swapaxes(sg_b, 1, 2),
        "cq_g": row(mla_cq_g),
        "ckv_g": row(mla_ckv_g),
        "w_uq": _head_layout(mla_w_uq, MLA_HEADS).astype(BF16),
        "w_uk": w_uk.astype(BF16),
        "w_uvt": w_uvt.astype(BF16),
        "q_g": q_g,
        "k_g": k_g,
        "q_pad": jnp.where(bias_lane & fixed, -bound, 0.0).astype(F32),
        "k_pad": jnp.broadcast_to(jnp.where(bias_lane, 1.0, 0.0).astype(F32), (depth, 1, HEAD_PAD)),
        "fixed": fixed.astype(jnp.int32).reshape(depth),
        "w_gu": w_gu.astype(BF16),
        "b_gu": row(gla_b_gate),
        "b_gate": row(b_gate),
        "o_g": row(gla_o_g),
        "w_branch": w_branch.astype(BF16),
        "w_out": w_out.astype(BF16),
    }


def kernel(x, positions, norm_g, w_in, b_gate, sg_ln_g, sg_ln_b, sg_w, sg_b, mla_cq_g, mla_ckv_g, mla_w_uq, mla_w_ukv, mla_q_g, mla_k_g, gla_w_gate, gla_b_gate, gla_o_g, w_branch, w_out):
    batch, seq, d = x.shape
    assert d == D_MODEL and seq % max(ATTN_Q_TILE, GLA_TILE) == 0
    assert (batch * seq) % IN_TILE == 0 and (batch * seq) % MERGE_TILE == 0
    assert batch % ROWS_PER_STEP == 0
    assert IN_TILE % ATTN_TILE == 0 and IN_TILE % SG_BLOCK == 0
    depth = w_in.shape[0]
    cos_t, sin_t = _rope_tables(positions)
    x2 = x.reshape(batch * seq, d)
    p = _pack_params(norm_g, w_in, b_gate, sg_ln_g, sg_ln_b, sg_w, sg_b, mla_cq_g, mla_ckv_g,
                     mla_w_uq, mla_w_ukv, mla_q_g, mla_k_g, gla_w_gate, gla_b_gate, gla_o_g,
                     w_branch, w_out)
    for l in range(depth):
        ya, qb, kb, vbt, zb, qc, kc, vc, la, zc = _in_proj(x2, cos_t, sin_t, p, l)
        yc = _gla(qc, kc, vc, la, zc, p["o_g"], l, batch, seq)
        yb = _attn(p["fixed"], qb, kb, vbt, zb, l, batch, seq)
        x2 = _merge(x2, ya, yb, yc, p, l)
    return x2.reshape(batch, seq, d)
```

```python
import functools
import math

import jax
import jax.numpy as jnp
from jax import lax
from jax.experimental import pallas as pl
from jax.experimental.pallas import tpu as pltpu

F32 = jnp.float32
BF16 = jnp.bfloat16

D_MODEL = 1024
CHUNK = 64
BRANCH_W = 512
EPS = 1e-6
SG_BLOCK = 128
SG_GROUPS = 4
MLA_HEADS = 8
MLA_NOPE = 64
MLA_ROPE = 32
MLA_QK = MLA_NOPE + MLA_ROPE
MLA_V = 64
MLA_Q_RANK = 256
MLA_KV_RANK = 128
ROPE_THETA = 10000.0
GLA_HEADS = 4
GLA_DK = 64
GLA_DV = 128
GLA_GATE_RANK = 16
GLA_TAU = 16.0

LANES = 128
SUBLANES = 8
HEAD_PAD = LANES
ROPE_HALF = MLA_ROPE // 2
X1_LO = 0
X2_LO = LANES // 2
BIAS_LANE = MLA_QK
GATE_LANE = BIAS_LANE + 1


def _head_lane_source():
    src = [-1] * LANES
    nope_lanes = list(range(ROPE_HALF, X2_LO)) + list(range(X2_LO + ROPE_HALF, MLA_QK))
    for f, lane in enumerate(nope_lanes):
        src[lane] = f
    for f in range(ROPE_HALF):
        src[X1_LO + f] = MLA_NOPE + f
        src[X2_LO + f] = MLA_NOPE + ROPE_HALF + f
    return src


HEAD_LANE_SOURCE = _head_lane_source()

A_U, A_V, A_Z = 0, BRANCH_W, 2 * BRANCH_W
B_CQ = 3 * BRANCH_W
B_CKV = B_CQ + MLA_Q_RANK
B_KR = B_CKV + MLA_KV_RANK
B_Z = B_KR + LANES
C_Q = B_Z + BRANCH_W
C_K = C_Q + GLA_HEADS * GLA_DK
C_V = C_K + GLA_HEADS * GLA_DK
C_Z = C_V + GLA_HEADS * GLA_DV
SRC_KR = B_KR
SRC_ZB = SRC_KR + MLA_ROPE
SRC_GC = SRC_ZB + BRANCH_W + 2 * GLA_HEADS * GLA_DK + GLA_HEADS * GLA_DV
SRC_ZC = SRC_GC + GLA_GATE_RANK
GATE_SRC = SRC_ZC + BRANCH_W

IN_TILE = 1024
MERGE_TILE = 512
ATTN_TILE = 256
ATTN_Q_TILE = 2 * ATTN_TILE
GLA_TILE = 512
GLA_CUMSUM_SPAN = 256
ROWS_PER_STEP = 4
VMEM_LIMIT = 56 * 1024 * 1024

SCORE_BOUND_MARGIN = 1.02
FIXED_REFERENCE_MAX_BOUND = 40.0

NT_DIMS = (((1,), (1,)), ((), ()))
TN_DIMS = (((0,), (0,)), ((), ()))


def _layer_spec(stacked, layer):
    tail = stacked.shape[1:]
    index = (layer,) + (0,) * len(tail)
    return pl.BlockSpec((None,) + tail, lambda *_: index, pipeline_mode=pl.Buffered(1))


def _sigmoid(x):
    return 0.5 * (jnp.tanh(0.5 * x) + 1.0)


def _silu(x):
    return x * _sigmoid(x)


def _gelu_tanh(x):
    c = math.sqrt(2.0 / math.pi)
    return 0.5 * x * (1.0 + jnp.tanh(c * (x + 0.044715 * (x * x * x))))


def _log_sigmoid(x):
    return jnp.minimum(x, 0.0) - jnp.log(1.0 + jnp.exp(-jnp.abs(x)))


def _rms(x, width):
    return lax.rsqrt(jnp.sum(x * x, axis=-1, keepdims=True) * (1.0 / width) + EPS)


def _rope_body(pos_ref, cos_ref, sin_ref):
    tm = pos_ref.shape[-1]
    fidx = lax.broadcasted_iota(jnp.int32, (ROPE_HALF, 1), 0).astype(F32)
    inv = 1.0 / jnp.exp(fidx * (2.0 / MLA_ROPE) * math.log(ROPE_THETA))
    ang = inv * pos_ref[0].astype(F32)
    cos = jnp.cos(ang)
    sin = jnp.sin(ang)
    nope_a = X2_LO - (X1_LO + ROPE_HALF)
    nope_b = MLA_QK - (X2_LO + ROPE_HALF)
    ones = lambda n: jnp.ones((n, tm), F32)
    zeros = lambda n: jnp.zeros((n, tm), F32)
    cos_t = jnp.concatenate([cos, ones(nope_a), cos, ones(nope_b), zeros(LANES - MLA_QK)], axis=0)
    sin_t = jnp.concatenate([-sin, zeros(nope_a), sin, zeros(nope_b + LANES - MLA_QK)], axis=0)
    cos_ref[...] = cos_t.T
    sin_ref[...] = sin_t.T


def _rope_tables(positions):
    t = positions.size
    tm = 1024 if t % 1024 == 0 else ATTN_TILE
    return pl.pallas_call(
        _rope_body,
        grid=(t // tm,),
        in_specs=[pl.BlockSpec((1, 1, tm), lambda i: (i, 0, 0))],
        out_specs=[pl.BlockSpec((tm, LANES), lambda i: (i, 0))] * 2,
        out_shape=[jax.ShapeDtypeStruct((t, LANES), F32)] * 2,
        name="rope_tables",
    )(positions.reshape(t // tm, 1, tm))


def _in_proj_body(x_ref, cos_ref, sin_ref, ng_ref, win_ref, lng_ref, lnb_ref, sgw_ref, sgbt_ref,
                  cqg_ref, ckvg_ref, wuq_ref, wuk_ref, wuvt_ref, qg_ref, kg_ref, qpad_ref, kpad_ref,
                  wgu_ref, bgu_ref,
                  ya_ref, qb_ref, kb_ref, vbt_ref, zb_ref, qc_ref, kc_ref, vc_ref, la_ref, zc_ref):
    tm = x_ref.shape[0]
    x = x_ref[...]
    h = (x * _rms(x, D_MODEL) * ng_ref[...]).astype(BF16)

    def proj(lo, width):
        return jnp.dot(h, win_ref[:, lo:lo + width], preferred_element_type=F32)

    u = _gelu_tanh(proj(A_U, BRANCH_W))
    v = _gelu_tanh(proj(A_V, BRANCH_W))
    mu = jnp.mean(v, axis=-1, keepdims=True)
    vc = v - mu
    vn = vc * lax.rsqrt(jnp.mean(vc * vc, axis=-1, keepdims=True) + EPS)
    vn = (vn * lng_ref[...] + lnb_ref[...]).astype(BF16)
    uz = u * _silu(proj(A_Z, BRANCH_W))
    ri = lax.broadcasted_iota(jnp.int32, (SG_BLOCK, SG_BLOCK), 0) // CHUNK
    ci = lax.broadcasted_iota(jnp.int32, (SG_BLOCK, SG_BLOCK), 1) // CHUNK
    chunk_causal = ri >= ci
    for g in range(SG_GROUPS):
        wg = jnp.where(chunk_causal, sgw_ref[g], 0.0).astype(BF16)
        bias = sgbt_ref[:, g:g + 1]
        cols = slice(g * LANES, (g + 1) * LANES)
        for n in range(tm // SG_BLOCK):
            rows = slice(n * SG_BLOCK, (n + 1) * SG_BLOCK)
            sv = jnp.dot(wg, vn[rows, cols], preferred_element_type=F32) + bias
            ya_ref[rows, cols] = (uz[rows, cols] * sv).astype(BF16)

    cq = proj(B_CQ, MLA_Q_RANK)
    cqn = (cq * _rms(cq, MLA_Q_RANK) * cqg_ref[...]).astype(BF16)
    q = jnp.dot(cqn, wuq_ref[...], preferred_element_type=F32)
    ckv = proj(B_CKV, MLA_KV_RANK)
    ckvn = (ckv * _rms(ckv, MLA_KV_RANK) * ckvg_ref[...]).astype(BF16)
    kn = jnp.dot(ckvn, wuk_ref[...], preferred_element_type=F32)
    for s in range(tm // ATTN_TILE):
        vbt_ref[s] = lax.dot_general(wuvt_ref[...], ckvn[s * ATTN_TILE:(s + 1) * ATTN_TILE], NT_DIMS,
                                     preferred_element_type=F32).astype(BF16)
    kr_glr = proj(B_KR, LANES)
    lane = lax.broadcasted_iota(jnp.int32, (1, LANES), 1)
    kr = jnp.where(lane < MLA_QK, kr_glr, 0.0)
    cos = cos_ref[...]
    sin = sin_ref[...]

    def norm_rope(xh, gain, pad_row):
        xn = xh * _rms(xh, MLA_QK) * gain
        partner = pltpu.roll(xn, LANES // 2, 1)
        return (xn * cos + partner * sin + pad_row).astype(BF16)

    for hd in range(MLA_HEADS):
        cols = slice(hd * HEAD_PAD, (hd + 1) * HEAD_PAD)
        qb_ref[:, cols] = norm_rope(q[:, cols], qg_ref[...], qpad_ref[...])
        kb_ref[:, cols] = norm_rope(kn[:, cols] + kr, kg_ref[...], kpad_ref[...])
    zb_ref[...] = _silu(proj(B_Z, BRANCH_W)).astype(BF16)

    qc_ref[...] = (proj(C_Q, GLA_HEADS * GLA_DK) * (GLA_DK ** -0.5)).astype(BF16)
    kc_ref[...] = proj(C_K, GLA_HEADS * GLA_DK).astype(BF16)
    vc_ref[...] = proj(C_V, GLA_HEADS * GLA_DV).astype(BF16)
    gl = jnp.dot(kr_glr.astype(BF16), wgu_ref[...], preferred_element_type=F32) + bgu_ref[...]
    la_ref[...] = _log_sigmoid(gl) * (1.0 / GLA_TAU)
    zc_ref[...] = _silu(proj(C_Z, BRANCH_W)).astype(BF16)


def _in_proj(x2, cos_t, sin_t, p, layer):
    t = x2.shape[0]
    tm = IN_TILE
    row = lambda w: pl.BlockSpec((tm, w), lambda i: (i, 0))
    consts = [p["norm_g"], p["w_mix"], p["sg_ln_g"], p["sg_ln_b"], p["sg_w"], p["sg_bt"],
              p["cq_g"], p["ckv_g"], p["w_uq"], p["w_uk"], p["w_uvt"], p["q_g"], p["k_g"],
              p["q_pad"], p["k_pad"], p["w_gu"], p["b_gu"]]
    out_widths = [(BRANCH_W, BF16), (MLA_HEADS * HEAD_PAD, BF16), (MLA_HEADS * HEAD_PAD, BF16),
                  None, (BRANCH_W, BF16),
                  (GLA_HEADS * GLA_DK, BF16), (GLA_HEADS * GLA_DK, BF16), (GLA_HEADS * GLA_DV, BF16),
                  (GLA_HEADS * GLA_DK, F32), (BRANCH_W, BF16)]
    per_step = tm // ATTN_TILE
    vt_spec = pl.BlockSpec((per_step, MLA_HEADS * MLA_V, ATTN_TILE), lambda i: (i, 0, 0))
    vt_shape = jax.ShapeDtypeStruct((t // ATTN_TILE, MLA_HEADS * MLA_V, ATTN_TILE), BF16)
    return pl.pallas_call(
        _in_proj_body,
        grid=(t // tm,),
        in_specs=[row(D_MODEL), row(LANES), row(LANES)] + [_layer_spec(c, layer) for c in consts],
        out_specs=[vt_spec if o is None else row(o[0]) for o in out_widths],
        out_shape=[vt_shape if o is None else jax.ShapeDtypeStruct((t, o[0]), o[1]) for o in out_widths],
        compiler_params=pltpu.CompilerParams(dimension_semantics=("parallel",),
                                             vmem_limit_bytes=VMEM_LIMIT),
        name="in_proj",
    )(x2, cos_t, sin_t, *consts)


def _gla_body(q_ref, k_ref, v_ref, la_ref, z_ref, og_ref, y_ref, state_ref):
    @pl.when(pl.program_id(1) == 0)
    def _():
        state_ref[...] = jnp.zeros_like(state_ref)

    n_rows, tc = q_ref.shape[0], q_ref.shape[1]
    kw = GLA_HEADS * GLA_DK
    stack = GLA_HEADS * CHUNK
    span = GLA_CUMSUM_SPAN
    ri = lax.broadcasted_iota(jnp.int32, (span, span), 0)
    ci = lax.broadcasted_iota(jnp.int32, (span, span), 1)
    tri = jnp.where((ri // CHUNK == ci // CHUNK) & (ri >= ci), 1.0, 0.0).astype(BF16)
    si = lax.broadcasted_iota(jnp.int32, (stack, CHUNK), 0) % CHUNK
    sj = lax.broadcasted_iota(jnp.int32, (stack, CHUNK), 1)
    causal = si >= sj
    lane = lax.broadcasted_iota(jnp.int32, (1, kw), 1)
    cums = []
    for r in range(n_rows):
        parts = []
        for s in range(tc // span):
            la = la_ref[r, s * span:(s + 1) * span, :]
            la_hi = la.astype(BF16)
            la_lo = (la - la_hi.astype(F32)).astype(BF16)
            parts.append(jnp.dot(tri, la_hi, preferred_element_type=F32)
                         + jnp.dot(tri, la_lo, preferred_element_type=F32))
        cums.append(jnp.concatenate(parts, axis=0))
    for c in range(tc // CHUNK):
        rows = slice(c * CHUNK, (c + 1) * CHUNK)
        for r in range(n_rows):
            b = cums[r][rows]
            b_last = b[CHUNK - 1:CHUNK, :]
            qt = q_ref[r, rows, :].astype(F32) * jnp.exp(b)
            kf = k_ref[r, rows, :].astype(F32)
            kt = (kf * jnp.exp(-b)).astype(BF16)
            ks = (kf * jnp.exp(b_last - b)).astype(BF16)
            dec = jnp.exp(b_last)
            q_stack = jnp.concatenate(
                [jnp.where((lane >= hd * GLA_DK) & (lane < (hd + 1) * GLA_DK), qt, 0.0).astype(BF16)
                 for hd in range(GLA_HEADS)], axis=0)
            v = v_ref[r, rows, :]
            att = lax.dot_general(q_stack, kt, NT_DIMS, preferred_element_type=F32)
            att = jnp.where(causal, att, 0.0).astype(BF16)
            st = state_ref[r]
            o_inter = lax.dot_general(q_stack, st.astype(BF16), NT_DIMS, preferred_element_type=F32)
            state_ref[r] = st * dec + lax.dot_general(v, ks, TN_DIMS, preferred_element_type=F32)
            for hd in range(GLA_HEADS):
                srows = slice(hd * CHUNK, (hd + 1) * CHUNK)
                cols = slice(hd * GLA_DV, (hd + 1) * GLA_DV)
                o = (jnp.dot(att[srows], v[:, cols], preferred_element_type=F32)
                     + o_inter[srows, cols])
                on = o * _rms(o, GLA_DV) * og_ref[...]
                y_ref[r, rows, cols] = (on * z_ref[r, rows, cols].astype(F32)).astype(BF16)


def _gla(qc, kc, vc, la, zc, o_g, layer, batch, seq):
    tc = GLA_TILE
    rps = ROWS_PER_STEP
    kw = GLA_HEADS * GLA_DK
    vw = GLA_HEADS * GLA_DV
    row = lambda w: pl.BlockSpec((rps, tc, w), lambda b, s: (b, s, 0))
    rows3 = lambda a: a.reshape(batch, seq, a.shape[-1])
    return pl.pallas_call(
        _gla_body,
        grid=(batch // rps, seq // tc),
        in_specs=[row(kw), row(kw), row(vw), row(kw), row(vw), _layer_spec(o_g, layer)],
        out_specs=row(vw),
        out_shape=jax.ShapeDtypeStruct((batch, seq, vw), BF16),
        scratch_shapes=[pltpu.VMEM((rps, vw, kw), F32)],
        compiler_params=pltpu.CompilerParams(dimension_semantics=("parallel", "arbitrary"),
                                             vmem_limit_bytes=VMEM_LIMIT),
        name="gla",
    )(rows3(qc), rows3(kc), rows3(vc), rows3(la), rows3(zc), o_g).reshape(batch * seq, vw)


def _attn_body(layer, fixed_ref, q_ref, k_ref, vt_ref, z_ref, y_ref, m_ref, l_ref, alpha_ref, acc_ref,
               s_ref, p_ref):
    tq = q_ref.shape[0]
    tk = vt_ref.shape[2]
    i = pl.program_id(1)
    ki = lax.broadcasted_iota(jnp.int32, (tk, tq), 0) // CHUNK
    qi = lax.broadcasted_iota(jnp.int32, (tk, tq), 1) // CHUNK
    visible = {"low": ki <= qi, "high": ki + tk // CHUNK <= qi}
    slabs = MLA_V // SUBLANES

    def all_sublanes(x, op):
        for shift in (4, 2, 1):
            x = op(x, pltpu.roll(x, shift, 0))
        return x

    def reduce_rows(st, op):
        parts = op(st.reshape(4, tk // SUBLANES // 4, SUBLANES, st.shape[-1]), axis=1)
        return op(parts, axis=0)

    def block_scores(hd, j, diagonal, queries=slice(None)):
        start = pl.multiple_of(j * tk, tk)
        cols = slice(hd * HEAD_PAD, (hd + 1) * HEAD_PAD)
        st = lax.dot_general(k_ref[pl.ds(start, tk), cols], q_ref[queries, cols], NT_DIMS,
                             preferred_element_type=F32)
        if diagonal is not None:
            st = jnp.where(visible[diagonal][:, queries], st, -jnp.inf)
        return st

    heads = range(MLA_HEADS)
    low, high = 2 * i, 2 * i + 1
    l_ref[...] = jnp.zeros(l_ref.shape, F32)
    acc_ref[...] = jnp.zeros(acc_ref.shape, F32)

    def scores_exp(hd, j, slot, diagonal=None, queries=slice(None)):
        st = block_scores(hd, j, diagonal, queries)
        p = jnp.exp2(st.reshape(tk // SUBLANES, SUBLANES, st.shape[-1]))
        l_ref[hd, :, queries] = l_ref[hd, :, queries] + all_sublanes(reduce_rows(p, jnp.sum), jnp.add)
        p_ref[slot, hd, :, queries] = p.reshape(st.shape).astype(BF16)

    def values_plain(hd, j, slot, queries=slice(None)):
        rows = slice(hd * MLA_V, (hd + 1) * MLA_V)
        acc_ref[rows, queries] = acc_ref[rows, queries] + jnp.dot(
            vt_ref[j, rows, :], p_ref[slot, hd, :, queries], preferred_element_type=F32)

    @pl.when(fixed_ref[layer] != 0)
    def _():
        upper = slice(tq // 2, tq)
        for hd in heads:
            scores_exp(hd, high, 0, "high", upper)
        for hd in heads:
            values_plain(hd, high, 0, upper)
        for hd in heads:
            scores_exp(hd, low, 1, "low")

        def pair(u, carry):
            for hd in heads:
                values_plain(hd, jnp.where(u == 0, low, 2 * u - 1), 1)
            for hd in heads:
                scores_exp(hd, 2 * u, 0)
            for hd in heads:
                values_plain(hd, 2 * u, 0)
            for hd in heads:
                scores_exp(hd, 2 * u + 1, 1)
            return carry

        lax.fori_loop(0, i, pair, 0)
        for hd in heads:
            values_plain(hd, jnp.where(i == 0, low, low - 1), 1)

    def softmax_running_max(hd):
        st = s_ref[hd].reshape(tk // SUBLANES, SUBLANES, tq)
        m_prev = m_ref[hd]
        m_new = jnp.maximum(m_prev, all_sublanes(reduce_rows(st, jnp.max), jnp.maximum))
        alpha = jnp.exp2(m_prev - m_new)
        p = jnp.exp2(st - m_new[None])
        l_ref[hd] = alpha * l_ref[hd] + all_sublanes(reduce_rows(p, jnp.sum), jnp.add)
        p_ref[0, hd] = p.reshape(tk, tq).astype(BF16)
        alpha_ref[hd] = alpha
        m_ref[hd] = m_new

    def values_rescaled(hd, j):
        rows = slice(hd * MLA_V, (hd + 1) * MLA_V)
        pv = jnp.dot(vt_ref[j, rows, :], p_ref[0, hd], preferred_element_type=F32)
        acc = acc_ref[rows, :].reshape(slabs, SUBLANES, tq) * alpha_ref[hd][None]
        acc_ref[rows, :] = acc.reshape(MLA_V, tq) + pv

    def round_(prev_block, next_block, diagonal=None):
        if prev_block is not None:
            for hd in heads:
                values_rescaled(hd, prev_block)
        for hd in heads:
            softmax_running_max(hd)
        if next_block is not None:
            for hd in heads:
                s_ref[hd] = block_scores(hd, next_block, diagonal)

    @pl.when(fixed_ref[layer] == 0)
    def _():
        m_ref[...] = jnp.full(m_ref.shape, -jnp.inf, F32)
        for hd in heads:
            s_ref[hd] = block_scores(hd, low, "low")
        round_(None, high, "high")

        @pl.when(i == 0)
        def _():
            round_(low, None)

        @pl.when(i > 0)
        def _():
            round_(low, 0)

            def body(t, carry):
                round_(jnp.where(t == 2, high, t - 3), t - 1)
                return carry

            lax.fori_loop(2, high, body, 0)
            round_(low - 2, None)

        for hd in heads:
            values_rescaled(hd, jnp.where(i == 0, high, low - 1))

    for hd in range(MLA_HEADS):
        rows = slice(hd * MLA_V, (hd + 1) * MLA_V)
        acc = acc_ref[rows, :].reshape(slabs, SUBLANES, tq) / l_ref[hd][None]
        acc_ref[rows, :] = acc.reshape(MLA_V, tq)
    y_ref[...] = (acc_ref[...].T * z_ref[...].astype(F32)).astype(BF16)


def _attn(fixed, qb, kb, vbt, zb, layer, batch, seq):
    tq, tk = ATTN_Q_TILE, ATTN_TILE
    nq = seq // tq
    hw = MLA_HEADS * HEAD_PAD
    stat = pltpu.VMEM((MLA_HEADS, SUBLANES, tq), F32)
    return pl.pallas_call(
        functools.partial(_attn_body, layer),
        grid=(batch, nq),
        in_specs=[pl.BlockSpec(memory_space=pltpu.SMEM),
                  pl.BlockSpec((tq, hw), lambda b, i: (b * nq + i, 0)),
                  pl.BlockSpec((seq, hw), lambda b, i: (b, 0)),
                  pl.BlockSpec((seq // tk, BRANCH_W, tk), lambda b, i: (b, 0, 0)),
                  pl.BlockSpec((tq, BRANCH_W), lambda b, i: (b * nq + i, 0))],
        out_specs=pl.BlockSpec((tq, BRANCH_W), lambda b, i: (b * nq + i, 0)),
        out_shape=jax.ShapeDtypeStruct((batch * seq, BRANCH_W), BF16),
        scratch_shapes=[stat, stat, stat,
                        pltpu.VMEM((MLA_HEADS * MLA_V, tq), F32),
                        pltpu.VMEM((MLA_HEADS, tk, tq), F32),
                        pltpu.VMEM((2, MLA_HEADS, tk, tq), BF16)],
        compiler_params=pltpu.CompilerParams(dimension_semantics=("parallel", "arbitrary"),
                                             vmem_limit_bytes=VMEM_LIMIT),
        name="mla_attn",
    )(fixed, qb, kb, vbt, zb)


def _merge_body(x_ref, ya_ref, yb_ref, yc_ref, ng_ref, wgate_ref, bgate_ref, wbr_ref, wout_ref, o_ref):
    x = x_ref[...]
    h = (x * _rms(x, D_MODEL) * ng_ref[...]).astype(BF16)
    merged = None
    for n, y_ref in enumerate((ya_ref, yb_ref, yc_ref)):
        cols = slice(n * D_MODEL, (n + 1) * D_MODEL)
        logits = jnp.dot(h, wgate_ref[:, cols], preferred_element_type=F32) + bgate_ref[:, cols]
        term = _sigmoid(logits) * jnp.dot(y_ref[...], wbr_ref[n], preferred_element_type=F32)
        merged = term if merged is None else merged + term
    o_ref[...] = x + jnp.dot(merged.astype(BF16), wout_ref[...], preferred_element_type=F32)


def _merge(x2, ya, yb, yc, p, layer):
    t = x2.shape[0]
    tm = MERGE_TILE
    row = lambda w: pl.BlockSpec((tm, w), lambda i: (i, 0))
    consts = [p["norm_g"], p["w_gate"], p["b_gate"], p["w_branch"], p["w_out"]]
    return pl.pallas_call(
        _merge_body,
        grid=(t // tm,),
        in_specs=[row(D_MODEL), row(BRANCH_W), row(BRANCH_W), row(BRANCH_W)]
                 + [_layer_spec(c, layer) for c in consts],
        out_specs=row(D_MODEL),
        out_shape=jax.ShapeDtypeStruct((t, D_MODEL), F32),
        compiler_params=pltpu.CompilerParams(dimension_semantics=("parallel",),
                                             vmem_limit_bytes=VMEM_LIMIT),
        name="merge",
    )(x2, ya, yb, yc, *consts)


def _head_layout(w, heads):
    lead = w.shape[:-1]
    w = w.reshape(lead + (heads, MLA_QK))
    w = jnp.pad(w, [(0, 0)] * (len(lead) + 1) + [(0, 1)])
    src = jnp.asarray([MLA_QK if s < 0 else s for s in HEAD_LANE_SOURCE], jnp.int32)
    return jnp.take(w, src, axis=-1).reshape(lead + (heads * HEAD_PAD,))


def _pack_params(norm_g, w_in, b_gate, sg_ln_g, sg_ln_b, sg_w, sg_b, mla_cq_g, mla_ckv_g, mla_w_uq,
                 mla_w_ukv, mla_q_g, mla_k_g, gla_w_gate, gla_b_gate, gla_o_g, w_branch, w_out):
    depth, d = w_in.shape[0], w_in.shape[1]
    w = w_in
    row = lambda g: g.reshape(depth, 1, -1)
    kr_block = _head_layout(
        jnp.concatenate([jnp.zeros((depth, d, MLA_NOPE), w.dtype), w[:, :, SRC_KR:SRC_ZB]], axis=2), 1)
    kr_block = kr_block.at[:, :, GATE_LANE:GATE_LANE + GLA_GATE_RANK].set(w[:, :, SRC_GC:SRC_ZC])
    w_mix = jnp.concatenate([w[:, :, :SRC_KR], kr_block, w[:, :, SRC_ZB:SRC_GC],
                             w[:, :, SRC_ZC:GATE_SRC]], axis=2)
    ukv = mla_w_ukv.reshape(depth, MLA_KV_RANK, MLA_HEADS, MLA_NOPE + MLA_V)
    w_uk = _head_layout(jnp.pad(ukv[..., :MLA_NOPE], ((0, 0), (0, 0), (0, 0), (0, MLA_ROPE)))
                        .reshape(depth, MLA_KV_RANK, -1), MLA_HEADS)
    w_uvt = jnp.swapaxes(ukv[..., MLA_NOPE:].reshape(depth, MLA_KV_RANK, -1), 1, 2)
    w_gu = jnp.zeros((depth, LANES, GLA_HEADS * GLA_DK), F32).at[
        :, GATE_LANE:GATE_LANE + GLA_GATE_RANK].set(gla_w_gate)
    q_g = row(_head_layout(mla_q_g, 1)) * (MLA_QK ** -0.5 * math.log2(math.e))
    k_g = row(_head_layout(mla_k_g, 1))
    bound = (MLA_QK * SCORE_BOUND_MARGIN) * (jnp.max(jnp.abs(q_g), axis=(1, 2), keepdims=True)
                                             * jnp.max(jnp.abs(k_g), axis=(1, 2), keepdims=True))
    fixed = bound <= FIXED_REFERENCE_MAX_BOUND
    bias_lane = (jnp.arange(HEAD_PAD) == BIAS_LANE).reshape(1, 1, HEAD_PAD)
    return {
        "norm_g": row(norm_g),
        "w_mix": w_mix.astype(BF16),
        "w_gate": w[:, :, GATE_SRC:].astype(BF16),
        "sg_ln_g": row(sg_ln_g),
        "sg_ln_b": row(sg_ln_b),
        "sg_w": sg_w,
        "sg_bt": jnp.swapaxes(sg_b, 1, 2),
        "cq_g": row(mla_cq_g),
        "ckv_g": row(mla_ckv_g),
        "w_uq": _head_layout(mla_w_uq, MLA_HEADS).astype(BF16),
        "w_uk": w_uk.astype(BF16),
        "w_uvt": w_uvt.astype(BF16),
        "q_g": q_g,
        "k_g": k_g,
        "q_pad": jnp.where(bias_lane & fixed, -bound, 0.0).astype(F32),
        "k_pad": jnp.broadcast_to(jnp.where(bias_lane, 1.0, 0.0).astype(F32), (depth, 1, HEAD_PAD)),
        "fixed": fixed.astype(jnp.int32).reshape(depth),
        "w_gu": w_gu.astype(BF16),
        "b_gu": row(gla_b_gate),
        "b_gate": row(b_gate),
        "o_g": row(gla_o_g),
        "w_branch": w_branch.astype(BF16),
        "w_out": w_out.astype(BF16),
    }


def kernel(x, positions, norm_g, w_in, b_gate, sg_ln_g, sg_ln_b, sg_w, sg_b, mla_cq_g, mla_ckv_g, mla_w_uq, mla_w_ukv, mla_q_g, mla_k_g, gla_w_gate, gla_b_gate, gla_o_g, w_branch, w_out):
    batch, seq, d = x.shape
    assert d == D_MODEL and seq % max(ATTN_Q_TILE, GLA_TILE) == 0
    assert (batch * seq) % IN_TILE == 0 and (batch * seq) % MERGE_TILE == 0
    assert batch % ROWS_PER_STEP == 0
    assert IN_TILE % ATTN_TILE == 0 and IN_TILE % SG_BLOCK == 0
    depth = w_in.shape[0]
    cos_t, sin_t = _rope_tables(positions)
    x2 = x.reshape(batch * seq, d)
    p = _pack_params(norm_g, w_in, b_gate, sg_ln_g, sg_ln_b, sg_w, sg_b, mla_cq_g, mla_ckv_g,
                     mla_w_uq, mla_w_ukv, mla_q_g, mla_k_g, gla_w_gate, gla_b_gate, gla_o_g,
                     w_branch, w_out)
    for l in range(depth):
        ya, qb, kb, vbt, zb, qc, kc, vc, la, zc = _in_proj(x2, cos_t, sin_t, p, l)
        yc = _gla(qc, kc, vc, la, zc, p["o_g"], l, batch, seq)
        yb = _attn(p["fixed"], qb, kb, vbt, zb, l, batch, seq)
        x2 = _merge(x2, ya, yb, yc, p, l)
    return x2.reshape(batch, seq, d)
```

```python
import functools
import math

import jax
import jax.numpy as jnp
from jax import lax
from jax.experimental import pallas as pl
from jax.experimental.pallas import tpu as pltpu

F32 = jnp.float32
BF16 = jnp.bfloat16

D_MODEL = 1024
CHUNK = 64
BRANCH_W = 512
EPS = 1e-6
SG_BLOCK = 128
SG_GROUPS = 4
MLA_HEADS = 8
MLA_NOPE = 64
MLA_ROPE = 32
MLA_QK = MLA_NOPE + MLA_ROPE
MLA_V = 64
MLA_Q_RANK = 256
MLA_KV_RANK = 128
ROPE_THETA = 10000.0
GLA_HEADS = 4
GLA_DK = 64
GLA_DV = 128
GLA_GATE_RANK = 16
GLA_TAU = 16.0

LANES = 128
SUBLANES = 8
HEAD_PAD = LANES
ROPE_HALF = MLA_ROPE // 2
X1_LO = 0
X2_LO = LANES // 2
BIAS_LANE = MLA_QK
GATE_LANE = BIAS_LANE + 1


def _head_lane_source():
    src = [-1] * LANES
    nope_lanes = list(range(ROPE_HALF, X2_LO)) + list(range(X2_LO + ROPE_HALF, MLA_QK))
    for f, lane in enumerate(nope_lanes):
        src[lane] = f
    for f in range(ROPE_HALF):
        src[X1_LO + f] = MLA_NOPE + f
        src[X2_LO + f] = MLA_NOPE + ROPE_HALF + f
    return src


HEAD_LANE_SOURCE = _head_lane_source()

A_U, A_V, A_Z = 0, BRANCH_W, 2 * BRANCH_W
B_CQ = 3 * BRANCH_W
B_CKV = B_CQ + MLA_Q_RANK
B_KR = B_CKV + MLA_KV_RANK
B_Z = B_KR + LANES
C_Q = B_Z + BRANCH_W
C_K = C_Q + GLA_HEADS * GLA_DK
C_V = C_K + GLA_HEADS * GLA_DK
C_Z = C_V + GLA_HEADS * GLA_DV
SRC_KR = B_KR
SRC_ZB = SRC_KR + MLA_ROPE
SRC_GC = SRC_ZB + BRANCH_W + 2 * GLA_HEADS * GLA_DK + GLA_HEADS * GLA_DV
SRC_ZC = SRC_GC + GLA_GATE_RANK
GATE_SRC = SRC_ZC + BRANCH_W

IN_TILE = 1024
MERGE_TILE = 512
ATTN_TILE = 256
ATTN_Q_TILE = 2 * ATTN_TILE
GLA_TILE = 512
GLA_CUMSUM_SPAN = 256
ROWS_PER_STEP = 4
VMEM_LIMIT = 56 * 1024 * 1024

SCORE_BOUND_MARGIN = 1.02
FIXED_REFERENCE_MAX_BOUND = 40.0

NT_DIMS = (((1,), (1,)), ((), ()))
TN_DIMS = (((0,), (0,)), ((), ()))


def _layer_spec(stacked, layer):
    tail = stacked.shape[1:]
    index = (layer,) + (0,) * len(tail)
    return pl.BlockSpec((None,) + tail, lambda *_: index, pipeline_mode=pl.Buffered(1))


def _sigmoid(x):
    return 0.5 * (jnp.tanh(0.5 * x) + 1.0)


def _silu(x):
    return x * _sigmoid(x)


def _gelu_tanh(x):
    c = math.sqrt(2.0 / math.pi)
    return 0.5 * x * (1.0 + jnp.tanh(c * (x + 0.044715 * (x * x * x))))


def _log_sigmoid(x):
    return jnp.minimum(x, 0.0) - jnp.log(1.0 + jnp.exp(-jnp.abs(x)))


def _rms(x, width):
    return lax.rsqrt(jnp.sum(x * x, axis=-1, keepdims=True) * (1.0 / width) + EPS)


def _rope_body(pos_ref, cos_ref, sin_ref):
    tm = pos_ref.shape[-1]
    fidx = lax.broadcasted_iota(jnp.int32, (ROPE_HALF, 1), 0).astype(F32)
    inv = 1.0 / jnp.exp(fidx * (2.0 / MLA_ROPE) * math.log(ROPE_THETA))
    ang = inv * pos_ref[0].astype(F32)
    cos = jnp.cos(ang)
    sin = jnp.sin(ang)
    nope_a = X2_LO - (X1_LO + ROPE_HALF)
    nope_b = MLA_QK - (X2_LO + ROPE_HALF)
    ones = lambda n: jnp.ones((n, tm), F32)
    zeros = lambda n: jnp.zeros((n, tm), F32)
    cos_t = jnp.concatenate([cos, ones(nope_a), cos, ones(nope_b), zeros(LANES - MLA_QK)], axis=0)
    sin_t = jnp.concatenate([-sin, zeros(nope_a), sin, zeros(nope_b + LANES - MLA_QK)], axis=0)
    cos_ref[...] = cos_t.T
    sin_ref[...] = sin_t.T


def _rope_tables(positions):
    t = positions.size
    tm = 1024 if t % 1024 == 0 else ATTN_TILE
    return pl.pallas_call(
        _rope_body,
        grid=(t // tm,),
        in_specs=[pl.BlockSpec((1, 1, tm), lambda i: (i, 0, 0))],
        out_specs=[pl.BlockSpec((tm, LANES), lambda i: (i, 0))] * 2,
        out_shape=[jax.ShapeDtypeStruct((t, LANES), F32)] * 2,
        name="rope_tables",
    )(positions.reshape(t // tm, 1, tm))


def _in_proj_body(x_ref, cos_ref, sin_ref, ng_ref, win_ref, lng_ref, lnb_ref, sgw_ref, sgbt_ref,
                  cqg_ref, ckvg_ref, wuq_ref, wuk_ref, wuvt_ref, qg_ref, kg_ref, qpad_ref, kpad_ref,
                  wgu_ref, bgu_ref,
                  ya_ref, qb_ref, kb_ref, vbt_ref, zb_ref, qc_ref, kc_ref, vc_ref, la_ref, zc_ref):
    tm = x_ref.shape[0]
    x = x_ref[...]
    h = (x * _rms(x, D_MODEL) * ng_ref[...]).astype(BF16)

    def proj(lo, width):
        return jnp.dot(h, win_ref[:, lo:lo + width], preferred_element_type=F32)

    u = _gelu_tanh(proj(A_U, BRANCH_W))
    v = _gelu_tanh(proj(A_V, BRANCH_W))
    mu = jnp.mean(v, axis=-1, keepdims=True)
    vc = v - mu
    vn = vc * lax.rsqrt(jnp.mean(vc * vc, axis=-1, keepdims=True) + EPS)
    vn = (vn * lng_ref[...] + lnb_ref[...]).astype(BF16)
    uz = u * _silu(proj(A_Z, BRANCH_W))
    ri = lax.broadcasted_iota(jnp.int32, (SG_BLOCK, SG_BLOCK), 0) // CHUNK
    ci = lax.broadcasted_iota(jnp.int32, (SG_BLOCK, SG_BLOCK), 1) // CHUNK
    chunk_causal = ri >= ci
    for g in range(SG_GROUPS):
        wg = jnp.where(chunk_causal, sgw_ref[g], 0.0).astype(BF16)
        bias = sgbt_ref[:, g:g + 1]
        cols = slice(g * LANES, (g + 1) * LANES)
        for n in range(tm // SG_BLOCK):
            rows = slice(n * SG_BLOCK, (n + 1) * SG_BLOCK)
            sv = jnp.dot(wg, vn[rows, cols], preferred_element_type=F32) + bias
            ya_ref[rows, cols] = (uz[rows, cols] * sv).astype(BF16)

    cq = proj(B_CQ, MLA_Q_RANK)
    cqn = (cq * _rms(cq, MLA_Q_RANK) * cqg_ref[...]).astype(BF16)
    q = jnp.dot(cqn, wuq_ref[...], preferred_element_type=F32)
    ckv = proj(B_CKV, MLA_KV_RANK)
    ckvn = (ckv * _rms(ckv, MLA_KV_RANK) * ckvg_ref[...]).astype(BF16)
    kn = jnp.dot(ckvn, wuk_ref[...], preferred_element_type=F32)
    for s in range(tm // ATTN_TILE):
        vbt_ref[s] = lax.dot_general(wuvt_ref[...], ckvn[s * ATTN_TILE:(s + 1) * ATTN_TILE], NT_DIMS,
                                     preferred_element_type=F32).astype(BF16)
    kr_glr = proj(B_KR, LANES)
    lane = lax.broadcasted_iota(jnp.int32, (1, LANES), 1)
    kr = jnp.where(lane < MLA_QK, kr_glr, 0.0)
    cos = cos_ref[...]
    sin = sin_ref[...]

    def norm_rope(xh, gain, pad_row):
        xn = xh * _rms(xh, MLA_QK) * gain
        partner = pltpu.roll(xn, LANES // 2, 1)
        return (xn * cos + partner * sin + pad_row).astype(BF16)

    for hd in range(MLA_HEADS):
        cols = slice(hd * HEAD_PAD, (hd + 1) * HEAD_PAD)
        qb_ref[:, cols] = norm_rope(q[:, cols], qg_ref[...], qpad_ref[...])
        kb_ref[:, cols] = norm_rope(kn[:, cols] + kr, kg_ref[...], kpad_ref[...])
    zb_ref[...] = _silu(proj(B_Z, BRANCH_W)).astype(BF16)

    qc_ref[...] = (proj(C_Q, GLA_HEADS * GLA_DK) * (GLA_DK ** -0.5)).astype(BF16)
    kc_ref[...] = proj(C_K, GLA_HEADS * GLA_DK).astype(BF16)
    vc_ref[...] = proj(C_V, GLA_HEADS * GLA_DV).astype(BF16)
    gl = jnp.dot(kr_glr.astype(BF16), wgu_ref[...], preferred_element_type=F32) + bgu_ref[...]
    la_ref[...] = _log_sigmoid(gl) * (1.0 / GLA_TAU)
    zc_ref[...] = _silu(proj(C_Z, BRANCH_W)).astype(BF16)


def _in_proj(x2, cos_t, sin_t, p, layer):
    t = x2.shape[0]
    tm = IN_TILE
    row = lambda w: pl.BlockSpec((tm, w), lambda i: (i, 0))
    consts = [p["norm_g"], p["w_mix"], p["sg_ln_g"], p["sg_ln_b"], p["sg_w"], p["sg_bt"],
              p["cq_g"], p["ckv_g"], p["w_uq"], p["w_uk"], p["w_uvt"], p["q_g"], p["k_g"],
              p["q_pad"], p["k_pad"], p["w_gu"], p["b_gu"]]
    out_widths = [(BRANCH_W, BF16), (MLA_HEADS * HEAD_PAD, BF16), (MLA_HEADS * HEAD_PAD, BF16),
                  None, (BRANCH_W, BF16),
                  (GLA_HEADS * GLA_DK, BF16), (GLA_HEADS * GLA_DK, BF16), (GLA_HEADS * GLA_DV, BF16),
                  (GLA_HEADS * GLA_DK, F32), (BRANCH_W, BF16)]
    per_step = tm // ATTN_TILE
    vt_spec = pl.BlockSpec((per_step, MLA_HEADS * MLA_V, ATTN_TILE), lambda i: (i, 0, 0))
    vt_shape = jax.ShapeDtypeStruct((t // ATTN_TILE, MLA_HEADS * MLA_V, ATTN_TILE), BF16)
    return pl.pallas_call(
        _in_proj_body,
        grid=(t // tm,),
        in_specs=[row(D_MODEL), row(LANES), row(LANES)] + [_layer_spec(c, layer) for c in consts],
        out_specs=[vt_spec if o is None else row(o[0]) for o in out_widths],
        out_shape=[vt_shape if o is None else jax.ShapeDtypeStruct((t, o[0]), o[1]) for o in out_widths],
        compiler_params=pltpu.CompilerParams(dimension_semantics=("parallel",),
                                             vmem_limit_bytes=VMEM_LIMIT),
        name="in_proj",
    )(x2, cos_t, sin_t, *consts)


def _gla_body(q_ref, k_ref, v_ref, la_ref, z_ref, og_ref, y_ref, state_ref):
    @pl.when(pl.program_id(1) == 0)
    def _():
        state_ref[...] = jnp.zeros_like(state_ref)

    n_rows, tc = q_ref.shape[0], q_ref.shape[1]
    kw = GLA_HEADS * GLA_DK
    stack = GLA_HEADS * CHUNK
    span = GLA_CUMSUM_SPAN
    ri = lax.broadcasted_iota(jnp.int32, (span, span), 0)
    ci = lax.broadcasted_iota(jnp.int32, (span, span), 1)
    tri = jnp.where((ri // CHUNK == ci // CHUNK) & (ri >= ci), 1.0, 0.0).astype(BF16)
    si = lax.broadcasted_iota(jnp.int32, (stack, CHUNK), 0) % CHUNK
    sj = lax.broadcasted_iota(jnp.int32, (stack, CHUNK), 1)
    causal = si >= sj
    lane = lax.broadcasted_iota(jnp.int32, (1, kw), 1)
    cums = []
    for r in range(n_rows):
        parts = []
        for s in range(tc // span):
            la = la_ref[r, s * span:(s + 1) * span, :]
            la_hi = la.astype(BF16)
            la_lo = (la - la_hi.astype(F32)).astype(BF16)
            parts.append(jnp.dot(tri, la_hi, preferred_element_type=F32)
                         + jnp.dot(tri, la_lo, preferred_element_type=F32))
        cums.append(jnp.concatenate(parts, axis=0))
    for c in range(tc // CHUNK):
        rows = slice(c * CHUNK, (c + 1) * CHUNK)
        for r in range(n_rows):
            b = cums[r][rows]
            b_last = b[CHUNK - 1:CHUNK, :]
            qt = q_ref[r, rows, :].astype(F32) * jnp.exp(b)
            kf = k_ref[r, rows, :].astype(F32)
            kt = (kf * jnp.exp(-b)).astype(BF16)
            ks = (kf * jnp.exp(b_last - b)).astype(BF16)
            dec = jnp.exp(b_last)
            q_stack = jnp.concatenate(
                [jnp.where((lane >= hd * GLA_DK) & (lane < (hd + 1) * GLA_DK), qt, 0.0).astype(BF16)
                 for hd in range(GLA_HEADS)], axis=0)
            v = v_ref[r, rows, :]
            att = lax.dot_general(q_stack, kt, NT_DIMS, preferred_element_type=F32)
            att = jnp.where(causal, att, 0.0).astype(BF16)
            st = state_ref[r]
            o_inter = lax.dot_general(q_stack, st.astype(BF16), NT_DIMS, preferred_element_type=F32)
            state_ref[r] = st * dec + lax.dot_general(v, ks, TN_DIMS, preferred_element_type=F32)
            for hd in range(GLA_HEADS):
                srows = slice(hd * CHUNK, (hd + 1) * CHUNK)
                cols = slice(hd * GLA_DV, (hd + 1) * GLA_DV)
                o = (jnp.dot(att[srows], v[:, cols], preferred_element_type=F32)
                     + o_inter[srows, cols])
                on = o * _rms(o, GLA_DV) * og_ref[...]
                y_ref[r, rows, cols] = (on * z_ref[r, rows, cols].astype(F32)).astype(BF16)


def _gla(qc, kc, vc, la, zc, o_g, layer, batch, seq):
    tc = GLA_TILE
    rps = ROWS_PER_STEP
    kw = GLA_HEADS * GLA_DK
    vw = GLA_HEADS * GLA_DV
    row = lambda w: pl.BlockSpec((rps, tc, w), lambda b, s: (b, s, 0))
    rows3 = lambda a: a.reshape(batch, seq, a.shape[-1])
    return pl.pallas_call(
        _gla_body,
        grid=(batch // rps, seq // tc),
        in_specs=[row(kw), row(kw), row(vw), row(kw), row(vw), _layer_spec(o_g, layer)],
        out_specs=row(vw),
        out_shape=jax.ShapeDtypeStruct((batch, seq, vw), BF16),
        scratch_shapes=[pltpu.VMEM((rps, vw, kw), F32)],
        compiler_params=pltpu.CompilerParams(dimension_semantics=("parallel", "arbitrary"),
                                             vmem_limit_bytes=VMEM_LIMIT),
        name="gla",
    )(rows3(qc), rows3(kc), rows3(vc), rows3(la), rows3(zc), o_g).reshape(batch * seq, vw)


def _attn_body(layer, fixed_ref, q_ref, k_ref, vt_ref, z_ref, y_ref, m_ref, l_ref, alpha_ref, acc_ref,
               s_ref, p_ref):
    tq = q_ref.shape[0]
    tk = vt_ref.shape[2]
    i = pl.program_id(1)
    ki = lax.broadcasted_iota(jnp.int32, (tk, tq), 0) // CHUNK
    qi = lax.broadcasted_iota(jnp.int32, (tk, tq), 1) // CHUNK
    visible = {"low": ki <= qi, "high": ki + tk // CHUNK <= qi}
    slabs = MLA_V // SUBLANES

    def all_sublanes(x, op):
        for shift in (4, 2, 1):
            x = op(x, pltpu.roll(x, shift, 0))
        return x

    def reduce_rows(st, op):
        parts = op(st.reshape(4, tk // SUBLANES // 4, SUBLANES, st.shape[-1]), axis=1)
        return op(parts, axis=0)

    def block_scores(hd, j, diagonal, queries=slice(None)):
        start = pl.multiple_of(j * tk, tk)
        cols = slice(hd * HEAD_PAD, (hd + 1) * HEAD_PAD)
        st = lax.dot_general(k_ref[pl.ds(start, tk), cols], q_ref[queries, cols], NT_DIMS,
                             preferred_element_type=F32)
        if diagonal is not None:
            st = jnp.where(visible[diagonal][:, queries], st, -jnp.inf)
        return st

    heads = range(MLA_HEADS)
    low, high = 2 * i, 2 * i + 1
    l_ref[...] = jnp.zeros(l_ref.shape, F32)
    acc_ref[...] = jnp.zeros(acc_ref.shape, F32)

    def scores_exp(hd, j, slot, diagonal=None, queries=slice(None)):
        st = block_scores(hd, j, diagonal, queries)
        p = jnp.exp2(st.reshape(tk // SUBLANES, SUBLANES, st.shape[-1]))
        l_ref[hd, :, queries] = l_ref[hd, :, queries] + all_sublanes(reduce_rows(p, jnp.sum), jnp.add)
        p_ref[slot, hd, :, queries] = p.reshape(st.shape).astype(BF16)

    def values_plain(hd, j, slot, queries=slice(None)):
        rows = slice(hd * MLA_V, (hd + 1) * MLA_V)
        acc_ref[rows, queries] = acc_ref[rows, queries] + jnp.dot(
            vt_ref[j, rows, :], p_ref[slot, hd, :, queries], preferred_element_type=F32)

    @pl.when(fixed_ref[layer] != 0)
    def _():
        upper = slice(tq // 2, tq)
        for hd in heads:
            scores_exp(hd, high, 0, "high", upper)
        for hd in heads:
            scores_exp(hd, low, 1, "low")
        for hd in heads:
            values_plain(hd, high, 0, upper)

        def pair(u):
            for hd in heads:
                values_plain(hd, jnp.where(u == 0, low, 2 * u - 1), 1)
            for hd in heads:
                scores_exp(hd, 2 * u, 0)
            for hd in heads:
                values_plain(hd, 2 * u, 0)
            for hd in heads:
                scores_exp(hd, 2 * u + 1, 1)

        def two_pairs(w, carry):
            pair(2 * w)
            pair(2 * w + 1)
            return carry

        lax.fori_loop(0, i // 2, two_pairs, 0)

        @pl.when(i % 2 == 1)
        def _():
            pair(i - 1)

        for hd in heads:
            values_plain(hd, jnp.where(i == 0, low, low - 1), 1)

    def softmax_running_max(hd):
        st = s_ref[hd].reshape(tk // SUBLANES, SUBLANES, tq)
        m_prev = m_ref[hd]
        m_new = jnp.maximum(m_prev, all_sublanes(reduce_rows(st, jnp.max), jnp.maximum))
        alpha = jnp.exp2(m_prev - m_new)
        p = jnp.exp2(st - m_new[None])
        l_ref[hd] = alpha * l_ref[hd] + all_sublanes(reduce_rows(p, jnp.sum), jnp.add)
        p_ref[0, hd] = p.reshape(tk, tq).astype(BF16)
        alpha_ref[hd] = alpha
        m_ref[hd] = m_new

    def values_rescaled(hd, j):
        rows = slice(hd * MLA_V, (hd + 1) * MLA_V)
        pv = jnp.dot(vt_ref[j, rows, :], p_ref[0, hd], preferred_element_type=F32)
        acc = acc_ref[rows, :].reshape(slabs, SUBLANES, tq) * alpha_ref[hd][None]
        acc_ref[rows, :] = acc.reshape(MLA_V, tq) + pv

    def round_(prev_block, next_block, diagonal=None):
        if prev_block is not None:
            for hd in heads:
                values_rescaled(hd, prev_block)
        for hd in heads:
            softmax_running_max(hd)
        if next_block is not None:
            for hd in heads:
                s_ref[hd] = block_scores(hd, next_block, diagonal)

    @pl.when(fixed_ref[layer] == 0)
    def _():
        m_ref[...] = jnp.full(m_ref.shape, -jnp.inf, F32)
        for hd in heads:
            s_ref[hd] = block_scores(hd, low, "low")
        round_(None, high, "high")

        @pl.when(i == 0)
        def _():
            round_(low, None)

        @pl.when(i > 0)
        def _():
            round_(low, 0)

            def body(t, carry):
                round_(jnp.where(t == 2, high, t - 3), t - 1)
                return carry

            lax.fori_loop(2, high, body, 0)
            round_(low - 2, None)

        for hd in heads:
            values_rescaled(hd, jnp.where(i == 0, high, low - 1))

    for hd in range(MLA_HEADS):
        rows = slice(hd * MLA_V, (hd + 1) * MLA_V)
        acc = acc_ref[rows, :].reshape(slabs, SUBLANES, tq) / l_ref[hd][None]
        acc_ref[rows, :] = acc.reshape(MLA_V, tq)
    y_ref[...] = (acc_ref[...].T * z_ref[...].astype(F32)).astype(BF16)


def _attn(fixed, qb, kb, vbt, zb, layer, batch, seq):
    tq, tk = ATTN_Q_TILE, ATTN_TILE
    nq = seq // tq
    hw = MLA_HEADS * HEAD_PAD
    stat = pltpu.VMEM((MLA_HEADS, SUBLANES, tq), F32)
    return pl.pallas_call(
        functools.partial(_attn_body, layer),
        grid=(batch, nq),
        in_specs=[pl.BlockSpec(memory_space=pltpu.SMEM),
                  pl.BlockSpec((tq, hw), lambda b, i: (b * nq + i, 0)),
                  pl.BlockSpec((seq, hw), lambda b, i: (b, 0)),
                  pl.BlockSpec((seq // tk, BRANCH_W, tk), lambda b, i: (b, 0, 0)),
                  pl.BlockSpec((tq, BRANCH_W), lambda b, i: (b * nq + i, 0))],
        out_specs=pl.BlockSpec((tq, BRANCH_W), lambda b, i: (b * nq + i, 0)),
        out_shape=jax.ShapeDtypeStruct((batch * seq, BRANCH_W), BF16),
        scratch_shapes=[stat, stat, stat,
                        pltpu.VMEM((MLA_HEADS * MLA_V, tq), F32),
                        pltpu.VMEM((MLA_HEADS, tk, tq), F32),
                        pltpu.VMEM((2, MLA_HEADS, tk, tq), BF16)],
        compiler_params=pltpu.CompilerParams(dimension_semantics=("parallel", "arbitrary"),
                                             vmem_limit_bytes=VMEM_LIMIT),
        name="mla_attn",
    )(fixed, qb, kb, vbt, zb)


def _merge_body(x_ref, ya_ref, yb_ref, yc_ref, ng_ref, wgate_ref, bgate_ref, wbr_ref, wout_ref, o_ref):
    x = x_ref[...]
    h = (x * _rms(x, D_MODEL) * ng_ref[...]).astype(BF16)
    merged = None
    for n, y_ref in enumerate((ya_ref, yb_ref, yc_ref)):
        cols = slice(n * D_MODEL, (n + 1) * D_MODEL)
        logits = jnp.dot(h, wgate_ref[:, cols], preferred_element_type=F32) + bgate_ref[:, cols]
        term = _sigmoid(logits) * jnp.dot(y_ref[...], wbr_ref[n], preferred_element_type=F32)
        merged = term if merged is None else merged + term
    o_ref[...] = x + jnp.dot(merged.astype(BF16), wout_ref[...], preferred_element_type=F32)


def _merge(x2, ya, yb, yc, p, layer):
    t = x2.shape[0]
    tm = MERGE_TILE
    row = lambda w: pl.BlockSpec((tm, w), lambda i: (i, 0))
    consts = [p["norm_g"], p["w_gate"], p["b_gate"], p["w_branch"], p["w_out"]]
    return pl.pallas_call(
        _merge_body,
        grid=(t // tm,),
        in_specs=[row(D_MODEL), row(BRANCH_W), row(BRANCH_W), row(BRANCH_W)]
                 + [_layer_spec(c, layer) for c in consts],
        out_specs=row(D_MODEL),
        out_shape=jax.ShapeDtypeStruct((t, D_MODEL), F32),
        compiler_params=pltpu.CompilerParams(dimension_semantics=("parallel",),
                                             vmem_limit_bytes=VMEM_LIMIT),
        name="merge",
    )(x2, ya, yb, yc, *consts)


def _head_layout(w, heads):
    lead = w.shape[:-1]
    w = w.reshape(lead + (heads, MLA_QK))
    w = jnp.pad(w, [(0, 0)] * (len(lead) + 1) + [(0, 1)])
    src = jnp.asarray([MLA_QK if s < 0 else s for s in HEAD_LANE_SOURCE], jnp.int32)
    return jnp.take(w, src, axis=-1).reshape(lead + (heads * HEAD_PAD,))


def _pack_params(norm_g, w_in, b_gate, sg_ln_g, sg_ln_b, sg_w, sg_b, mla_cq_g, mla_ckv_g, mla_w_uq,
                 mla_w_ukv, mla_q_g, mla_k_g, gla_w_gate, gla_b_gate, gla_o_g, w_branch, w_out):
    depth, d = w_in.shape[0], w_in.shape[1]
    w = w_in
    row = lambda g: g.reshape(depth, 1, -1)
    kr_block = _head_layout(
        jnp.concatenate([jnp.zeros((depth, d, MLA_NOPE), w.dtype), w[:, :, SRC_KR:SRC_ZB]], axis=2), 1)
    kr_block = kr_block.at[:, :, GATE_LANE:GATE_LANE + GLA_GATE_RANK].set(w[:, :, SRC_GC:SRC_ZC])
    w_mix = jnp.concatenate([w[:, :, :SRC_KR], kr_block, w[:, :, SRC_ZB:SRC_GC],
                             w[:, :, SRC_ZC:GATE_SRC]], axis=2)
    ukv = mla_w_ukv.reshape(depth, MLA_KV_RANK, MLA_HEADS, MLA_NOPE + MLA_V)
    w_uk = _head_layout(jnp.pad(ukv[..., :MLA_NOPE], ((0, 0), (0, 0), (0, 0), (0, MLA_ROPE)))
                        .reshape(depth, MLA_KV_RANK, -1), MLA_HEADS)
    w_uvt = jnp.swapaxes(ukv[..., MLA_NOPE:].reshape(depth, MLA_KV_RANK, -1), 1, 2)
    w_gu = jnp.zeros((depth, LANES, GLA_HEADS * GLA_DK), F32).at[
        :, GATE_LANE:GATE_LANE + GLA_GATE_RANK].set(gla_w_gate)
    q_g = row(_head_layout(mla_q_g, 1)) * (MLA_QK ** -0.5 * math.log2(math.e))
    k_g = row(_head_layout(mla_k_g, 1))
    bound = (MLA_QK * SCORE_BOUND_MARGIN) * (jnp.max(jnp.abs(q_g), axis=(1, 2), keepdims=True)
                                             * jnp.max(jnp.abs(k_g), axis=(1, 2), keepdims=True))
    fixed = bound <= FIXED_REFERENCE_MAX_BOUND
    bias_lane = (jnp.arange(HEAD_PAD) == BIAS_LANE).reshape(1, 1, HEAD_PAD)
    return {
        "norm_g": row(norm_g),
        "w_mix": w_mix.astype(BF16),
        "w_gate": w[:, :, GATE_SRC:].astype(BF16),
        "sg_ln_g": row(sg_ln_g),
        "sg_ln_b": row(sg_ln_b),
        "sg_w": sg_w,
        "sg_bt": jnp.swapaxes(sg_b, 1, 2),
        "cq_g": row(mla_cq_g),
        "ckv_g": row(mla_ckv_g),
        "w_uq": _head_layout(mla_w_uq, MLA_HEADS).astype(BF16),
        "w_uk": w_uk.astype(BF16),
        "w_uvt": w_uvt.astype(BF16),
        "q_g": q_g,
        "k_g": k_g,
        "q_pad": jnp.where(bias_lane & fixed, -bound, 0.0).astype(F32),
        "k_pad": jnp.broadcast_to(jnp.where(bias_lane, 1.0, 0.0).astype(F32), (depth, 1, HEAD_PAD)),
        "fixed": fixed.astype(jnp.int32).reshape(depth),
        "w_gu": w_gu.astype(BF16),
        "b_gu": row(gla_b_gate),
        "b_gate": row(b_gate),
        "o_g": row(gla_o_g),
        "w_branch": w_branch.astype(BF16),
        "w_out": w_out.astype(BF16),
    }


def kernel(x, positions, norm_g, w_in, b_gate, sg_ln_g, sg_ln_b, sg_w, sg_b, mla_cq_g, mla_ckv_g, mla_w_uq, mla_w_ukv, mla_q_g, mla_k_g, gla_w_gate, gla_b_gate, gla_o_g, w_branch, w_out):
    batch, seq, d = x.shape
    assert d == D_MODEL and seq % max(ATTN_Q_TILE, GLA_TILE) == 0
    assert (batch * seq) % IN_TILE == 0 and (batch * seq) % MERGE_TILE == 0
    assert batch % ROWS_PER_STEP == 0
    assert IN_TILE % ATTN_TILE == 0 and IN_TILE % SG_BLOCK == 0
    depth = w_in.shape[0]
    cos_t, sin_t = _rope_tables(positions)
    x2 = x.reshape(batch * seq, d)
    p = _pack_params(norm_g, w_in, b_gate, sg_ln_g, sg_ln_b, sg_w, sg_b, mla_cq_g, mla_ckv_g,
                     mla_w_uq, mla_w_ukv, mla_q_g, mla_k_g, gla_w_gate, gla_b_gate, gla_o_g,
                     w_branch, w_out)
    for l in range(depth):
        ya, qb, kb, vbt, zb, qc, kc, vc, la, zc = _in_proj(x2, cos_t, sin_t, p, l)
        yc = _gla(qc, kc, vc, la, zc, p["o_g"], l, batch, seq)
        yb = _attn(p["fixed"], qb, kb, vbt, zb, l, batch, seq)
        x2 = _merge(x2, ya, yb, yc, p, l)
    return x2.reshape(batch, seq, d)
```

```python
import functools
import math

import jax
import jax.numpy as jnp
from jax import lax
from jax.experimental import pallas as pl
from jax.experimental.pallas import tpu as pltpu

F32 = jnp.float32
BF16 = jnp.bfloat16

D_MODEL = 1024
CHUNK = 64
BRANCH_W = 512
EPS = 1e-6
SG_BLOCK = 128
SG_GROUPS = 4
MLA_HEADS = 8
MLA_NOPE = 64
MLA_ROPE = 32
MLA_QK = MLA_NOPE + MLA_ROPE
MLA_V = 64
MLA_Q_RANK = 256
MLA_KV_RANK = 128
ROPE_THETA = 10000.0
GLA_HEADS = 4
GLA_DK = 64
GLA_DV = 128
GLA_GATE_RANK = 16
GLA_TAU = 16.0

LANES = 128
SUBLANES = 8
HEAD_PAD = LANES
ROPE_HALF = MLA_ROPE // 2
X1_LO = 0
X2_LO = LANES // 2
BIAS_LANE = MLA_QK
GATE_LANE = BIAS_LANE + 1


def _head_lane_source():
    src = [-1] * LANES
    nope_lanes = list(range(ROPE_HALF, X2_LO)) + list(range(X2_LO + ROPE_HALF, MLA_QK))
    for f, lane in enumerate(nope_lanes):
        src[lane] = f
    for f in range(ROPE_HALF):
        src[X1_LO + f] = MLA_NOPE + f
        src[X2_LO + f] = MLA_NOPE + ROPE_HALF + f
    return src


HEAD_LANE_SOURCE = _head_lane_source()

A_U, A_V, A_Z = 0, BRANCH_W, 2 * BRANCH_W
B_CQ = 3 * BRANCH_W
B_CKV = B_CQ + MLA_Q_RANK
B_KR = B_CKV + MLA_KV_RANK
B_Z = B_KR + LANES
C_Q = B_Z + BRANCH_W
C_K = C_Q + GLA_HEADS * GLA_DK
C_V = C_K + GLA_HEADS * GLA_DK
C_Z = C_V + GLA_HEADS * GLA_DV
SRC_KR = B_KR
SRC_ZB = SRC_KR + MLA_ROPE
SRC_GC = SRC_ZB + BRANCH_W + 2 * GLA_HEADS * GLA_DK + GLA_HEADS * GLA_DV
SRC_ZC = SRC_GC + GLA_GATE_RANK
GATE_SRC = SRC_ZC + BRANCH_W

IN_TILE = 1024
MERGE_TILE = 512
ATTN_TILE = 256
ATTN_Q_TILE = 2 * ATTN_TILE
GLA_TILE = 512
GLA_CUMSUM_SPAN = 256
ROWS_PER_STEP = 4
VMEM_LIMIT = 56 * 1024 * 1024

SCORE_BOUND_MARGIN = 1.02
FIXED_REFERENCE_MAX_BOUND = 40.0

NT_DIMS = (((1,), (1,)), ((), ()))
TN_DIMS = (((0,), (0,)), ((), ()))


def _layer_spec(stacked, layer):
    tail = stacked.shape[1:]
    index = (layer,) + (0,) * len(tail)
    return pl.BlockSpec((None,) + tail, lambda *_: index, pipeline_mode=pl.Buffered(1))


def _sigmoid(x):
    return 0.5 * (jnp.tanh(0.5 * x) + 1.0)


def _silu(x):
    return x * _sigmoid(x)


def _gelu_tanh(x):
    c = math.sqrt(2.0 / math.pi)
    return 0.5 * x * (1.0 + jnp.tanh(c * (x + 0.044715 * (x * x * x))))


def _log_sigmoid(x):
    return jnp.minimum(x, 0.0) - jnp.log(1.0 + jnp.exp(-jnp.abs(x)))


def _rms(x, width):
    return lax.rsqrt(jnp.sum(x * x, axis=-1, keepdims=True) * (1.0 / width) + EPS)


def _rope_body(pos_ref, cos_ref, sin_ref):
    tm = pos_ref.shape[-1]
    fidx = lax.broadcasted_iota(jnp.int32, (ROPE_HALF, 1), 0).astype(F32)
    inv = 1.0 / jnp.exp(fidx * (2.0 / MLA_ROPE) * math.log(ROPE_THETA))
    ang = inv * pos_ref[0].astype(F32)
    cos = jnp.cos(ang)
    sin = jnp.sin(ang)
    nope_a = X2_LO - (X1_LO + ROPE_HALF)
    nope_b = MLA_QK - (X2_LO + ROPE_HALF)
    ones = lambda n: jnp.ones((n, tm), F32)
    zeros = lambda n: jnp.zeros((n, tm), F32)
    cos_t = jnp.concatenate([cos, ones(nope_a), cos, ones(nope_b), zeros(LANES - MLA_QK)], axis=0)
    sin_t = jnp.concatenate([-sin, zeros(nope_a), sin, zeros(nope_b + LANES - MLA_QK)], axis=0)
    cos_ref[...] = cos_t.T
    sin_ref[...] = sin_t.T


def _rope_tables(positions):
    t = positions.size
    tm = 1024 if t % 1024 == 0 else ATTN_TILE
    return pl.pallas_call(
        _rope_body,
        grid=(t // tm,),
        in_specs=[pl.BlockSpec((1, 1, tm), lambda i: (i, 0, 0))],
        out_specs=[pl.BlockSpec((tm, LANES), lambda i: (i, 0))] * 2,
        out_shape=[jax.ShapeDtypeStruct((t, LANES), F32)] * 2,
        name="rope_tables",
    )(positions.reshape(t // tm, 1, tm))


def _in_proj_body(x_ref, cos_ref, sin_ref, ng_ref, win_ref, lng_ref, lnb_ref, sgw_ref, sgbt_ref,
                  cqg_ref, ckvg_ref, wuq_ref, wuk_ref, wuvt_ref, qg_ref, kg_ref, qpad_ref, kpad_ref,
                  wgu_ref, bgu_ref,
                  ya_ref, qb_ref, kb_ref, vbt_ref, zb_ref, qc_ref, kc_ref, vc_ref, la_ref, zc_ref):
    tm = x_ref.shape[0]
    x = x_ref[...]
    h = (x * _rms(x, D_MODEL) * ng_ref[...]).astype(BF16)

    def proj(lo, width):
        return jnp.dot(h, win_ref[:, lo:lo + width], preferred_element_type=F32)

    u = _gelu_tanh(proj(A_U, BRANCH_W))
    v = _gelu_tanh(proj(A_V, BRANCH_W))
    mu = jnp.mean(v, axis=-1, keepdims=True)
    vc = v - mu
    vn = vc * lax.rsqrt(jnp.mean(vc * vc, axis=-1, keepdims=True) + EPS)
    vn = (vn * lng_ref[...] + lnb_ref[...]).astype(BF16)
    uz = u * _silu(proj(A_Z, BRANCH_W))
    ri = lax.broadcasted_iota(jnp.int32, (SG_BLOCK, SG_BLOCK), 0) // CHUNK
    ci = lax.broadcasted_iota(jnp.int32, (SG_BLOCK, SG_BLOCK), 1) // CHUNK
    chunk_causal = ri >= ci
    for g in range(SG_GROUPS):
        wg = jnp.where(chunk_causal, sgw_ref[g], 0.0).astype(BF16)
        bias = sgbt_ref[:, g:g + 1]
        cols = slice(g * LANES, (g + 1) * LANES)
        for n in range(tm // SG_BLOCK):
            rows = slice(n * SG_BLOCK, (n + 1) * SG_BLOCK)
            sv = jnp.dot(wg, vn[rows, cols], preferred_element_type=F32) + bias
            ya_ref[rows, cols] = (uz[rows, cols] * sv).astype(BF16)

    cq = proj(B_CQ, MLA_Q_RANK)
    cqn = (cq * _rms(cq, MLA_Q_RANK) * cqg_ref[...]).astype(BF16)
    q = jnp.dot(cqn, wuq_ref[...], preferred_element_type=F32)
    ckv = proj(B_CKV, MLA_KV_RANK)
    ckvn = (ckv * _rms(ckv, MLA_KV_RANK) * ckvg_ref[...]).astype(BF16)
    kn = jnp.dot(ckvn, wuk_ref[...], preferred_element_type=F32)
    for s in range(tm // ATTN_TILE):
        vbt_ref[s] = lax.dot_general(wuvt_ref[...], ckvn[s * ATTN_TILE:(s + 1) * ATTN_TILE], NT_DIMS,
                                     preferred_element_type=F32).astype(BF16)
    kr_glr = proj(B_KR, LANES)
    lane = lax.broadcasted_iota(jnp.int32, (1, LANES), 1)
    kr = jnp.where(lane < MLA_QK, kr_glr, 0.0)
    cos = cos_ref[...]
    sin = sin_ref[...]

    def rope(xg):
        partner = pltpu.roll(xg, LANES // 2, 1)
        return xg * cos + partner * sin

    kg = kg_ref[...]
    kr_roped = rope(kr * kg)
    for hd in range(MLA_HEADS):
        cols = slice(hd * HEAD_PAD, (hd + 1) * HEAD_PAD)
        qh = q[:, cols]
        qb_ref[:, cols] = (rope(qh * _rms(qh, MLA_QK) * qg_ref[...]) + qpad_ref[...]).astype(BF16)
        kh = kn[:, cols]
        kb_ref[:, cols] = ((kh * kg + kr_roped) * _rms(kh + kr, MLA_QK) + kpad_ref[...]).astype(BF16)
    zb_ref[...] = _silu(proj(B_Z, BRANCH_W)).astype(BF16)

    qc_ref[...] = (proj(C_Q, GLA_HEADS * GLA_DK) * (GLA_DK ** -0.5)).astype(BF16)
    kc_ref[...] = proj(C_K, GLA_HEADS * GLA_DK).astype(BF16)
    vc_ref[...] = proj(C_V, GLA_HEADS * GLA_DV).astype(BF16)
    gl = jnp.dot(kr_glr.astype(BF16), wgu_ref[...], preferred_element_type=F32) + bgu_ref[...]
    la_ref[...] = _log_sigmoid(gl) * (1.0 / GLA_TAU)
    zc_ref[...] = _silu(proj(C_Z, BRANCH_W)).astype(BF16)


def _in_proj(x2, cos_t, sin_t, p, layer):
    t = x2.shape[0]
    tm = IN_TILE
    row = lambda w: pl.BlockSpec((tm, w), lambda i: (i, 0))
    consts = [p["norm_g"], p["w_mix"], p["sg_ln_g"], p["sg_ln_b"], p["sg_w"], p["sg_bt"],
              p["cq_g"], p["ckv_g"], p["w_uq"], p["w_uk"], p["w_uvt"], p["q_g"], p["k_g"],
              p["q_pad"], p["k_pad"], p["w_gu"], p["b_gu"]]
    out_widths = [(BRANCH_W, BF16), (MLA_HEADS * HEAD_PAD, BF16), (MLA_HEADS * HEAD_PAD, BF16),
                  None, (BRANCH_W, BF16),
                  (GLA_HEADS * GLA_DK, BF16), (GLA_HEADS * GLA_DK, BF16), (GLA_HEADS * GLA_DV, BF16),
                  (GLA_HEADS * GLA_DK, F32), (BRANCH_W, BF16)]
    per_step = tm // ATTN_TILE
    vt_spec = pl.BlockSpec((per_step, MLA_HEADS * MLA_V, ATTN_TILE), lambda i: (i, 0, 0))
    vt_shape = jax.ShapeDtypeStruct((t // ATTN_TILE, MLA_HEADS * MLA_V, ATTN_TILE), BF16)
    return pl.pallas_call(
        _in_proj_body,
        grid=(t // tm,),
        in_specs=[row(D_MODEL), row(LANES), row(LANES)] + [_layer_spec(c, layer) for c in consts],
        out_specs=[vt_spec if o is None else row(o[0]) for o in out_widths],
        out_shape=[vt_shape if o is None else jax.ShapeDtypeStruct((t, o[0]), o[1]) for o in out_widths],
        compiler_params=pltpu.CompilerParams(dimension_semantics=("parallel",),
                                             vmem_limit_bytes=VMEM_LIMIT),
        name="in_proj",
    )(x2, cos_t, sin_t, *consts)


def _gla_body(q_ref, k_ref, v_ref, la_ref, z_ref, og_ref, y_ref, state_ref):
    @pl.when(pl.program_id(1) == 0)
    def _():
        state_ref[...] = jnp.zeros_like(state_ref)

    n_rows, tc = q_ref.shape[0], q_ref.shape[1]
    kw = GLA_HEADS * GLA_DK
    stack = GLA_HEADS * CHUNK
    span = GLA_CUMSUM_SPAN
    ri = lax.broadcasted_iota(jnp.int32, (span, span), 0)
    ci = lax.broadcasted_iota(jnp.int32, (span, span), 1)
    tri = jnp.where((ri // CHUNK == ci // CHUNK) & (ri >= ci), 1.0, 0.0).astype(BF16)
    si = lax.broadcasted_iota(jnp.int32, (stack, CHUNK), 0) % CHUNK
    sj = lax.broadcasted_iota(jnp.int32, (stack, CHUNK), 1)
    causal = si >= sj
    lane = lax.broadcasted_iota(jnp.int32, (1, kw), 1)
    cums = []
    for r in range(n_rows):
        parts = []
        for s in range(tc // span):
            la = la_ref[r, s * span:(s + 1) * span, :]
            la_hi = la.astype(BF16)
            la_lo = (la - la_hi.astype(F32)).astype(BF16)
            parts.append(jnp.dot(tri, la_hi, preferred_element_type=F32)
                         + jnp.dot(tri, la_lo, preferred_element_type=F32))
        cums.append(jnp.concatenate(parts, axis=0))
    for c in range(tc // CHUNK):
        rows = slice(c * CHUNK, (c + 1) * CHUNK)
        for r in range(n_rows):
            b = cums[r][rows]
            b_last = b[CHUNK - 1:CHUNK, :]
            qt = q_ref[r, rows, :].astype(F32) * jnp.exp(b)
            kf = k_ref[r, rows, :].astype(F32)
            kt = (kf * jnp.exp(-b)).astype(BF16)
            ks = (kf * jnp.exp(b_last - b)).astype(BF16)
            dec = jnp.exp(b_last)
            q_stack = jnp.concatenate(
                [jnp.where((lane >= hd * GLA_DK) & (lane < (hd + 1) * GLA_DK), qt, 0.0).astype(BF16)
                 for hd in range(GLA_HEADS)], axis=0)
            v = v_ref[r, rows, :]
            att = lax.dot_general(q_stack, kt, NT_DIMS, preferred_element_type=F32)
            att = jnp.where(causal, att, 0.0).astype(BF16)
            st = state_ref[r]
            o_inter = lax.dot_general(q_stack, st.astype(BF16), NT_DIMS, preferred_element_type=F32)
            state_ref[r] = st * dec + lax.dot_general(v, ks, TN_DIMS, preferred_element_type=F32)
            for hd in range(GLA_HEADS):
                srows = slice(hd * CHUNK, (hd + 1) * CHUNK)
                cols = slice(hd * GLA_DV, (hd + 1) * GLA_DV)
                o = (jnp.dot(att[srows], v[:, cols], preferred_element_type=F32)
                     + o_inter[srows, cols])
                on = o * _rms(o, GLA_DV) * og_ref[...]
                y_ref[r, rows, cols] = (on * z_ref[r, rows, cols].astype(F32)).astype(BF16)


def _gla(qc, kc, vc, la, zc, o_g, layer, batch, seq):
    tc = GLA_TILE
    rps = ROWS_PER_STEP
    kw = GLA_HEADS * GLA_DK
    vw = GLA_HEADS * GLA_DV
    row = lambda w: pl.BlockSpec((rps, tc, w), lambda b, s: (b, s, 0))
    rows3 = lambda a: a.reshape(batch, seq, a.shape[-1])
    return pl.pallas_call(
        _gla_body,
        grid=(batch // rps, seq // tc),
        in_specs=[row(kw), row(kw), row(vw), row(kw), row(vw), _layer_spec(o_g, layer)],
        out_specs=row(vw),
        out_shape=jax.ShapeDtypeStruct((batch, seq, vw), BF16),
        scratch_shapes=[pltpu.VMEM((rps, vw, kw), F32)],
        compiler_params=pltpu.CompilerParams(dimension_semantics=("parallel", "arbitrary"),
                                             vmem_limit_bytes=VMEM_LIMIT),
        name="gla",
    )(rows3(qc), rows3(kc), rows3(vc), rows3(la), rows3(zc), o_g).reshape(batch * seq, vw)


def _attn_body(layer, fixed_ref, q_ref, k_ref, vt_ref, z_ref, y_ref, m_ref, l_ref, alpha_ref, acc_ref,
               s_ref, p_ref):
    tq = q_ref.shape[0]
    tk = vt_ref.shape[2]
    i = pl.program_id(1)
    ki = lax.broadcasted_iota(jnp.int32, (tk, tq), 0) // CHUNK
    qi = lax.broadcasted_iota(jnp.int32, (tk, tq), 1) // CHUNK
    visible = {"low": ki <= qi, "high": ki + tk // CHUNK <= qi}
    slabs = MLA_V // SUBLANES

    def all_sublanes(x, op):
        for shift in (4, 2, 1):
            x = op(x, pltpu.roll(x, shift, 0))
        return x

    def reduce_rows(st, op):
        parts = op(st.reshape(4, tk // SUBLANES // 4, SUBLANES, st.shape[-1]), axis=1)
        return op(parts, axis=0)

    def block_scores(hd, j, diagonal, queries=slice(None)):
        start = pl.multiple_of(j * tk, tk)
        cols = slice(hd * HEAD_PAD, (hd + 1) * HEAD_PAD)
        st = lax.dot_general(k_ref[pl.ds(start, tk), cols], q_ref[queries, cols], NT_DIMS,
                             preferred_element_type=F32)
        if diagonal is not None:
            st = jnp.where(visible[diagonal][:, queries], st, -jnp.inf)
        return st

    heads = range(MLA_HEADS)
    low, high = 2 * i, 2 * i + 1
    l_ref[...] = jnp.zeros(l_ref.shape, F32)
    acc_ref[...] = jnp.zeros(acc_ref.shape, F32)

    def scores_exp(hd, j, slot, diagonal=None, queries=slice(None)):
        st = block_scores(hd, j, diagonal, queries)
        p = jnp.exp2(st.reshape(tk // SUBLANES, SUBLANES, st.shape[-1]))
        l_ref[hd, :, queries] = l_ref[hd, :, queries] + all_sublanes(reduce_rows(p, jnp.sum), jnp.add)
        p_ref[slot, hd, :, queries] = p.reshape(st.shape).astype(BF16)

    def values_plain(hd, j, slot, queries=slice(None)):
        rows = slice(hd * MLA_V, (hd + 1) * MLA_V)
        acc_ref[rows, queries] = acc_ref[rows, queries] + jnp.dot(
            vt_ref[j, rows, :], p_ref[slot, hd, :, queries], preferred_element_type=F32)

    @pl.when(fixed_ref[layer] != 0)
    def _():
        upper = slice(tq // 2, tq)
        for hd in heads:
            scores_exp(hd, high, 0, "high", upper)
        for hd in heads:
            scores_exp(hd, low, 1, "low")
        for hd in heads:
            values_plain(hd, high, 0, upper)

        def pair(u):
            for hd in heads:
                values_plain(hd, jnp.where(u == 0, low, 2 * u - 1), 1)
            for hd in heads:
                scores_exp(hd, 2 * u, 0)
            for hd in heads:
                values_plain(hd, 2 * u, 0)
            for hd in heads:
                scores_exp(hd, 2 * u + 1, 1)

        def two_pairs(w, carry):
            pair(2 * w)
            pair(2 * w + 1)
            return carry

        lax.fori_loop(0, i // 2, two_pairs, 0)

        @pl.when(i % 2 == 1)
        def _():
            pair(i - 1)

        for hd in heads:
            values_plain(hd, jnp.where(i == 0, low, low - 1), 1)

    def softmax_running_max(hd):
        st = s_ref[hd].reshape(tk // SUBLANES, SUBLANES, tq)
        m_prev = m_ref[hd]
        m_new = jnp.maximum(m_prev, all_sublanes(reduce_rows(st, jnp.max), jnp.maximum))
        alpha = jnp.exp2(m_prev - m_new)
        p = jnp.exp2(st - m_new[None])
        l_ref[hd] = alpha * l_ref[hd] + all_sublanes(reduce_rows(p, jnp.sum), jnp.add)
        p_ref[0, hd] = p.reshape(tk, tq).astype(BF16)
        alpha_ref[hd] = alpha
        m_ref[hd] = m_new

    def values_rescaled(hd, j):
        rows = slice(hd * MLA_V, (hd + 1) * MLA_V)
        pv = jnp.dot(vt_ref[j, rows, :], p_ref[0, hd], preferred_element_type=F32)
        acc = acc_ref[rows, :].reshape(slabs, SUBLANES, tq) * alpha_ref[hd][None]
        acc_ref[rows, :] = acc.reshape(MLA_V, tq) + pv

    def round_(prev_block, next_block, diagonal=None):
        if prev_block is not None:
            for hd in heads:
                values_rescaled(hd, prev_block)
        for hd in heads:
            softmax_running_max(hd)
        if next_block is not None:
            for hd in heads:
                s_ref[hd] = block_scores(hd, next_block, diagonal)

    @pl.when(fixed_ref[layer] == 0)
    def _():
        m_ref[...] = jnp.full(m_ref.shape, -jnp.inf, F32)
        for hd in heads:
            s_ref[hd] = block_scores(hd, low, "low")
        round_(None, high, "high")

        @pl.when(i == 0)
        def _():
            round_(low, None)

        @pl.when(i > 0)
        def _():
            round_(low, 0)

            def body(t, carry):
                round_(jnp.where(t == 2, high, t - 3), t - 1)
                return carry

            lax.fori_loop(2, high, body, 0)
            round_(low - 2, None)

        for hd in heads:
            values_rescaled(hd, jnp.where(i == 0, high, low - 1))

    for hd in range(MLA_HEADS):
        rows = slice(hd * MLA_V, (hd + 1) * MLA_V)
        acc = acc_ref[rows, :].reshape(slabs, SUBLANES, tq) / l_ref[hd][None]
        acc_ref[rows, :] = acc.reshape(MLA_V, tq)
    y_ref[...] = (acc_ref[...].T * z_ref[...].astype(F32)).astype(BF16)


def _attn(fixed, qb, kb, vbt, zb, layer, batch, seq):
    tq, tk = ATTN_Q_TILE, ATTN_TILE
    nq = seq // tq
    hw = MLA_HEADS * HEAD_PAD
    stat = pltpu.VMEM((MLA_HEADS, SUBLANES, tq), F32)
    return pl.pallas_call(
        functools.partial(_attn_body, layer),
        grid=(batch, nq),
        in_specs=[pl.BlockSpec(memory_space=pltpu.SMEM),
                  pl.BlockSpec((tq, hw), lambda b, i: (b * nq + i, 0)),
                  pl.BlockSpec((seq, hw), lambda b, i: (b, 0)),
                  pl.BlockSpec((seq // tk, BRANCH_W, tk), lambda b, i: (b, 0, 0)),
                  pl.BlockSpec((tq, BRANCH_W), lambda b, i: (b * nq + i, 0))],
        out_specs=pl.BlockSpec((tq, BRANCH_W), lambda b, i: (b * nq + i, 0)),
        out_shape=jax.ShapeDtypeStruct((batch * seq, BRANCH_W), BF16),
        scratch_shapes=[stat, stat, stat,
                        pltpu.VMEM((MLA_HEADS * MLA_V, tq), F32),
                        pltpu.VMEM((MLA_HEADS, tk, tq), F32),
                        pltpu.VMEM((2, MLA_HEADS, tk, tq), BF16)],
        compiler_params=pltpu.CompilerParams(dimension_semantics=("parallel", "arbitrary"),
                                             vmem_limit_bytes=VMEM_LIMIT),
        name="mla_attn",
    )(fixed, qb, kb, vbt, zb)


def _merge_body(x_ref, ya_ref, yb_ref, yc_ref, ng_ref, wgate_ref, bgate_ref, wbr_ref, wout_ref, o_ref):
    x = x_ref[...]
    h = (x * _rms(x, D_MODEL) * ng_ref[...]).astype(BF16)
    merged = None
    for n, y_ref in enumerate((ya_ref, yb_ref, yc_ref)):
        cols = slice(n * D_MODEL, (n + 1) * D_MODEL)
        logits = jnp.dot(h, wgate_ref[:, cols], preferred_element_type=F32) + bgate_ref[:, cols]
        term = _sigmoid(logits) * jnp.dot(y_ref[...], wbr_ref[n], preferred_element_type=F32)
        merged = term if merged is None else merged + term
    o_ref[...] = x + jnp.dot(merged.astype(BF16), wout_ref[...], preferred_element_type=F32)


def _merge(x2, ya, yb, yc, p, layer):
    t = x2.shape[0]
    tm = MERGE_TILE
    row = lambda w: pl.BlockSpec((tm, w), lambda i: (i, 0))
    consts = [p["norm_g"], p["w_gate"], p["b_gate"], p["w_branch"], p["w_out"]]
    return pl.pallas_call(
        _merge_body,
        grid=(t // tm,),
        in_specs=[row(D_MODEL), row(BRANCH_W), row(BRANCH_W), row(BRANCH_W)]
                 + [_layer_spec(c, layer) for c in consts],
        out_specs=row(D_MODEL),
        out_shape=jax.ShapeDtypeStruct((t, D_MODEL), F32),
        compiler_params=pltpu.CompilerParams(dimension_semantics=("parallel",),
                                             vmem_limit_bytes=VMEM_LIMIT),
        name="merge",
    )(x2, ya, yb, yc, *consts)


def _head_layout(w, heads):
    lead = w.shape[:-1]
    w = w.reshape(lead + (heads, MLA_QK))
    w = jnp.pad(w, [(0, 0)] * (len(lead) + 1) + [(0, 1)])
    src = jnp.asarray([MLA_QK if s < 0 else s for s in HEAD_LANE_SOURCE], jnp.int32)
    return jnp.take(w, src, axis=-1).reshape(lead + (heads * HEAD_PAD,))


def _pack_params(norm_g, w_in, b_gate, sg_ln_g, sg_ln_b, sg_w, sg_b, mla_cq_g, mla_ckv_g, mla_w_uq,
                 mla_w_ukv, mla_q_g, mla_k_g, gla_w_gate, gla_b_gate, gla_o_g, w_branch, w_out):
    depth, d = w_in.shape[0], w_in.shape[1]
    w = w_in
    row = lambda g: g.reshape(depth, 1, -1)
    kr_block = _head_layout(
        jnp.concatenate([jnp.zeros((depth, d, MLA_NOPE), w.dtype), w[:, :, SRC_KR:SRC_ZB]], axis=2), 1)
    kr_block = kr_block.at[:, :, GATE_LANE:GATE_LANE + GLA_GATE_RANK].set(w[:, :, SRC_GC:SRC_ZC])
    w_mix = jnp.concatenate([w[:, :, :SRC_KR], kr_block, w[:, :, SRC_ZB:SRC_GC],
                             w[:, :, SRC_ZC:GATE_SRC]], axis=2)
    ukv = mla_w_ukv.reshape(depth, MLA_KV_RANK, MLA_HEADS, MLA_NOPE + MLA_V)
    w_uk = _head_layout(jnp.pad(ukv[..., :MLA_NOPE], ((0, 0), (0, 0), (0, 0), (0, MLA_ROPE)))
                        .reshape(depth, MLA_KV_RANK, -1), MLA_HEADS)
    w_uvt = jnp.swapaxes(ukv[..., MLA_NOPE:].reshape(depth, MLA_KV_RANK, -1), 1, 2)
    w_gu = jnp.zeros((depth, LANES, GLA_HEADS * GLA_DK), F32).at[
        :, GATE_LANE:GATE_LANE + GLA_GATE_RANK].set(gla_w_gate)
    q_g = row(_head_layout(mla_q_g, 1)) * (MLA_QK ** -0.5 * math.log2(math.e))
    k_g = row(_head_layout(mla_k_g, 1))
    bound = (MLA_QK * SCORE_BOUND_MARGIN) * (jnp.max(jnp.abs(q_g), axis=(1, 2), keepdims=True)
                                             * jnp.max(jnp.abs(k_g), axis=(1, 2), keepdims=True))
    fixed = bound <= FIXED_REFERENCE_MAX_BOUND
    bias_lane = (jnp.arange(HEAD_PAD) == BIAS_LANE).reshape(1, 1, HEAD_PAD)
    return {
        "norm_g": row(norm_g),
        "w_mix": w_mix.astype(BF16),
        "w_gate": w[:, :, GATE_SRC:].astype(BF16),
        "sg_ln_g": row(sg_ln_g),
        "sg_ln_b": row(sg_ln_b),
        "sg_w": sg_w,
        "sg_bt": jnp.swapaxes(sg_b, 1, 2),
        "cq_g": row(mla_cq_g),
        "ckv_g": row(mla_ckv_g),
        "w_uq": _head_layout(mla_w_uq, MLA_HEADS).astype(BF16),
        "w_uk": w_uk.astype(BF16),
        "w_uvt": w_uvt.astype(BF16),
        "q_g": q_g,
        "k_g": k_g,
        "q_pad": jnp.where(bias_lane & fixed, -bound, 0.0).astype(F32),
        "k_pad": jnp.broadcast_to(jnp.where(bias_lane, 1.0, 0.0).astype(F32), (depth, 1, HEAD_PAD)),
        "fixed": fixed.astype(jnp.int32).reshape(depth),
        "w_gu": w_gu.astype(BF16),
        "b_gu": row(gla_b_gate),
        "b_gate": row(b_gate),
        "o_g": row(gla_o_g),
        "w_branch": w_branch.astype(BF16),
        "w_out": w_out.astype(BF16),
    }


def kernel(x, positions, norm_g, w_in, b_gate, sg_ln_g, sg_ln_b, sg_w, sg_b, mla_cq_g, mla_ckv_g, mla_w_uq, mla_w_ukv, mla_q_g, mla_k_g, gla_w_gate, gla_b_gate, gla_o_g, w_branch, w_out):
    batch, seq, d = x.shape
    assert d == D_MODEL and seq % max(ATTN_Q_TILE, GLA_TILE) == 0
    assert (batch * seq) % IN_TILE == 0 and (batch * seq) % MERGE_TILE == 0
    assert batch % ROWS_PER_STEP == 0
    assert IN_TILE % ATTN_TILE == 0 and IN_TILE % SG_BLOCK == 0
    depth = w_in.shape[0]
    cos_t, sin_t = _rope_tables(positions)
    x2 = x.reshape(batch * seq, d)
    p = _pack_params(norm_g, w_in, b_gate, sg_ln_g, sg_ln_b, sg_w, sg_b, mla_cq_g, mla_ckv_g,
                     mla_w_uq, mla_w_ukv, mla_q_g, mla_k_g, gla_w_gate, gla_b_gate, gla_o_g,
                     w_branch, w_out)
    for l in range(depth):
        ya, qb, kb, vbt, zb, qc, kc, vc, la, zc = _in_proj(x2, cos_t, sin_t, p, l)
        yc = _gla(qc, kc, vc, la, zc, p["o_g"], l, batch, seq)
        yb = _attn(p["fixed"], qb, kb, vbt, zb, l, batch, seq)
        x2 = _merge(x2, ya, yb, yc, p, l)
    return x2.reshape(batch, seq, d)
```

```python
import functools
import math

import jax
import jax.numpy as jnp
from jax import lax
from jax.experimental import pallas as pl
from jax.experimental.pallas import tpu as pltpu

F32 = jnp.float32
BF16 = jnp.bfloat16

D_MODEL = 1024
CHUNK = 64
BRANCH_W = 512
EPS = 1e-6
SG_BLOCK = 128
SG_GROUPS = 4
MLA_HEADS = 8
MLA_NOPE = 64
MLA_ROPE = 32
MLA_QK = MLA_NOPE + MLA_ROPE
MLA_V = 64
MLA_Q_RANK = 256
MLA_KV_RANK = 128
ROPE_THETA = 10000.0
GLA_HEADS = 4
GLA_DK = 64
GLA_DV = 128
GLA_GATE_RANK = 16
GLA_TAU = 16.0

LANES = 128
SUBLANES = 8
HEAD_PAD = LANES
ROPE_HALF = MLA_ROPE // 2
X1_LO = 0
X2_LO = LANES // 2
BIAS_LANE = MLA_QK
GATE_LANE = BIAS_LANE + 1


def _head_lane_source():
    src = [-1] * LANES
    nope_lanes = list(range(ROPE_HALF, X2_LO)) + list(range(X2_LO + ROPE_HALF, MLA_QK))
    for f, lane in enumerate(nope_lanes):
        src[lane] = f
    for f in range(ROPE_HALF):
        src[X1_LO + f] = MLA_NOPE + f
        src[X2_LO + f] = MLA_NOPE + ROPE_HALF + f
    return src


HEAD_LANE_SOURCE = _head_lane_source()

A_U, A_V, A_Z = 0, BRANCH_W, 2 * BRANCH_W
B_CQ = 3 * BRANCH_W
B_CKV = B_CQ + MLA_Q_RANK
B_KR = B_CKV + MLA_KV_RANK
B_Z = B_KR + LANES
C_Q = B_Z + BRANCH_W
C_K = C_Q + GLA_HEADS * GLA_DK
C_V = C_K + GLA_HEADS * GLA_DK
C_Z = C_V + GLA_HEADS * GLA_DV
SRC_KR = B_KR
SRC_ZB = SRC_KR + MLA_ROPE
SRC_GC = SRC_ZB + BRANCH_W + 2 * GLA_HEADS * GLA_DK + GLA_HEADS * GLA_DV
SRC_ZC = SRC_GC + GLA_GATE_RANK
GATE_SRC = SRC_ZC + BRANCH_W

IN_TILE = 1024
MERGE_TILE = 512
ATTN_TILE = 256
ATTN_Q_TILE = 2 * ATTN_TILE
GLA_TILE = 512
GLA_CUMSUM_SPAN = 256
ROWS_PER_STEP = 4
VMEM_LIMIT = 56 * 1024 * 1024

SCORE_BOUND_MARGIN = 1.02
FIXED_REFERENCE_MAX_BOUND = 40.0

NT_DIMS = (((1,), (1,)), ((), ()))
TN_DIMS = (((0,), (0,)), ((), ()))


def _layer_spec(stacked, layer):
    tail = stacked.shape[1:]
    index = (layer,) + (0,) * len(tail)
    return pl.BlockSpec((None,) + tail, lambda *_: index, pipeline_mode=pl.Buffered(1))


def _sigmoid(x):
    return 0.5 * (jnp.tanh(0.5 * x) + 1.0)


def _silu(x):
    return x * _sigmoid(x)


def _gelu_tanh(x):
    c = math.sqrt(2.0 / math.pi)
    return 0.5 * x * (1.0 + jnp.tanh(c * (x + 0.044715 * (x * x * x))))


def _log_sigmoid(x):
    return jnp.minimum(x, 0.0) - jnp.log(1.0 + jnp.exp(-jnp.abs(x)))


def _rms(x, width):
    return lax.rsqrt(jnp.sum(x * x, axis=-1, keepdims=True) * (1.0 / width) + EPS)


def _rope_body(pos_ref, cos_ref, sin_ref):
    tm = pos_ref.shape[-1]
    fidx = lax.broadcasted_iota(jnp.int32, (ROPE_HALF, 1), 0).astype(F32)
    inv = 1.0 / jnp.exp(fidx * (2.0 / MLA_ROPE) * math.log(ROPE_THETA))
    ang = inv * pos_ref[0].astype(F32)
    cos = jnp.cos(ang)
    sin = jnp.sin(ang)
    nope_a = X2_LO - (X1_LO + ROPE_HALF)
    nope_b = MLA_QK - (X2_LO + ROPE_HALF)
    ones = lambda n: jnp.ones((n, tm), F32)
    zeros = lambda n: jnp.zeros((n, tm), F32)
    cos_t = jnp.concatenate([cos, ones(nope_a), cos, ones(nope_b), zeros(LANES - MLA_QK)], axis=0)
    sin_t = jnp.concatenate([-sin, zeros(nope_a), sin, zeros(nope_b + LANES - MLA_QK)], axis=0)
    cos_ref[...] = cos_t.T
    sin_ref[...] = sin_t.T


def _rope_tables(positions):
    t = positions.size
    tm = 1024 if t % 1024 == 0 else ATTN_TILE
    return pl.pallas_call(
        _rope_body,
        grid=(t // tm,),
        in_specs=[pl.BlockSpec((1, 1, tm), lambda i: (i, 0, 0))],
        out_specs=[pl.BlockSpec((tm, LANES), lambda i: (i, 0))] * 2,
        out_shape=[jax.ShapeDtypeStruct((t, LANES), F32)] * 2,
        name="rope_tables",
    )(positions.reshape(t // tm, 1, tm))


def _in_proj_body(x_ref, cos_ref, sin_ref, ng_ref, win_ref, lng_ref, lnb_ref, sgw_ref, sgbt_ref,
                  cqg_ref, ckvg_ref, wuq_ref, wuk_ref, wuvt_ref, qg_ref, kg_ref, qpad_ref, kpad_ref,
                  wgu_ref, bgu_ref,
                  ya_ref, qb_ref, kb_ref, vbt_ref, zb_ref, qc_ref, kc_ref, vc_ref, la_ref, zc_ref):
    tm = x_ref.shape[0]
    x = x_ref[...]
    h = (x * _rms(x, D_MODEL) * ng_ref[...]).astype(BF16)

    def proj(lo, width):
        return jnp.dot(h, win_ref[:, lo:lo + width], preferred_element_type=F32)

    u = _gelu_tanh(proj(A_U, BRANCH_W))
    v = _gelu_tanh(proj(A_V, BRANCH_W))
    mu = jnp.mean(v, axis=-1, keepdims=True)
    vc = v - mu
    vn = vc * lax.rsqrt(jnp.mean(vc * vc, axis=-1, keepdims=True) + EPS)
    vn = (vn * lng_ref[...] + lnb_ref[...]).astype(BF16)
    uz = u * _silu(proj(A_Z, BRANCH_W))
    ri = lax.broadcasted_iota(jnp.int32, (SG_BLOCK, SG_BLOCK), 0) // CHUNK
    ci = lax.broadcasted_iota(jnp.int32, (SG_BLOCK, SG_BLOCK), 1) // CHUNK
    chunk_causal = ri >= ci
    for g in range(SG_GROUPS):
        wg = jnp.where(chunk_causal, sgw_ref[g], 0.0).astype(BF16)
        bias = sgbt_ref[:, g:g + 1]
        cols = slice(g * LANES, (g + 1) * LANES)
        for n in range(tm // SG_BLOCK):
            rows = slice(n * SG_BLOCK, (n + 1) * SG_BLOCK)
            sv = jnp.dot(wg, vn[rows, cols], preferred_element_type=F32) + bias
            ya_ref[rows, cols] = (uz[rows, cols] * sv).astype(BF16)

    cq = proj(B_CQ, MLA_Q_RANK)
    cqn = (cq * _rms(cq, MLA_Q_RANK) * cqg_ref[...]).astype(BF16)
    q = jnp.dot(cqn, wuq_ref[...], preferred_element_type=F32)
    ckv = proj(B_CKV, MLA_KV_RANK)
    ckvn = (ckv * _rms(ckv, MLA_KV_RANK) * ckvg_ref[...]).astype(BF16)
    kn = jnp.dot(ckvn, wuk_ref[...], preferred_element_type=F32)
    for s in range(tm // ATTN_TILE):
        vbt_ref[s] = lax.dot_general(wuvt_ref[...], ckvn[s * ATTN_TILE:(s + 1) * ATTN_TILE], NT_DIMS,
                                     preferred_element_type=F32).astype(BF16)
    kr_glr = proj(B_KR, LANES)
    lane = lax.broadcasted_iota(jnp.int32, (1, LANES), 1)
    kr = jnp.where(lane < MLA_QK, kr_glr, 0.0)
    cos = cos_ref[...]
    sin = sin_ref[...]

    def rope(xg):
        partner = pltpu.roll(xg, LANES // 2, 1)
        return xg * cos + partner * sin

    kg = kg_ref[...]
    kr_roped = rope(kr * kg)
    for hd in range(MLA_HEADS):
        cols = slice(hd * HEAD_PAD, (hd + 1) * HEAD_PAD)
        qh = q[:, cols]
        qb_ref[:, cols] = (rope(qh * _rms(qh, MLA_QK) * qg_ref[...]) + qpad_ref[...]).astype(BF16)
        kh = kn[:, cols]
        kb_ref[:, cols] = ((kh * kg + kr_roped) * _rms(kh + kr, MLA_QK) + kpad_ref[...]).astype(BF16)
    zb_ref[...] = _silu(proj(B_Z, BRANCH_W)).astype(BF16)

    qc_ref[...] = (proj(C_Q, GLA_HEADS * GLA_DK) * (GLA_DK ** -0.5)).astype(BF16)
    kc_ref[...] = proj(C_K, GLA_HEADS * GLA_DK).astype(BF16)
    vc_ref[...] = proj(C_V, GLA_HEADS * GLA_DV).astype(BF16)
    gl = jnp.dot(kr_glr.astype(BF16), wgu_ref[...], preferred_element_type=F32) + bgu_ref[...]
    la_ref[...] = _log_sigmoid(gl) * (1.0 / GLA_TAU)
    zc_ref[...] = _silu(proj(C_Z, BRANCH_W)).astype(BF16)


def _in_proj(x2, cos_t, sin_t, p, layer):
    t = x2.shape[0]
    tm = IN_TILE
    row = lambda w: pl.BlockSpec((tm, w), lambda i: (i, 0))
    consts = [p["norm_g"], p["w_mix"], p["sg_ln_g"], p["sg_ln_b"], p["sg_w"], p["sg_bt"],
              p["cq_g"], p["ckv_g"], p["w_uq"], p["w_uk"], p["w_uvt"], p["q_g"], p["k_g"],
              p["q_pad"], p["k_pad"], p["w_gu"], p["b_gu"]]
    out_widths = [(BRANCH_W, BF16), (MLA_HEADS * HEAD_PAD, BF16), (MLA_HEADS * HEAD_PAD, BF16),
                  ATTN_TILE, (BRANCH_W, BF16),
                  (GLA_HEADS * GLA_DK, BF16), (GLA_HEADS * GLA_DK, BF16), (GLA_HEADS * GLA_DV, BF16),
                  (GLA_HEADS * GLA_DK, F32), (BRANCH_W, BF16)]
    t_spec = lambda n: pl.BlockSpec((tm // n, BRANCH_W, n), lambda i: (i, 0, 0))
    t_shape = lambda n: jax.ShapeDtypeStruct((t // n, BRANCH_W, n), BF16)
    return pl.pallas_call(
        _in_proj_body,
        grid=(t // tm,),
        in_specs=[row(D_MODEL), row(LANES), row(LANES)] + [_layer_spec(c, layer) for c in consts],
        out_specs=[t_spec(o) if isinstance(o, int) else row(o[0]) for o in out_widths],
        out_shape=[t_shape(o) if isinstance(o, int) else jax.ShapeDtypeStruct((t, o[0]), o[1])
                   for o in out_widths],
        compiler_params=pltpu.CompilerParams(dimension_semantics=("parallel",),
                                             vmem_limit_bytes=VMEM_LIMIT),
        name="in_proj",
    )(x2, cos_t, sin_t, *consts)


def _gla_body(q_ref, k_ref, v_ref, la_ref, z_ref, og_ref, y_ref, state_ref):
    @pl.when(pl.program_id(1) == 0)
    def _():
        state_ref[...] = jnp.zeros_like(state_ref)

    n_rows, tc = q_ref.shape[0], q_ref.shape[1]
    kw = GLA_HEADS * GLA_DK
    stack = GLA_HEADS * CHUNK
    span = GLA_CUMSUM_SPAN
    ri = lax.broadcasted_iota(jnp.int32, (span, span), 0)
    ci = lax.broadcasted_iota(jnp.int32, (span, span), 1)
    tri = jnp.where((ri // CHUNK == ci // CHUNK) & (ri >= ci), 1.0, 0.0).astype(BF16)
    si = lax.broadcasted_iota(jnp.int32, (stack, CHUNK), 0) % CHUNK
    sj = lax.broadcasted_iota(jnp.int32, (stack, CHUNK), 1)
    causal = si >= sj
    lane = lax.broadcasted_iota(jnp.int32, (1, kw), 1)
    cums = []
    for r in range(n_rows):
        parts = []
        for s in range(tc // span):
            la = la_ref[r, s * span:(s + 1) * span, :]
            la_hi = la.astype(BF16)
            la_lo = (la - la_hi.astype(F32)).astype(BF16)
            parts.append(jnp.dot(tri, la_hi, preferred_element_type=F32)
                         + jnp.dot(tri, la_lo, preferred_element_type=F32))
        cums.append(jnp.concatenate(parts, axis=0))
    for c in range(tc // CHUNK):
        rows = slice(c * CHUNK, (c + 1) * CHUNK)
        for r in range(n_rows):
            b = cums[r][rows]
            b_last = b[CHUNK - 1:CHUNK, :]
            qt = q_ref[r, rows, :].astype(F32) * jnp.exp(b)
            kf = k_ref[r, rows, :].astype(F32)
            kt = (kf * jnp.exp(-b)).astype(BF16)
            ks = (kf * jnp.exp(b_last - b)).astype(BF16)
            dec = jnp.exp(b_last)
            q_stack = jnp.concatenate(
                [jnp.where((lane >= hd * GLA_DK) & (lane < (hd + 1) * GLA_DK), qt, 0.0).astype(BF16)
                 for hd in range(GLA_HEADS)], axis=0)
            v = v_ref[r, rows, :]
            att = lax.dot_general(q_stack, kt, NT_DIMS, preferred_element_type=F32)
            att = jnp.where(causal, att, 0.0).astype(BF16)
            st = state_ref[r]
            o_inter = lax.dot_general(q_stack, st.astype(BF16), NT_DIMS, preferred_element_type=F32)
            state_ref[r] = st * dec + lax.dot_general(v, ks, TN_DIMS, preferred_element_type=F32)
            for hd in range(GLA_HEADS):
                srows = slice(hd * CHUNK, (hd + 1) * CHUNK)
                cols = slice(hd * GLA_DV, (hd + 1) * GLA_DV)
                o = (jnp.dot(att[srows], v[:, cols], preferred_element_type=F32)
                     + o_inter[srows, cols])
                on = o * _rms(o, GLA_DV) * og_ref[...]
                y_ref[r, rows, cols] = (on * z_ref[r, rows, cols].astype(F32)).astype(BF16)


def _gla(qc, kc, vc, la, zc, o_g, layer, batch, seq):
    tc = GLA_TILE
    rps = ROWS_PER_STEP
    kw = GLA_HEADS * GLA_DK
    vw = GLA_HEADS * GLA_DV
    row = lambda w: pl.BlockSpec((rps, tc, w), lambda b, s: (b, s, 0))
    rows3 = lambda a: a.reshape(batch, seq, a.shape[-1])
    return pl.pallas_call(
        _gla_body,
        grid=(batch // rps, seq // tc),
        in_specs=[row(kw), row(kw), row(vw), row(kw), row(vw), _layer_spec(o_g, layer)],
        out_specs=row(vw),
        out_shape=jax.ShapeDtypeStruct((batch, seq, vw), BF16),
        scratch_shapes=[pltpu.VMEM((rps, vw, kw), F32)],
        compiler_params=pltpu.CompilerParams(dimension_semantics=("parallel", "arbitrary"),
                                             vmem_limit_bytes=VMEM_LIMIT),
        name="gla",
    )(rows3(qc), rows3(kc), rows3(vc), rows3(la), rows3(zc), o_g).reshape(batch * seq, vw)


def _attn_body(layer, fixed_ref, q_ref, k_ref, vt_ref, y_ref, m_ref, l_ref, alpha_ref, acc_ref,
               s_ref, p_ref):
    tq = q_ref.shape[0]
    tk = vt_ref.shape[2]
    i = pl.program_id(1)
    ki = lax.broadcasted_iota(jnp.int32, (tk, tq), 0) // CHUNK
    qi = lax.broadcasted_iota(jnp.int32, (tk, tq), 1) // CHUNK
    visible = {"low": ki <= qi, "high": ki + tk // CHUNK <= qi}
    slabs = MLA_V // SUBLANES

    def all_sublanes(x, op):
        for shift in (4, 2, 1):
            x = op(x, pltpu.roll(x, shift, 0))
        return x

    def reduce_rows(st, op):
        parts = op(st.reshape(4, tk // SUBLANES // 4, SUBLANES, st.shape[-1]), axis=1)
        return op(parts, axis=0)

    def block_scores(hd, j, diagonal, queries=slice(None)):
        start = pl.multiple_of(j * tk, tk)
        cols = slice(hd * HEAD_PAD, (hd + 1) * HEAD_PAD)
        st = lax.dot_general(k_ref[pl.ds(start, tk), cols], q_ref[queries, cols], NT_DIMS,
                             preferred_element_type=F32)
        if diagonal is not None:
            st = jnp.where(visible[diagonal][:, queries], st, -jnp.inf)
        return st

    heads = range(MLA_HEADS)
    low, high = 2 * i, 2 * i + 1
    l_ref[...] = jnp.zeros(l_ref.shape, F32)
    acc_ref[...] = jnp.zeros(acc_ref.shape, F32)

    def scores_exp(hd, j, slot, diagonal=None, queries=slice(None)):
        st = block_scores(hd, j, diagonal, queries)
        p = jnp.exp2(st.reshape(tk // SUBLANES, SUBLANES, st.shape[-1]))
        l_ref[hd, :, queries] = l_ref[hd, :, queries] + all_sublanes(reduce_rows(p, jnp.sum), jnp.add)
        p_ref[slot, hd, :, queries] = p.reshape(st.shape).astype(BF16)

    def values_plain(hd, j, slot, queries=slice(None)):
        rows = slice(hd * MLA_V, (hd + 1) * MLA_V)
        acc_ref[rows, queries] = acc_ref[rows, queries] + jnp.dot(
            vt_ref[j, rows, :], p_ref[slot, hd, :, queries], preferred_element_type=F32)

    @pl.when(fixed_ref[layer] != 0)
    def _():
        upper = slice(tq // 2, tq)
        for hd in heads:
            scores_exp(hd, high, 0, "high", upper)
        for hd in heads:
            scores_exp(hd, low, 1, "low")
        for hd in heads:
            values_plain(hd, high, 0, upper)

        def pair(u):
            for hd in heads:
                values_plain(hd, jnp.where(u == 0, low, 2 * u - 1), 1)
            for hd in heads:
                scores_exp(hd, 2 * u, 0)
            for hd in heads:
                values_plain(hd, 2 * u, 0)
            for hd in heads:
                scores_exp(hd, 2 * u + 1, 1)

        def two_pairs(w, carry):
            pair(2 * w)
            pair(2 * w + 1)
            return carry

        lax.fori_loop(0, i // 2, two_pairs, 0)

        @pl.when(i % 2 == 1)
        def _():
            pair(i - 1)

        for hd in heads:
            values_plain(hd, jnp.where(i == 0, low, low - 1), 1)

    def softmax_running_max(hd):
        st = s_ref[hd].reshape(tk // SUBLANES, SUBLANES, tq)
        m_prev = m_ref[hd]
        m_new = jnp.maximum(m_prev, all_sublanes(reduce_rows(st, jnp.max), jnp.maximum))
        alpha = jnp.exp2(m_prev - m_new)
        p = jnp.exp2(st - m_new[None])
        l_ref[hd] = alpha * l_ref[hd] + all_sublanes(reduce_rows(p, jnp.sum), jnp.add)
        p_ref[0, hd] = p.reshape(tk, tq).astype(BF16)
        alpha_ref[hd] = alpha
        m_ref[hd] = m_new

    def values_rescaled(hd, j):
        rows = slice(hd * MLA_V, (hd + 1) * MLA_V)
        pv = jnp.dot(vt_ref[j, rows, :], p_ref[0, hd], preferred_element_type=F32)
        acc = acc_ref[rows, :].reshape(slabs, SUBLANES, tq) * alpha_ref[hd][None]
        acc_ref[rows, :] = acc.reshape(MLA_V, tq) + pv

    def round_(prev_block, next_block, diagonal=None):
        if prev_block is not None:
            for hd in heads:
                values_rescaled(hd, prev_block)
        for hd in heads:
            softmax_running_max(hd)
        if next_block is not None:
            for hd in heads:
                s_ref[hd] = block_scores(hd, next_block, diagonal)

    @pl.when(fixed_ref[layer] == 0)
    def _():
        m_ref[...] = jnp.full(m_ref.shape, -jnp.inf, F32)
        for hd in heads:
            s_ref[hd] = block_scores(hd, low, "low")
        round_(None, high, "high")

        @pl.when(i == 0)
        def _():
            round_(low, None)

        @pl.when(i > 0)
        def _():
            round_(low, 0)

            def body(t, carry):
                round_(jnp.where(t == 2, high, t - 3), t - 1)
                return carry

            lax.fori_loop(2, high, body, 0)
            round_(low - 2, None)

        for hd in heads:
            values_rescaled(hd, jnp.where(i == 0, high, low - 1))

    for hd in range(MLA_HEADS):
        rows = slice(hd * MLA_V, (hd + 1) * MLA_V)
        out = (acc_ref[rows, :].reshape(slabs, SUBLANES, tq) / l_ref[hd][None]).reshape(MLA_V, tq)
        y_ref[0, rows, :] = out.astype(BF16)


def _attn(fixed, qb, kb, vbt, layer, batch, seq):
    tq, tk = ATTN_Q_TILE, ATTN_TILE
    nq = seq // tq
    hw = MLA_HEADS * HEAD_PAD
    stat = pltpu.VMEM((MLA_HEADS, SUBLANES, tq), F32)
    return pl.pallas_call(
        functools.partial(_attn_body, layer),
        grid=(batch, nq),
        in_specs=[pl.BlockSpec(memory_space=pltpu.SMEM),
                  pl.BlockSpec((tq, hw), lambda b, i: (b * nq + i, 0)),
                  pl.BlockSpec((seq, hw), lambda b, i: (b, 0)),
                  pl.BlockSpec((seq // tk, BRANCH_W, tk), lambda b, i: (b, 0, 0))],
        out_specs=pl.BlockSpec((1, BRANCH_W, tq), lambda b, i: (b * nq + i, 0, 0)),
        out_shape=jax.ShapeDtypeStruct((batch * seq // tq, BRANCH_W, tq), BF16),
        scratch_shapes=[stat, stat, stat,
                        pltpu.VMEM((MLA_HEADS * MLA_V, tq), F32),
                        pltpu.VMEM((MLA_HEADS, tk, tq), F32),
                        pltpu.VMEM((2, MLA_HEADS, tk, tq), BF16)],
        compiler_params=pltpu.CompilerParams(dimension_semantics=("parallel", "arbitrary"),
                                             vmem_limit_bytes=VMEM_LIMIT),
        name="mla_attn",
    )(fixed, qb, kb, vbt)


def _merge_body(x_ref, ya_ref, ob_ref, zb_ref, yc_ref, ng_ref, wgate_ref, bgate_ref, wbr_ref, wout_ref,
                o_ref):
    x = x_ref[...]
    h = (x * _rms(x, D_MODEL) * ng_ref[...]).astype(BF16)
    yb = (ob_ref[0].astype(F32).T * zb_ref[...].astype(F32)).astype(BF16)
    merged = None
    for n, y in enumerate((ya_ref[...], yb, yc_ref[...])):
        cols = slice(n * D_MODEL, (n + 1) * D_MODEL)
        logits = jnp.dot(h, wgate_ref[:, cols], preferred_element_type=F32) + bgate_ref[:, cols]
        term = _sigmoid(logits) * jnp.dot(y, wbr_ref[n], preferred_element_type=F32)
        merged = term if merged is None else merged + term
    o_ref[...] = x + jnp.dot(merged.astype(BF16), wout_ref[...], preferred_element_type=F32)


def _merge(x2, ya, ob, zb, yc, p, layer):
    t = x2.shape[0]
    tm = MERGE_TILE
    row = lambda w: pl.BlockSpec((tm, w), lambda i: (i, 0))
    consts = [p["norm_g"], p["w_gate"], p["b_gate"], p["w_branch"], p["w_out"]]
    return pl.pallas_call(
        _merge_body,
        grid=(t // tm,),
        in_specs=[row(D_MODEL), row(BRANCH_W), pl.BlockSpec((1, BRANCH_W, tm), lambda i: (i, 0, 0)),
                  row(BRANCH_W), row(BRANCH_W)] + [_layer_spec(c, layer) for c in consts],
        out_specs=row(D_MODEL),
        out_shape=jax.ShapeDtypeStruct((t, D_MODEL), F32),
        compiler_params=pltpu.CompilerParams(dimension_semantics=("parallel",),
                                             vmem_limit_bytes=VMEM_LIMIT),
        name="merge",
    )(x2, ya, ob, zb, yc, *consts)


def _head_layout(w, heads):
    lead = w.shape[:-1]
    w = w.reshape(lead + (heads, MLA_QK))
    w = jnp.pad(w, [(0, 0)] * (len(lead) + 1) + [(0, 1)])
    src = jnp.asarray([MLA_QK if s < 0 else s for s in HEAD_LANE_SOURCE], jnp.int32)
    return jnp.take(w, src, axis=-1).reshape(lead + (heads * HEAD_PAD,))


def _pack_params(norm_g, w_in, b_gate, sg_ln_g, sg_ln_b, sg_w, sg_b, mla_cq_g, mla_ckv_g, mla_w_uq,
                 mla_w_ukv, mla_q_g, mla_k_g, gla_w_gate, gla_b_gate, gla_o_g, w_branch, w_out):
    depth, d = w_in.shape[0], w_in.shape[1]
    w = w_in
    row = lambda g: g.reshape(depth, 1, -1)
    kr_block = _head_layout(
        jnp.concatenate([jnp.zeros((depth, d, MLA_NOPE), w.dtype), w[:, :, SRC_KR:SRC_ZB]], axis=2), 1)
    kr_block = kr_block.at[:, :, GATE_LANE:GATE_LANE + GLA_GATE_RANK].set(w[:, :, SRC_GC:SRC_ZC])
    w_mix = jnp.concatenate([w[:, :, :SRC_KR], kr_block, w[:, :, SRC_ZB:SRC_GC],
                             w[:, :, SRC_ZC:GATE_SRC]], axis=2)
    ukv = mla_w_ukv.reshape(depth, MLA_KV_RANK, MLA_HEADS, MLA_NOPE + MLA_V)
    w_uk = _head_layout(jnp.pad(ukv[..., :MLA_NOPE], ((0, 0), (0, 0), (0, 0), (0, MLA_ROPE)))
                        .reshape(depth, MLA_KV_RANK, -1), MLA_HEADS)
    w_uvt = jnp.swapaxes(ukv[..., MLA_NOPE:].reshape(depth, MLA_KV_RANK, -1), 1, 2)
    w_gu = jnp.zeros((depth, LANES, GLA_HEADS * GLA_DK), F32).at[
        :, GATE_LANE:GATE_LANE + GLA_GATE_RANK].set(gla_w_gate)
    q_g = row(_head_layout(mla_q_g, 1)) * (MLA_QK ** -0.5 * math.log2(math.e))
    k_g = row(_head_layout(mla_k_g, 1))
    bound = (MLA_QK * SCORE_BOUND_MARGIN) * (jnp.max(jnp.abs(q_g), axis=(1, 2), keepdims=True)
                                             * jnp.max(jnp.abs(k_g), axis=(1, 2), keepdims=True))
    fixed = bound <= FIXED_REFERENCE_MAX_BOUND
    bias_lane = (jnp.arange(HEAD_PAD) == BIAS_LANE).reshape(1, 1, HEAD_PAD)
    return {
        "norm_g": row(norm_g),
        "w_mix": w_mix.astype(BF16),
        "w_gate": w[:, :, GATE_SRC:].astype(BF16),
        "sg_ln_g": row(sg_ln_g),
        "sg_ln_b": row(sg_ln_b),
        "sg_w": sg_w,
        "sg_bt": jnp.swapaxes(sg_b, 1, 2),
        "cq_g": row(mla_cq_g),
        "ckv_g": row(mla_ckv_g),
        "w_uq": _head_layout(mla_w_uq, MLA_HEADS).astype(BF16),
        "w_uk": w_uk.astype(BF16),
        "w_uvt": w_uvt.astype(BF16),
        "q_g": q_g,
        "k_g": k_g,
        "q_pad": jnp.where(bias_lane & fixed, -bound, 0.0).astype(F32),
        "k_pad": jnp.broadcast_to(jnp.where(bias_lane, 1.0, 0.0).astype(F32), (depth, 1, HEAD_PAD)),
        "fixed": fixed.astype(jnp.int32).reshape(depth),
        "w_gu": w_gu.astype(BF16),
        "b_gu": row(gla_b_gate),
        "b_gate": row(b_gate),
        "o_g": row(gla_o_g),
        "w_branch": w_branch.astype(BF16),
        "w_out": w_out.astype(BF16),
    }


def kernel(x, positions, norm_g, w_in, b_gate, sg_ln_g, sg_ln_b, sg_w, sg_b, mla_cq_g, mla_ckv_g, mla_w_uq, mla_w_ukv, mla_q_g, mla_k_g, gla_w_gate, gla_b_gate, gla_o_g, w_branch, w_out):
    batch, seq, d = x.shape
    assert d == D_MODEL and seq % max(ATTN_Q_TILE, GLA_TILE) == 0
    assert (batch * seq) % IN_TILE == 0 and (batch * seq) % MERGE_TILE == 0
    assert batch % ROWS_PER_STEP == 0
    assert IN_TILE % ATTN_Q_TILE == 0 and IN_TILE % SG_BLOCK == 0
    assert MERGE_TILE == ATTN_Q_TILE
    depth = w_in.shape[0]
    cos_t, sin_t = _rope_tables(positions)
    x2 = x.reshape(batch * seq, d)
    p = _pack_params(norm_g, w_in, b_gate, sg_ln_g, sg_ln_b, sg_w, sg_b, mla_cq_g, mla_ckv_g,
                     mla_w_uq, mla_w_ukv, mla_q_g, mla_k_g, gla_w_gate, gla_b_gate, gla_o_g,
                     w_branch, w_out)
    for l in range(depth):
        ya, qb, kb, vbt, zb, qc, kc, vc, la, zc = _in_proj(x2, cos_t, sin_t, p, l)
        yc = _gla(qc, kc, vc, la, zc, p["o_g"], l, batch, seq)
        ob = _attn(p["fixed"], qb, kb, vbt, l, batch, seq)
        x2 = _merge(x2, ya, ob, zb, yc, p, l)
    return x2.reshape(batch, seq, d)
```

```python
import functools
import math

import jax
import jax.numpy as jnp
from jax import lax
from jax.experimental import pallas as pl
from jax.experimental.pallas import tpu as pltpu

F32 = jnp.float32
BF16 = jnp.bfloat16

D_MODEL = 1024
CHUNK = 64
BRANCH_W = 512
EPS = 1e-6
SG_BLOCK = 128
SG_GROUPS = 4
MLA_HEADS = 8
MLA_NOPE = 64
MLA_ROPE = 32
MLA_QK = MLA_NOPE + MLA_ROPE
MLA_V = 64
MLA_Q_RANK = 256
MLA_KV_RANK = 128
ROPE_THETA = 10000.0
GLA_HEADS = 4
GLA_DK = 64
GLA_DV = 128
GLA_GATE_RANK = 16
GLA_TAU = 16.0

LANES = 128
SUBLANES = 8
HEAD_PAD = LANES
ROPE_HALF = MLA_ROPE // 2
X1_LO = 0
X2_LO = LANES // 2
BIAS_LANE = MLA_QK
GATE_LANE = BIAS_LANE + 1


def _head_lane_source():
    src = [-1] * LANES
    nope_lanes = list(range(ROPE_HALF, X2_LO)) + list(range(X2_LO + ROPE_HALF, MLA_QK))
    for f, lane in enumerate(nope_lanes):
        src[lane] = f
    for f in range(ROPE_HALF):
        src[X1_LO + f] = MLA_NOPE + f
        src[X2_LO + f] = MLA_NOPE + ROPE_HALF + f
    return src


HEAD_LANE_SOURCE = _head_lane_source()

A_U, A_V, A_Z = 0, BRANCH_W, 2 * BRANCH_W
B_CQ = 3 * BRANCH_W
B_CKV = B_CQ + MLA_Q_RANK
B_KR = B_CKV + MLA_KV_RANK
B_Z = B_KR + LANES
C_Q = B_Z + BRANCH_W
C_K = C_Q + GLA_HEADS * GLA_DK
C_V = C_K + GLA_HEADS * GLA_DK
C_Z = C_V + GLA_HEADS * GLA_DV
SRC_KR = B_KR
SRC_ZB = SRC_KR + MLA_ROPE
SRC_GC = SRC_ZB + BRANCH_W + 2 * GLA_HEADS * GLA_DK + GLA_HEADS * GLA_DV
SRC_ZC = SRC_GC + GLA_GATE_RANK
GATE_SRC = SRC_ZC + BRANCH_W

IN_TILE = 1024
MERGE_TILE = 512
ATTN_TILE = 256
ATTN_Q_TILE = 2 * ATTN_TILE
GLA_TILE = 512
GLA_CUMSUM_SPAN = 256
ROWS_PER_STEP = 4
VMEM_LIMIT = 56 * 1024 * 1024

SCORE_BOUND_MARGIN = 1.02
FIXED_REFERENCE_MAX_BOUND = 40.0

NT_DIMS = (((1,), (1,)), ((), ()))
TN_DIMS = (((0,), (0,)), ((), ()))


def _layer_spec(stacked, layer):
    tail = stacked.shape[1:]
    index = (layer,) + (0,) * len(tail)
    return pl.BlockSpec((None,) + tail, lambda *_: index, pipeline_mode=pl.Buffered(1))


def _sigmoid(x):
    return 0.5 * (jnp.tanh(0.5 * x) + 1.0)


def _silu(x):
    return x * _sigmoid(x)


def _gelu_tanh(x):
    c = math.sqrt(2.0 / math.pi)
    return 0.5 * x * (1.0 + jnp.tanh(c * (x + 0.044715 * (x * x * x))))


def _log_sigmoid(x):
    return jnp.minimum(x, 0.0) - jnp.log(1.0 + jnp.exp(-jnp.abs(x)))


def _rms(x, width):
    return lax.rsqrt(jnp.sum(x * x, axis=-1, keepdims=True) * (1.0 / width) + EPS)


def _rope_body(pos_ref, cos_ref, sin_ref):
    tm = pos_ref.shape[-1]
    fidx = lax.broadcasted_iota(jnp.int32, (ROPE_HALF, 1), 0).astype(F32)
    inv = 1.0 / jnp.exp(fidx * (2.0 / MLA_ROPE) * math.log(ROPE_THETA))
    ang = inv * pos_ref[0].astype(F32)
    cos = jnp.cos(ang)
    sin = jnp.sin(ang)
    nope_a = X2_LO - (X1_LO + ROPE_HALF)
    nope_b = MLA_QK - (X2_LO + ROPE_HALF)
    ones = lambda n: jnp.ones((n, tm), F32)
    zeros = lambda n: jnp.zeros((n, tm), F32)
    cos_t = jnp.concatenate([cos, ones(nope_a), cos, ones(nope_b), zeros(LANES - MLA_QK)], axis=0)
    sin_t = jnp.concatenate([-sin, zeros(nope_a), sin, zeros(nope_b + LANES - MLA_QK)], axis=0)
    cos_ref[...] = cos_t.T
    sin_ref[...] = sin_t.T


def _rope_tables(positions):
    t = positions.size
    tm = 1024 if t % 1024 == 0 else ATTN_TILE
    return pl.pallas_call(
        _rope_body,
        grid=(t // tm,),
        in_specs=[pl.BlockSpec((1, 1, tm), lambda i: (i, 0, 0))],
        out_specs=[pl.BlockSpec((tm, LANES), lambda i: (i, 0))] * 2,
        out_shape=[jax.ShapeDtypeStruct((t, LANES), F32)] * 2,
        name="rope_tables",
    )(positions.reshape(t // tm, 1, tm))


def _in_proj_body(x_ref, cos_ref, sin_ref, ng_ref, win_ref, lng_ref, lnb_ref, sgw_ref, sgbt_ref,
                  cqg_ref, ckvg_ref, wuq_ref, wuk_ref, wuvt_ref, qg_ref, kg_ref, qpad_ref, kpad_ref,
                  wgu_ref, bgu_ref,
                  ya_ref, qb_ref, kb_ref, vbt_ref, zb_ref, qc_ref, kc_ref, vc_ref, la_ref, zc_ref):
    tm = x_ref.shape[0]
    x = x_ref[...]
    h = (x * _rms(x, D_MODEL) * ng_ref[...]).astype(BF16)

    def proj(lo, width):
        return jnp.dot(h, win_ref[:, lo:lo + width], preferred_element_type=F32)

    u = _gelu_tanh(proj(A_U, BRANCH_W))
    v = _gelu_tanh(proj(A_V, BRANCH_W))
    mu = jnp.mean(v, axis=-1, keepdims=True)
    vc = v - mu
    vn = vc * lax.rsqrt(jnp.mean(vc * vc, axis=-1, keepdims=True) + EPS)
    vn = (vn * lng_ref[...] + lnb_ref[...]).astype(BF16)
    uz = u * _silu(proj(A_Z, BRANCH_W))
    ri = lax.broadcasted_iota(jnp.int32, (SG_BLOCK, SG_BLOCK), 0) // CHUNK
    ci = lax.broadcasted_iota(jnp.int32, (SG_BLOCK, SG_BLOCK), 1) // CHUNK
    chunk_causal = ri >= ci
    for g in range(SG_GROUPS):
        wg = jnp.where(chunk_causal, sgw_ref[g], 0.0).astype(BF16)
        bias = sgbt_ref[:, g:g + 1]
        cols = slice(g * LANES, (g + 1) * LANES)
        for n in range(tm // SG_BLOCK):
            rows = slice(n * SG_BLOCK, (n + 1) * SG_BLOCK)
            sv = jnp.dot(wg, vn[rows, cols], preferred_element_type=F32) + bias
            ya_ref[rows, cols] = (uz[rows, cols] * sv).astype(BF16)

    cq = proj(B_CQ, MLA_Q_RANK)
    cqn = (cq * _rms(cq, MLA_Q_RANK) * cqg_ref[...]).astype(BF16)
    q = jnp.dot(cqn, wuq_ref[...], preferred_element_type=F32)
    ckv = proj(B_CKV, MLA_KV_RANK)
    ckvn = (ckv * _rms(ckv, MLA_KV_RANK) * ckvg_ref[...]).astype(BF16)
    kn = jnp.dot(ckvn, wuk_ref[...], preferred_element_type=F32)
    for s in range(tm // ATTN_TILE):
        vbt_ref[s] = lax.dot_general(wuvt_ref[...], ckvn[s * ATTN_TILE:(s + 1) * ATTN_TILE], NT_DIMS,
                                     preferred_element_type=F32).astype(BF16)
    kr_glr = proj(B_KR, LANES)
    lane = lax.broadcasted_iota(jnp.int32, (1, LANES), 1)
    kr = jnp.where(lane < MLA_QK, kr_glr, 0.0)
    cos = cos_ref[...]
    sin = sin_ref[...]

    def rope(xg):
        partner = pltpu.roll(xg, LANES // 2, 1)
        return xg * cos + partner * sin

    kg = kg_ref[...]
    kr_roped = rope(kr * kg)
    for hd in range(MLA_HEADS):
        cols = slice(hd * HEAD_PAD, (hd + 1) * HEAD_PAD)
        qh = q[:, cols]
        qb_ref[:, cols] = (rope(qh * _rms(qh, MLA_QK) * qg_ref[...]) + qpad_ref[...]).astype(BF16)
        kh = kn[:, cols]
        kb_ref[:, cols] = ((kh * kg + kr_roped) * _rms(kh + kr, MLA_QK) + kpad_ref[...]).astype(BF16)
    zb_ref[...] = _silu(proj(B_Z, BRANCH_W)).astype(BF16)

    qc_ref[...] = (proj(C_Q, GLA_HEADS * GLA_DK) * (GLA_DK ** -0.5)).astype(BF16)
    kc_ref[...] = proj(C_K, GLA_HEADS * GLA_DK).astype(BF16)
    vc_ref[...] = proj(C_V, GLA_HEADS * GLA_DV).astype(BF16)
    gl = jnp.dot(kr_glr.astype(BF16), wgu_ref[...], preferred_element_type=F32) + bgu_ref[...]
    la_ref[...] = _log_sigmoid(gl) * (1.0 / GLA_TAU)
    zc_ref[...] = _silu(proj(C_Z, BRANCH_W)).astype(BF16)


def _in_proj(x2, cos_t, sin_t, p, layer):
    t = x2.shape[0]
    tm = IN_TILE
    row = lambda w: pl.BlockSpec((tm, w), lambda i: (i, 0))
    consts = [p["norm_g"], p["w_mix"], p["sg_ln_g"], p["sg_ln_b"], p["sg_w"], p["sg_bt"],
              p["cq_g"], p["ckv_g"], p["w_uq"], p["w_uk"], p["w_uvt"], p["q_g"], p["k_g"],
              p["q_pad"], p["k_pad"], p["w_gu"], p["b_gu"]]
    out_widths = [(BRANCH_W, BF16), (MLA_HEADS * HEAD_PAD, BF16), (MLA_HEADS * HEAD_PAD, BF16),
                  ATTN_TILE, (BRANCH_W, BF16),
                  (GLA_HEADS * GLA_DK, BF16), (GLA_HEADS * GLA_DK, BF16), (GLA_HEADS * GLA_DV, BF16),
                  (GLA_HEADS * GLA_DK, F32), (BRANCH_W, BF16)]
    t_spec = lambda n: pl.BlockSpec((tm // n, BRANCH_W, n), lambda i: (i, 0, 0))
    t_shape = lambda n: jax.ShapeDtypeStruct((t // n, BRANCH_W, n), BF16)
    return pl.pallas_call(
        _in_proj_body,
        grid=(t // tm,),
        in_specs=[row(D_MODEL), row(LANES), row(LANES)] + [_layer_spec(c, layer) for c in consts],
        out_specs=[t_spec(o) if isinstance(o, int) else row(o[0]) for o in out_widths],
        out_shape=[t_shape(o) if isinstance(o, int) else jax.ShapeDtypeStruct((t, o[0]), o[1])
                   for o in out_widths],
        compiler_params=pltpu.CompilerParams(dimension_semantics=("parallel",),
                                             vmem_limit_bytes=VMEM_LIMIT),
        name="in_proj",
    )(x2, cos_t, sin_t, *consts)


def _gla_body(q_ref, k_ref, v_ref, la_ref, z_ref, og_ref, y_ref, state_ref):
    @pl.when(pl.program_id(1) == 0)
    def _():
        state_ref[...] = jnp.zeros_like(state_ref)

    n_rows, tc = q_ref.shape[0], q_ref.shape[1]
    kw = GLA_HEADS * GLA_DK
    stack = GLA_HEADS * CHUNK
    span = GLA_CUMSUM_SPAN
    ri = lax.broadcasted_iota(jnp.int32, (span, span), 0)
    ci = lax.broadcasted_iota(jnp.int32, (span, span), 1)
    tri = jnp.where((ri // CHUNK == ci // CHUNK) & (ri >= ci), 1.0, 0.0).astype(BF16)
    si = lax.broadcasted_iota(jnp.int32, (stack, CHUNK), 0) % CHUNK
    sj = lax.broadcasted_iota(jnp.int32, (stack, CHUNK), 1)
    causal = si >= sj
    lane = lax.broadcasted_iota(jnp.int32, (1, kw), 1)
    cums = []
    for r in range(n_rows):
        parts = []
        for s in range(tc // span):
            la = la_ref[r, s * span:(s + 1) * span, :]
            la_hi = la.astype(BF16)
            la_lo = (la - la_hi.astype(F32)).astype(BF16)
            parts.append(jnp.dot(tri, la_hi, preferred_element_type=F32)
                         + jnp.dot(tri, la_lo, preferred_element_type=F32))
        cums.append(jnp.concatenate(parts, axis=0))
    for c in range(tc // CHUNK):
        rows = slice(c * CHUNK, (c + 1) * CHUNK)
        for r in range(n_rows):
            b = cums[r][rows]
            b_last = b[CHUNK - 1:CHUNK, :]
            qt = q_ref[r, rows, :].astype(F32) * jnp.exp(b)
            kf = k_ref[r, rows, :].astype(F32)
            kt = (kf * jnp.exp(-b)).astype(BF16)
            ks = (kf * jnp.exp(b_last - b)).astype(BF16)
            dec = jnp.exp(b_last)
            q_stack = jnp.concatenate(
                [jnp.where((lane >= hd * GLA_DK) & (lane < (hd + 1) * GLA_DK), qt, 0.0).astype(BF16)
                 for hd in range(GLA_HEADS)], axis=0)
            v = v_ref[r, rows, :]
            att = lax.dot_general(q_stack, kt, NT_DIMS, preferred_element_type=F32)
            att = jnp.where(causal, att, 0.0).astype(BF16)
            st = state_ref[r]
            o_inter = lax.dot_general(q_stack, st.astype(BF16), NT_DIMS, preferred_element_type=F32)
            state_ref[r] = st * dec + lax.dot_general(v, ks, TN_DIMS, preferred_element_type=F32)
            for hd in range(GLA_HEADS):
                srows = slice(hd * CHUNK, (hd + 1) * CHUNK)
                cols = slice(hd * GLA_DV, (hd + 1) * GLA_DV)
                o = (jnp.dot(att[srows], v[:, cols], preferred_element_type=F32)
                     + o_inter[srows, cols])
                on = o * _rms(o, GLA_DV) * og_ref[...]
                y_ref[r, rows, cols] = (on * z_ref[r, rows, cols].astype(F32)).astype(BF16)


def _gla(qc, kc, vc, la, zc, o_g, layer, batch, seq):
    tc = GLA_TILE
    rps = ROWS_PER_STEP
    kw = GLA_HEADS * GLA_DK
    vw = GLA_HEADS * GLA_DV
    row = lambda w: pl.BlockSpec((rps, tc, w), lambda b, s: (b, s, 0))
    rows3 = lambda a: a.reshape(batch, seq, a.shape[-1])
    return pl.pallas_call(
        _gla_body,
        grid=(batch // rps, seq // tc),
        in_specs=[row(kw), row(kw), row(vw), row(kw), row(vw), _layer_spec(o_g, layer)],
        out_specs=row(vw),
        out_shape=jax.ShapeDtypeStruct((batch, seq, vw), BF16),
        scratch_shapes=[pltpu.VMEM((rps, vw, kw), F32)],
        compiler_params=pltpu.CompilerParams(dimension_semantics=("parallel", "arbitrary"),
                                             vmem_limit_bytes=VMEM_LIMIT),
        name="gla",
    )(rows3(qc), rows3(kc), rows3(vc), rows3(la), rows3(zc), o_g).reshape(batch * seq, vw)


def _attn_body(layer, fixed_ref, q_ref, k_ref, vt_ref, y_ref, m_ref, l_ref, alpha_ref, acc_ref,
               s_ref, p_ref):
    tq = q_ref.shape[0]
    tk = vt_ref.shape[2]
    i = pl.program_id(1)
    ki = lax.broadcasted_iota(jnp.int32, (tk, tq), 0) // CHUNK
    qi = lax.broadcasted_iota(jnp.int32, (tk, tq), 1) // CHUNK
    visible = {"low": ki <= qi, "high": ki + tk // CHUNK <= qi}
    slabs = MLA_V // SUBLANES

    def all_sublanes(x, op):
        for shift in (4, 2, 1):
            x = op(x, pltpu.roll(x, shift, 0))
        return x

    def reduce_rows(st, op):
        parts = op(st.reshape(4, tk // SUBLANES // 4, SUBLANES, st.shape[-1]), axis=1)
        return op(parts, axis=0)

    def block_scores(hd, j, diagonal, queries=slice(None)):
        start = pl.multiple_of(j * tk, tk)
        cols = slice(hd * HEAD_PAD, (hd + 1) * HEAD_PAD)
        st = lax.dot_general(k_ref[pl.ds(start, tk), cols], q_ref[queries, cols], NT_DIMS,
                             preferred_element_type=F32)
        if diagonal is not None:
            st = jnp.where(visible[diagonal][:, queries], st, -jnp.inf)
        return st

    heads = range(MLA_HEADS)
    low, high = 2 * i, 2 * i + 1
    l_ref[...] = jnp.zeros(l_ref.shape, F32)
    acc_ref[...] = jnp.zeros(acc_ref.shape, F32)

    def scores_exp(hd, j, slot, diagonal=None, queries=slice(None)):
        st = block_scores(hd, j, diagonal, queries)
        p = jnp.exp2(st.reshape(tk // SUBLANES, SUBLANES, st.shape[-1]))
        l_ref[hd, :, queries] = l_ref[hd, :, queries] + all_sublanes(reduce_rows(p, jnp.sum), jnp.add)
        p_ref[slot, hd, :, queries] = p.reshape(st.shape).astype(BF16)

    def values_plain(hd, j, slot, queries=slice(None)):
        rows = slice(hd * MLA_V, (hd + 1) * MLA_V)
        acc_ref[rows, queries] = acc_ref[rows, queries] + jnp.dot(
            vt_ref[j, rows, :], p_ref[slot, hd, :, queries], preferred_element_type=F32)

    @pl.when(fixed_ref[layer] != 0)
    def _():
        upper = slice(tq // 2, tq)
        for hd in heads:
            scores_exp(hd, high, 0, "high", upper)
        for hd in heads:
            scores_exp(hd, low, 1, "low")
        for hd in heads:
            values_plain(hd, high, 0, upper)

        def pair(u):
            for hd in heads:
                values_plain(hd, jnp.where(u == 0, low, 2 * u - 1), 1)
            for hd in heads:
                scores_exp(hd, 2 * u, 0)
            for hd in heads:
                values_plain(hd, 2 * u, 0)
            for hd in heads:
                scores_exp(hd, 2 * u + 1, 1)

        def four_pairs(w, carry):
            for e in range(4):
                pair(4 * w + e)
            return carry

        lax.fori_loop(0, i // 4, four_pairs, 0)
        done = (i // 4) * 4

        @pl.when(i - done >= 2)
        def _():
            pair(done)
            pair(done + 1)

        @pl.when((i - done) % 2 == 1)
        def _():
            pair(i - 1)

        for hd in heads:
            values_plain(hd, jnp.where(i == 0, low, low - 1), 1)

    def softmax_running_max(hd):
        st = s_ref[hd].reshape(tk // SUBLANES, SUBLANES, tq)
        m_prev = m_ref[hd]
        m_new = jnp.maximum(m_prev, all_sublanes(reduce_rows(st, jnp.max), jnp.maximum))
        alpha = jnp.exp2(m_prev - m_new)
        p = jnp.exp2(st - m_new[None])
        l_ref[hd] = alpha * l_ref[hd] + all_sublanes(reduce_rows(p, jnp.sum), jnp.add)
        p_ref[0, hd] = p.reshape(tk, tq).astype(BF16)
        alpha_ref[hd] = alpha
        m_ref[hd] = m_new

    def values_rescaled(hd, j):
        rows = slice(hd * MLA_V, (hd + 1) * MLA_V)
        pv = jnp.dot(vt_ref[j, rows, :], p_ref[0, hd], preferred_element_type=F32)
        acc = acc_ref[rows, :].reshape(slabs, SUBLANES, tq) * alpha_ref[hd][None]
        acc_ref[rows, :] = acc.reshape(MLA_V, tq) + pv

    def round_(prev_block, next_block, diagonal=None):
        if prev_block is not None:
            for hd in heads:
                values_rescaled(hd, prev_block)
        for hd in heads:
            softmax_running_max(hd)
        if next_block is not None:
            for hd in heads:
                s_ref[hd] = block_scores(hd, next_block, diagonal)

    @pl.when(fixed_ref[layer] == 0)
    def _():
        m_ref[...] = jnp.full(m_ref.shape, -jnp.inf, F32)
        for hd in heads:
            s_ref[hd] = block_scores(hd, low, "low")
        round_(None, high, "high")

        @pl.when(i == 0)
        def _():
            round_(low, None)

        @pl.when(i > 0)
        def _():
            round_(low, 0)

            def body(t, carry):
                round_(jnp.where(t == 2, high, t - 3), t - 1)
                return carry

            lax.fori_loop(2, high, body, 0)
            round_(low - 2, None)

        for hd in heads:
            values_rescaled(hd, jnp.where(i == 0, high, low - 1))

    for hd in range(MLA_HEADS):
        rows = slice(hd * MLA_V, (hd + 1) * MLA_V)
        out = (acc_ref[rows, :].reshape(slabs, SUBLANES, tq) / l_ref[hd][None]).reshape(MLA_V, tq)
        y_ref[0, rows, :] = out.astype(BF16)


def _attn(fixed, qb, kb, vbt, layer, batch, seq):
    tq, tk = ATTN_Q_TILE, ATTN_TILE
    nq = seq // tq
    hw = MLA_HEADS * HEAD_PAD
    stat = pltpu.VMEM((MLA_HEADS, SUBLANES, tq), F32)
    return pl.pallas_call(
        functools.partial(_attn_body, layer),
        grid=(batch, nq),
        in_specs=[pl.BlockSpec(memory_space=pltpu.SMEM),
                  pl.BlockSpec((tq, hw), lambda b, i: (b * nq + i, 0)),
                  pl.BlockSpec((seq, hw), lambda b, i: (b, 0)),
                  pl.BlockSpec((seq // tk, BRANCH_W, tk), lambda b, i: (b, 0, 0))],
        out_specs=pl.BlockSpec((1, BRANCH_W, tq), lambda b, i: (b * nq + i, 0, 0)),
        out_shape=jax.ShapeDtypeStruct((batch * seq // tq, BRANCH_W, tq), BF16),
        scratch_shapes=[stat, stat, stat,
                        pltpu.VMEM((MLA_HEADS * MLA_V, tq), F32),
                        pltpu.VMEM((MLA_HEADS, tk, tq), F32),
                        pltpu.VMEM((2, MLA_HEADS, tk, tq), BF16)],
        compiler_params=pltpu.CompilerParams(dimension_semantics=("parallel", "arbitrary"),
                                             vmem_limit_bytes=VMEM_LIMIT),
        name="mla_attn",
    )(fixed, qb, kb, vbt)


def _merge_body(x_ref, ya_ref, ob_ref, zb_ref, yc_ref, ng_ref, wgate_ref, bgate_ref, wbr_ref, wout_ref,
                o_ref):
    x = x_ref[...]
    h = (x * _rms(x, D_MODEL) * ng_ref[...]).astype(BF16)
    yb = (ob_ref[0].T.astype(F32) * zb_ref[...].astype(F32)).astype(BF16)
    merged = None
    for n, y in enumerate((ya_ref[...], yb, yc_ref[...])):
        cols = slice(n * D_MODEL, (n + 1) * D_MODEL)
        logits = jnp.dot(h, wgate_ref[:, cols], preferred_element_type=F32) + bgate_ref[:, cols]
        term = _sigmoid(logits) * jnp.dot(y, wbr_ref[n], preferred_element_type=F32)
        merged = term if merged is None else merged + term
    o_ref[...] = x + jnp.dot(merged.astype(BF16), wout_ref[...], preferred_element_type=F32)


def _merge(x2, ya, ob, zb, yc, p, layer):
    t = x2.shape[0]
    tm = MERGE_TILE
    row = lambda w: pl.BlockSpec((tm, w), lambda i: (i, 0))
    consts = [p["norm_g"], p["w_gate"], p["b_gate"], p["w_branch"], p["w_out"]]
    return pl.pallas_call(
        _merge_body,
        grid=(t // tm,),
        in_specs=[row(D_MODEL), row(BRANCH_W), pl.BlockSpec((1, BRANCH_W, tm), lambda i: (i, 0, 0)),
                  row(BRANCH_W), row(BRANCH_W)] + [_layer_spec(c, layer) for c in consts],
        out_specs=row(D_MODEL),
        out_shape=jax.ShapeDtypeStruct((t, D_MODEL), F32),
        compiler_params=pltpu.CompilerParams(dimension_semantics=("parallel",),
                                             vmem_limit_bytes=VMEM_LIMIT),
        name="merge",
    )(x2, ya, ob, zb, yc, *consts)


def _head_layout(w, heads):
    lead = w.shape[:-1]
    w = w.reshape(lead + (heads, MLA_QK))
    w = jnp.pad(w, [(0, 0)] * (len(lead) + 1) + [(0, 1)])
    src = jnp.asarray([MLA_QK if s < 0 else s for s in HEAD_LANE_SOURCE], jnp.int32)
    return jnp.take(w, src, axis=-1).reshape(lead + (heads * HEAD_PAD,))


def _pack_params(norm_g, w_in, b_gate, sg_ln_g, sg_ln_b, sg_w, sg_b, mla_cq_g, mla_ckv_g, mla_w_uq,
                 mla_w_ukv, mla_q_g, mla_k_g, gla_w_gate, gla_b_gate, gla_o_g, w_branch, w_out):
    depth, d = w_in.shape[0], w_in.shape[1]
    w = w_in
    row = lambda g: g.reshape(depth, 1, -1)
    kr_block = _head_layout(
        jnp.concatenate([jnp.zeros((depth, d, MLA_NOPE), w.dtype), w[:, :, SRC_KR:SRC_ZB]], axis=2), 1)
    kr_block = kr_block.at[:, :, GATE_LANE:GATE_LANE + GLA_GATE_RANK].set(w[:, :, SRC_GC:SRC_ZC])
    w_mix = jnp.concatenate([w[:, :, :SRC_KR], kr_block, w[:, :, SRC_ZB:SRC_GC],
                             w[:, :, SRC_ZC:GATE_SRC]], axis=2)
    ukv = mla_w_ukv.reshape(depth, MLA_KV_RANK, MLA_HEADS, MLA_NOPE + MLA_V)
    w_uk = _head_layout(jnp.pad(ukv[..., :MLA_NOPE], ((0, 0), (0, 0), (0, 0), (0, MLA_ROPE)))
                        .reshape(depth, MLA_KV_RANK, -1), MLA_HEADS)
    w_uvt = jnp.swapaxes(ukv[..., MLA_NOPE:].reshape(depth, MLA_KV_RANK, -1), 1, 2)
    w_gu = jnp.zeros((depth, LANES, GLA_HEADS * GLA_DK), F32).at[
        :, GATE_LANE:GATE_LANE + GLA_GATE_RANK].set(gla_w_gate)
    q_g = row(_head_layout(mla_q_g, 1)) * (MLA_QK ** -0.5 * math.log2(math.e))
    k_g = row(_head_layout(mla_k_g, 1))
    bound = (MLA_QK * SCORE_BOUND_MARGIN) * (jnp.max(jnp.abs(q_g), axis=(1, 2), keepdims=True)
                                             * jnp.max(jnp.abs(k_g), axis=(1, 2), keepdims=True))
    fixed = bound <= FIXED_REFERENCE_MAX_BOUND
    bias_lane = (jnp.arange(HEAD_PAD) == BIAS_LANE).reshape(1, 1, HEAD_PAD)
    return {
        "norm_g": row(norm_g),
        "w_mix": w_mix.astype(BF16),
        "w_gate": w[:, :, GATE_SRC:].astype(BF16),
        "sg_ln_g": row(sg_ln_g),
        "sg_ln_b": row(sg_ln_b),
        "sg_w": sg_w,
        "sg_bt": jnp.swapaxes(sg_b, 1, 2),
        "cq_g": row(mla_cq_g),
        "ckv_g": row(mla_ckv_g),
        "w_uq": _head_layout(mla_w_uq, MLA_HEADS).astype(BF16),
        "w_uk": w_uk.astype(BF16),
        "w_uvt": w_uvt.astype(BF16),
        "q_g": q_g,
        "k_g": k_g,
        "q_pad": jnp.where(bias_lane & fixed, -bound, 0.0).astype(F32),
        "k_pad": jnp.broadcast_to(jnp.where(bias_lane, 1.0, 0.0).astype(F32), (depth, 1, HEAD_PAD)),
        "fixed": fixed.astype(jnp.int32).reshape(depth),
        "w_gu": w_gu.astype(BF16),
        "b_gu": row(gla_b_gate),
        "b_gate": row(b_gate),
        "o_g": row(gla_o_g),
        "w_branch": w_branch.astype(BF16),
        "w_out": w_out.astype(BF16),
    }


def kernel(x, positions, norm_g, w_in, b_gate, sg_ln_g, sg_ln_b, sg_w, sg_b, mla_cq_g, mla_ckv_g, mla_w_uq, mla_w_ukv, mla_q_g, mla_k_g, gla_w_gate, gla_b_gate, gla_o_g, w_branch, w_out):
    batch, seq, d = x.shape
    assert d == D_MODEL and seq % max(ATTN_Q_TILE, GLA_TILE) == 0
    assert (batch * seq) % IN_TILE == 0 and (batch * seq) % MERGE_TILE == 0
    assert batch % ROWS_PER_STEP == 0
    assert IN_TILE % ATTN_Q_TILE == 0 and IN_TILE % SG_BLOCK == 0
    assert MERGE_TILE == ATTN_Q_TILE
    depth = w_in.shape[0]
    cos_t, sin_t = _rope_tables(positions)
    x2 = x.reshape(batch * seq, d)
    p = _pack_params(norm_g, w_in, b_gate, sg_ln_g, sg_ln_b, sg_w, sg_b, mla_cq_g, mla_ckv_g,
                     mla_w_uq, mla_w_ukv, mla_q_g, mla_k_g, gla_w_gate, gla_b_gate, gla_o_g,
                     w_branch, w_out)
    for l in range(depth):
        ya, qb, kb, vbt, zb, qc, kc, vc, la, zc = _in_proj(x2, cos_t, sin_t, p, l)
        yc = _gla(qc, kc, vc, la, zc, p["o_g"], l, batch, seq)
        ob = _attn(p["fixed"], qb, kb, vbt, l, batch, seq)
        x2 = _merge(x2, ya, ob, zb, yc, p, l)
    return x2.reshape(batch, seq, d)
```
